```python
import math
import jax, jax.numpy as jnp
from jax import lax
import numpy as np

D_MODEL = 1024
BATCH = 2
SEQ = 8192
DEPTH = 4

NSA_WIDTH = D_MODEL // 2
POOL_WIDTH = D_MODEL // 4
SGU_WIDTH = D_MODEL // 4
MIX_WIDTH = NSA_WIDTH + POOL_WIDTH + SGU_WIDTH

NSA_HEADS = 8
NSA_KV_GROUPS = 2
HEADS_PER_GROUP = NSA_HEADS // NSA_KV_GROUPS
HEAD_DIM = NSA_WIDTH // NSA_HEADS
KV_WIDTH = NSA_KV_GROUPS * HEAD_DIM
GATE_WIDTH = NSA_HEADS * 3
CMP_STRIDE = 16
CMP_LEN = 2 * CMP_STRIDE
CMP_HIDDEN = 2 * HEAD_DIM
SEL_BLOCK = 64
SEL_TOPK = 16
N_LOCAL_SEL = 2
SEL_RATIO = SEL_BLOCK // CMP_STRIDE
SEL_AGG = (1.0, 2.0, 2.0, 2.0, 1.0)
FORCE_SCORE = 1e9
WINDOW = 512
Q_BLOCK = 128

REL_BUCKETS = 32
REL_MAX_DIST = 1024

POOL_WINDOWS = (2, 4, 8, 16)
POOL_GROUPS = len(POOL_WINDOWS)
POOL_CH = POOL_WIDTH // POOL_GROUPS

SGU_GROUPS = 4
SGU_CH = SGU_WIDTH // SGU_GROUPS
SGU_CHUNK = 128

IN_WIDTHS = (NSA_WIDTH, KV_WIDTH, KV_WIDTH, KV_WIDTH, KV_WIDTH, KV_WIDTH, KV_WIDTH,
             GATE_WIDTH, POOL_WIDTH, SGU_WIDTH, SGU_WIDTH)
IN_WIDTH = sum(IN_WIDTHS)
IN_SPLITS = tuple(int(s) for s in np.cumsum(IN_WIDTHS)[:-1])

MEM_LEN = 256
MEM_HEADS = 4
MEM_HEAD_DIM = D_MODEL // MEM_HEADS

FFN_HIDDEN = ((11 * D_MODEL // 4) // 128) * 128
CONV_WIDTH = 3

EPS = 1e-6
F32 = jnp.float32

kernel_name = 'hymba_nsa_pool_sgu_hybrid'


def rms_norm(x, g):
    xf = x.astype(F32)
    y = xf * lax.rsqrt(jnp.mean(xf * xf, axis=-1, keepdims=True) + EPS)
    return (y * g.astype(F32)).astype(x.dtype)


def layer_norm(x, g, b):
    xf = x.astype(F32)
    mu = jnp.mean(xf, axis=-1, keepdims=True)
    var = jnp.mean(jnp.square(xf - mu), axis=-1, keepdims=True)
    return ((xf - mu) * lax.rsqrt(var + EPS) * g.astype(F32) + b.astype(F32)).astype(x.dtype)


def t5_bucket(dist):
    n = jnp.maximum(dist, 0)
    max_exact = REL_BUCKETS // 2
    nf = jnp.maximum(n, 1).astype(F32)
    large = max_exact + (jnp.log(nf / max_exact) / math.log(REL_MAX_DIST / max_exact)
                         * (REL_BUCKETS - max_exact)).astype(jnp.int32)
    large = jnp.minimum(large, REL_BUCKETS - 1)
    return jnp.where(n < max_exact, n, large)


def masked_softmax(s, mask):
    s = jnp.where(mask, s.astype(F32), -jnp.inf)
    m = jnp.max(s, axis=-1, keepdims=True)
    m = jnp.where(jnp.isfinite(m), m, 0.0)
    e = jnp.where(mask, jnp.exp(s - m), 0.0)
    return e / jnp.maximum(jnp.sum(e, axis=-1, keepdims=True), 1.0)


def compress(kv, pos, w1, b1, w2, b2):
    B, S, G, dk = kv.shape
    r = kv.reshape(B, S // CMP_STRIDE, CMP_STRIDE, G, dk)
    blocks = jnp.concatenate([r[:, :-1], r[:, 1:]], axis=2) + pos[None, None, :, None, :]
    h = jax.nn.gelu(jnp.einsum('bclgd,ldh->bcgh', blocks, w1) + b1)
    return jnp.einsum('bcgh,hd->bcgd', h, w2) + b2


def nsa_mixer(q, k_cmp, v_cmp, k_sel, v_sel, k_win, v_win, gates, rel_bias,
              cmp_pos, cmp_w1, cmp_b1, cmp_w2, cmp_b2):
    B, S, G, Hg, dk = q.shape
    scale = dk ** -0.5
    kc = compress(k_cmp, cmp_pos[0], cmp_w1[0], cmp_b1[0], cmp_w2[0], cmp_b2[0])
    vc = compress(v_cmp, cmp_pos[1], cmp_w1[1], cmp_b1[1], cmp_w2[1], cmp_b2[1])
    NC = kc.shape[1]
    NS = S // SEL_BLOCK
    topk = min(SEL_TOPK, NS)
    ksb = jnp.transpose(k_sel.reshape(B, NS, SEL_BLOCK, G, dk), (0, 3, 1, 2, 4))
    vsb = jnp.transpose(v_sel.reshape(B, NS, SEL_BLOCK, G, dk), (0, 3, 1, 2, 4))
    kwp = jnp.pad(k_win, ((0, 0), (WINDOW, 0), (0, 0), (0, 0)))
    vwp = jnp.pad(v_win, ((0, 0), (WINDOW, 0), (0, 0), (0, 0)))
    cmp_end = jnp.arange(NC) * CMP_STRIDE + CMP_LEN - 1
    sel_start = jnp.arange(NS) * SEL_BLOCK
    jj = jnp.arange(NS)
    bias_g = jnp.transpose(rel_bias.reshape(REL_BUCKETS, G, Hg), (1, 0, 2))
    gidx = jnp.arange(G)
    bidx = jnp.arange(B)
    ti = jnp.arange(Q_BLOCK)
    wj = jnp.arange(Q_BLOCK + WINDOW)
    wdist = ti[:, None] + WINDOW - wj[None, :]
    win_band = (wdist >= 0) & (wdist < WINDOW)
    win_bias = jnp.transpose(rel_bias[t5_bucket(wdist)].reshape(Q_BLOCK, Q_BLOCK + WINDOW, G, Hg), (2, 3, 0, 1))
    nblk = S // Q_BLOCK
    qb = jnp.moveaxis(q.reshape(B, nblk, Q_BLOCK, G, Hg, dk), 1, 0)
    gb = jnp.moveaxis(gates.reshape(B, nblk, Q_BLOCK, G, Hg, 3), 1, 0)

    def block(args):
        qi, gi, i = args
        s0 = i * Q_BLOCK
        t = s0 + ti
        qs = qi * scale
        sc = jnp.einsum('btghd,bcgd->bghtc', qs, kc)
        cdist = t[:, None] - cmp_end[None, :]
        cbias = jnp.transpose(rel_bias[t5_bucket(cdist)].reshape(Q_BLOCK, NC, G, Hg), (2, 3, 0, 1))
        pc = masked_softmax(sc + cbias, cdist >= 0)
        o_cmp = jnp.einsum('bghtc,bcgd->btghd', pc.astype(vc.dtype), vc)
        imp = jnp.pad(jnp.sum(pc, axis=2), ((0, 0), (0, 0), (0, 0), (1, 1)))
        p_sel = sum(w * imp[..., m:m + SEL_RATIO * NS:SEL_RATIO] for m, w in enumerate(SEL_AGG))
        blk_q = t // SEL_BLOCK
        rel_blk = blk_q[:, None] - jj[None, :]
        svalid = sel_start[None, :] <= t[:, None]
        forced = (jj[None, :] == 0) | ((rel_blk >= 0) & (rel_blk < N_LOCAL_SEL))
        score = jnp.where(forced, FORCE_SCORE, jnp.where(svalid, p_sel, -1.0))
        _, sidx = lax.top_k(score, topk)
        ks = ksb[bidx[:, None, None, None], gidx[None, :, None, None], sidx]
        vs = vsb[bidx[:, None, None, None], gidx[None, :, None, None], sidx]
        spos = sidx[..., None] * SEL_BLOCK + jnp.arange(SEL_BLOCK)
        sdist = t[None, None, :, None, None] - spos
        sbias = bias_g[gidx[None, :, None, None, None], t5_bucket(sdist)]
        ss = jnp.einsum('btghd,bgtkld->bghtkl', qs, ks) + jnp.transpose(sbias, (0, 1, 5, 2, 3, 4))
        nsel = topk * SEL_BLOCK
        ps = masked_softmax(ss.reshape(B, G, Hg, Q_BLOCK, nsel), (sdist >= 0).reshape(B, G, 1, Q_BLOCK, nsel))
        o_sel = jnp.einsum('bghtn,bgtnd->btghd', ps.astype(vs.dtype), vs.reshape(B, G, Q_BLOCK, nsel, dk))
        kw = lax.dynamic_slice_in_dim(kwp, s0, Q_BLOCK + WINDOW, axis=1)
        vw = lax.dynamic_slice_in_dim(vwp, s0, Q_BLOCK + WINDOW, axis=1)
        wpos = s0 - WINDOW + wj
        sw = jnp.einsum('btghd,bsgd->bghts', qs, kw) + win_bias
        pw = masked_softmax(sw, win_band & (wpos >= 0)[None, :])
        o_win = jnp.einsum('bghts,bsgd->btghd', pw.astype(vw.dtype), vw)
        return gi[..., 0:1] * o_cmp + gi[..., 1:2] * o_sel + gi[..., 2:3] * o_win

    out = lax.map(block, (qb, gb, jnp.arange(nblk)))
    return jnp.moveaxis(out, 0, 1).reshape(B, S, G * Hg * dk)


def pool_mixer(p, w, scale):
    B, S, C = p.shape
    pf = p.astype(F32)
    cs = jnp.pad(jnp.cumsum(pf, axis=1), ((0, 0), (1, 0), (0, 0)))
    t = jnp.arange(S)
    outs = []
    for gi, win in enumerate(POOL_WINDOWS):
        seg = cs[:, :, gi * POOL_CH:(gi + 1) * POOL_CH]
        lo = jnp.maximum(t + 1 - win, 0)
        cnt = jnp.minimum(t + 1, win).astype(F32)
        outs.append((seg[:, 1:] - seg[:, lo]) / cnt[None, :, None] - pf[:, :, gi * POOL_CH:(gi + 1) * POOL_CH])
    d = jnp.stack(outs, axis=2).astype(p.dtype)
    return jnp.einsum('bsgc,gcd->bsgd', d, w).reshape(B, S, C) * scale


def sgu_mixer(u, v, norm_g, norm_b, w_s, b_s):
    B, S, C = u.shape
    u = jax.nn.gelu(u)
    v = layer_norm(jax.nn.gelu(v), norm_g, norm_b)
    vc = v.reshape(B, S // SGU_CHUNK, SGU_CHUNK, SGU_GROUPS, SGU_CH)
    causal = jnp.tril(jnp.ones((SGU_CHUNK, SGU_CHUNK), dtype=bool))
    ws = jnp.where(causal, w_s, 0.0)
    mixed = jnp.einsum('hts,bnshc->bnthc', ws, vc) + jnp.transpose(b_s)[:, :, None]
    return u * mixed.reshape(B, S, C)


def memory_attention(h, mem_n, wq, wk, wv, wo):
    B, S, D = h.shape
    M = mem_n.shape[1]
    q = (h @ wq).reshape(B, S, MEM_HEADS, MEM_HEAD_DIM)
    k = (mem_n @ wk).reshape(B, M, MEM_HEADS, MEM_HEAD_DIM)
    v = (mem_n @ wv).reshape(B, M, MEM_HEADS, MEM_HEAD_DIM)
    s = jnp.einsum('bshd,bmhd->bhsm', q, k).astype(F32) * (MEM_HEAD_DIM ** -0.5)
    p = jax.nn.softmax(s, axis=-1).astype(v.dtype)
    return jnp.einsum('bhsm,bmhd->bshd', p, v).reshape(B, S, D) @ wo


def conv_ffn(h, w_up, conv_w, conv_b, w_down):
    a = h @ w_up
    a = lax.conv_general_dilated(a, conv_w[:, None, :], window_strides=(1,),
                                 padding=[(CONV_WIDTH - 1, 0)],
                                 dimension_numbers=('NWC', 'WIO', 'NWC'),
                                 feature_group_count=a.shape[-1]) + conv_b
    gate, val = jnp.split(a, 2, axis=-1)
    return (jax.nn.gelu(gate) * val) @ w_down


def setup_inputs(seed: int = 0) -> dict:
    key = jax.random.key(seed)
    ks = iter(jax.random.split(key, 40))

    def nrm(shape, scale):
        return jax.random.normal(next(ks), shape, F32) * scale

    def gain(shape):
        return 1.0 + 0.1 * jax.random.normal(next(ks), shape, F32)

    L, D, H2 = DEPTH, D_MODEL, 2 * FFN_HIDDEN
    return {
        'x': nrm((BATCH, SEQ, D), 1.0),
        'mem': nrm((BATCH, MEM_LEN, D), 1.0),
        'rel_bias': nrm((REL_BUCKETS, NSA_HEADS), 0.2),
        'mix_norm_pre': gain((L, D)),
        'mix_norm_post': gain((L, D)),
        'w_in': nrm((L, D, IN_WIDTH), D ** -0.5),
        'cmp_pos': nrm((L, 2, CMP_LEN, HEAD_DIM), 0.1),
        'cmp_w1': nrm((L, 2, CMP_LEN, HEAD_DIM, CMP_HIDDEN), (CMP_LEN * HEAD_DIM) ** -0.5),
        'cmp_b1': nrm((L, 2, CMP_HIDDEN), 0.01),
        'cmp_w2': nrm((L, 2, CMP_HIDDEN, HEAD_DIM), CMP_HIDDEN ** -0.5),
        'cmp_b2': nrm((L, 2, HEAD_DIM), 0.01),
        'pool_w': nrm((L, POOL_GROUPS, POOL_CH, POOL_CH), POOL_CH ** -0.5),
        'pool_scale': gain((L, POOL_WIDTH)),
        'sgu_norm_g': gain((L, SGU_WIDTH)),
        'sgu_norm_b': nrm((L, SGU_WIDTH), 0.01),
        'sgu_w': nrm((L, SGU_GROUPS, SGU_CHUNK, SGU_CHUNK), SGU_CHUNK ** -0.5),
        'sgu_b': gain((L, SGU_GROUPS, SGU_CHUNK)),
        'w_out': nrm((L, MIX_WIDTH, D), MIX_WIDTH ** -0.5),
        'mem_norm_pre': gain((L, D)),
        'mem_norm_kv': gain((L, D)),
        'mem_norm_post': gain((L, D)),
        'w_mq': nrm((L, D, D), D ** -0.5),
        'w_mk': nrm((L, D, D), D ** -0.5),
        'w_mv': nrm((L, D, D), D ** -0.5),
        'w_mo': nrm((L, D, D), D ** -0.5),
        'ffn_norm_pre': gain((L, D)),
        'ffn_norm_post': gain((L, D)),
        'w_up': nrm((L, D, H2), D ** -0.5),
        'conv_w': nrm((L, CONV_WIDTH, H2), CONV_WIDTH ** -0.5),
        'conv_b': nrm((L, H2), 0.01),
        'w_down': nrm((L, FFN_HIDDEN, D), FFN_HIDDEN ** -0.5),
    }


def reference(x, mem, rel_bias, mix_norm_pre, mix_norm_post, w_in, cmp_pos, cmp_w1, cmp_b1,
              cmp_w2, cmp_b2, pool_w, pool_scale, sgu_norm_g, sgu_norm_b, sgu_w, sgu_b, w_out,
              mem_norm_pre, mem_norm_kv, mem_norm_post, w_mq, w_mk, w_mv, w_mo,
              ffn_norm_pre, ffn_norm_post, w_up, conv_w, conv_b, w_down):
    B, S, D = x.shape
    G, Hg, dk = NSA_KV_GROUPS, HEADS_PER_GROUP, HEAD_DIM
    for l in range(DEPTH):
        h = rms_norm(x, mix_norm_pre[l])
        proj = h @ w_in[l]
        q, kc, vc, ksl, vsl, kw, vw, gt, pin, u, v = jnp.split(proj, IN_SPLITS, axis=-1)
        gates = jax.nn.sigmoid(gt.astype(F32)).astype(x.dtype).reshape(B, S, G, Hg, 3)
        kv = lambda a: a.reshape(B, S, G, dk)
        out_a = nsa_mixer(q.reshape(B, S, G, Hg, dk), kv(kc), kv(vc), kv(ksl), kv(vsl), kv(kw), kv(vw),
                          gates, rel_bias, cmp_pos[l], cmp_w1[l], cmp_b1[l], cmp_w2[l], cmp_b2[l])
        out_b = pool_mixer(pin, pool_w[l], pool_scale[l])
        out_c = sgu_mixer(u, v, sgu_norm_g[l], sgu_norm_b[l], sgu_w[l], sgu_b[l])
        mix = jnp.concatenate([out_a, out_b, out_c], axis=-1) @ w_out[l]
        x = x + rms_norm(mix, mix_norm_post[l])
        h = rms_norm(x, mem_norm_pre[l])
        m = rms_norm(mem, mem_norm_kv[l])
        x = x + rms_norm(memory_attention(h, m, w_mq[l], w_mk[l], w_mv[l], w_mo[l]), mem_norm_post[l])
        h = rms_norm(x, ffn_norm_pre[l])
        x = x + rms_norm(conv_ffn(h, w_up[l], conv_w[l], conv_b[l], w_down[l]), ffn_norm_post[l])
    return x
```

```python
import functools
import math

import numpy as np
import jax
import jax.numpy as jnp
from jax import lax
from jax.experimental import pallas as pl
from jax.experimental.pallas import tpu as pltpu

F32 = jnp.float32
BF16 = jnp.bfloat16

D_MODEL = 1024
NSA_WIDTH = 512
POOL_WIDTH = 256
SGU_WIDTH = 256
NSA_HEADS = 8
KV_GROUPS = 2
HEADS_PER_GROUP = 4
HEAD_DIM = 64
KV_WIDTH = 128
GATE_WIDTH = 24
CMP_STRIDE = 16
CMP_LEN = 32
CMP_HIDDEN = 128
SEL_BLOCK = 64
SEL_TOPK = 16
N_LOCAL_SEL = 2
SEL_AGG = (1.0, 2.0, 2.0, 2.0, 1.0)
FORCE_SCORE = 1e9
WINDOW = 512
REL_BUCKETS = 32
REL_MAX_DIST = 1024
POOL_WINDOWS = (2, 4, 8, 16)
POOL_CH = 64
SGU_GROUPS = 4
SGU_CH = 64
SGU_CHUNK = 128
MEM_HEADS = 4
MEM_HEAD_DIM = 256
FFN_HIDDEN = 2816
EPS = 1e-6

LANES = 128
SUBLANES = 8
BF16_ROWS = 16
VMEM_LIMIT = 56 * 1024 * 1024

T_IN = 512
T_CMP = 256
T_ATT = 256
T_OUT = 512
T_MIX = 512
T_MEM = 512
T_FFN = 512
F_CHUNK = 256
HALO = 16

MASK_BIAS = -float(2 ** 30)
NEG_INIT = -1e30

_Q_OFF = 0
_KSEL_OFF = 1024
_KWIN_OFF = 1152
_KCMP_OFF = 1280
_VCMP_OFF = 1408
_VSEL_OFF = 1536
_VWIN_OFF = 1792
_GATE_OFF = 2048
_POOL_OFF = 2176
_U_OFF = 2432
_V_OFF = 2688
IN_COLS = 2944


def _bucket_table(n_max):
    n = np.arange(n_max)
    max_exact = REL_BUCKETS // 2
    nf = np.maximum(n, 1).astype(np.float64)
    large = max_exact + (np.log(nf / max_exact) / math.log(REL_MAX_DIST / max_exact)
                         * (REL_BUCKETS - max_exact)).astype(np.int32)
    large = np.minimum(large, REL_BUCKETS - 1)
    return np.where(n < max_exact, n, large).astype(np.int32)


_BUCKETS = _bucket_table(16384)
FAR_DIST = int(np.argmax(_BUCKETS == REL_BUCKETS - 1))


def _in_proj_columns():
    idx = np.zeros((IN_COLS,), np.int32)
    msk = np.zeros((IN_COLS,), np.float32)

    def put(dst, src, n):
        idx[dst:dst + n] = np.arange(src, src + n)
        msk[dst:dst + n] = 1.0

    for h in range(NSA_HEADS):
        g = h // HEADS_PER_GROUP
        put(_Q_OFF + h * LANES + g * HEAD_DIM, h * HEAD_DIM, HEAD_DIM)
    base = NSA_WIDTH
    put(_KCMP_OFF, base, KV_WIDTH)
    put(_VCMP_OFF, base + KV_WIDTH, KV_WIDTH)
    put(_KSEL_OFF, base + 2 * KV_WIDTH, KV_WIDTH)
    for g in range(KV_GROUPS):
        put(_VSEL_OFF + g * LANES, base + 3 * KV_WIDTH + g * HEAD_DIM, HEAD_DIM)
    put(_KWIN_OFF, base + 4 * KV_WIDTH, KV_WIDTH)
    for g in range(KV_GROUPS):
        put(_VWIN_OFF + g * LANES, base + 5 * KV_WIDTH + g * HEAD_DIM, HEAD_DIM)
    base += 6 * KV_WIDTH
    put(_GATE_OFF, base, GATE_WIDTH)
    base += GATE_WIDTH
    put(_POOL_OFF, base, POOL_WIDTH)
    put(_U_OFF, base + POOL_WIDTH, SGU_WIDTH)
    put(_V_OFF, base + POOL_WIDTH + SGU_WIDTH, SGU_WIDTH)
    return idx, msk


_IN_IDX, _IN_MASK = _in_proj_columns()


def _log2(n):
    assert n & (n - 1) == 0
    return n.bit_length() - 1


def _params(*sem):
    return pltpu.CompilerParams(dimension_semantics=sem, vmem_limit_bytes=VMEM_LIMIT)


def _rms(x, g):
    return x * lax.rsqrt(jnp.mean(x * x, axis=-1, keepdims=True) + EPS) * g


def _gelu(x):
    c = math.sqrt(2.0 / math.pi)
    return x * (0.5 * (1.0 + jnp.tanh(c * (x + 0.044715 * (x * x * x)))))


def _dot(a, b):
    return jnp.dot(a, b, preferred_element_type=F32)


def _dot_nt(a, b):
    return lax.dot_general(a, b, (((1,), (1,)), ((), ())), preferred_element_type=F32)


def _split3(x):
    hi = x.astype(BF16)
    r1 = x - hi.astype(F32)
    mid = r1.astype(BF16)
    lo = (r1 - mid.astype(F32)).astype(BF16)
    return hi, mid, lo


def _dot_f32(x, w_bf16):
    hi, mid, lo = _split3(x)
    return _dot(hi, w_bf16) + _dot(mid, w_bf16) + _dot(lo, w_bf16)


def _dot_f32_left(w_bf16, x):
    hi, mid, lo = _split3(x)
    return _dot(w_bf16, hi) + _dot(w_bf16, mid) + _dot(w_bf16, lo)


def _in_proj_kernel(x_ref, g_ref, w_ref, qa_ref, ksel_ref, kwin_ref, kcmp_ref, vcmp_ref,
                    vsel_ref, vwin_ref, gate_ref, pool_ref, u_ref, v_ref):
    t = x_ref.shape[1]
    h = _rms(x_ref[0], g_ref[...]).astype(BF16)

    def proj(off, n):
        return _dot(h, w_ref[:, off:off + n])

    scale = HEAD_DIM ** -0.5
    for hd in range(NSA_HEADS):
        qa_ref[0, hd] = (proj(_Q_OFF + hd * LANES, LANES) * scale).astype(BF16)

    row = pl.program_id(1) * t + lax.broadcasted_iota(jnp.int32, (t, LANES), 0)
    lane = lax.broadcasted_iota(jnp.int32, (t, LANES), 1)
    ksel_ref[0, :, :LANES] = proj(_KSEL_OFF, LANES).astype(BF16)
    blk = jnp.right_shift(row, _log2(SEL_BLOCK))
    ksel_ref[0, :, LANES:] = jnp.where(blk == lane, 1.0, 0.0).astype(BF16)
    kwin_ref[0] = proj(_KWIN_OFF, LANES).astype(BF16)
    kcmp_ref[0] = proj(_KCMP_OFF, LANES).astype(BF16)
    vcmp_ref[0] = proj(_VCMP_OFF, LANES).astype(BF16)
    for g in range(KV_GROUPS):
        vsel_ref[0, g] = jnp.where(lane == HEAD_DIM, 1.0,
                                   proj(_VSEL_OFF + g * LANES, LANES)).astype(BF16)
        vwin_ref[0, g] = jnp.where(lane == HEAD_DIM, 1.0,
                                   proj(_VWIN_OFF + g * LANES, LANES)).astype(BF16)
    gate_ref[0] = jax.nn.sigmoid(proj(_GATE_OFF, LANES))
    pool_ref[0] = proj(_POOL_OFF, POOL_WIDTH)
    u_ref[0] = proj(_U_OFF, SGU_WIDTH)
    v_ref[0] = proj(_V_OFF, SGU_WIDTH)


def _in_proj(x, g, w):
    b, s, d = x.shape
    t = T_IN
    grid = (b, s // t)
    row3 = lambda n: pl.BlockSpec((1, t, n), lambda bi, i: (bi, i, 0))
    out_shape = (
        jax.ShapeDtypeStruct((b, NSA_HEADS, s, LANES), BF16),
        jax.ShapeDtypeStruct((b, s, 2 * LANES), BF16),
        jax.ShapeDtypeStruct((b, s, LANES), BF16),
        jax.ShapeDtypeStruct((b, s, LANES), BF16),
        jax.ShapeDtypeStruct((b, s, LANES), BF16),
        jax.ShapeDtypeStruct((b, KV_GROUPS, s, LANES), BF16),
        jax.ShapeDtypeStruct((b, KV_GROUPS, s, LANES), BF16),
        jax.ShapeDtypeStruct((b, s, LANES), F32),
        jax.ShapeDtypeStruct((b, s, POOL_WIDTH), F32),
        jax.ShapeDtypeStruct((b, s, SGU_WIDTH), F32),
        jax.ShapeDtypeStruct((b, s, SGU_WIDTH), F32),
    )
    out_specs = (
        pl.BlockSpec((1, NSA_HEADS, t, LANES), lambda bi, i: (bi, 0, i, 0)),
        row3(2 * LANES), row3(LANES), row3(LANES), row3(LANES),
        pl.BlockSpec((1, KV_GROUPS, t, LANES), lambda bi, i: (bi, 0, i, 0)),
        pl.BlockSpec((1, KV_GROUPS, t, LANES), lambda bi, i: (bi, 0, i, 0)),
        row3(LANES), row3(POOL_WIDTH), row3(SGU_WIDTH), row3(SGU_WIDTH),
    )
    return pl.pallas_call(
        _in_proj_kernel, grid=grid,
        in_specs=[row3(d),
                  pl.BlockSpec((1, d), lambda bi, i: (0, 0)),
                  pl.BlockSpec((d, IN_COLS), lambda bi, i: (0, 0))],
        out_specs=out_specs, out_shape=out_shape,
        compiler_params=_params("parallel", "parallel"), name="in_proj",
    )(x, g, w)


def _compress_kernel(rk_ref, rv_ref, w1a_ref, w1b_ref, pos_ref, b1_ref, w2k_ref, b2k_ref,
                     w2v_ref, b2v_ref, kc_ref, vct_ref):
    nb = rk_ref.shape[1]
    keep = lax.broadcasted_iota(jnp.int32, (nb, LANES), 0) < nb - 1

    def hidden(r, which):
        a = _dot(r, w1a_ref[which])
        bm = _dot(r, w1b_ref[which])
        pre = a + pltpu.roll(bm, nb - 1, 0)
        posb = (_dot(pos_ref[which, 0], w1a_ref[which])[0:1]
                + _dot(pos_ref[which, 1], w1b_ref[which])[0:1])
        return _gelu(pre + posb + b1_ref[which]).astype(BF16)

    kc = _dot(hidden(rk_ref[0], 0), w2k_ref[...]) + b2k_ref[...]
    kc_ref[0] = jnp.where(keep, kc, 0.0).astype(BF16)
    vc = _dot(hidden(rv_ref[0], 1), w2v_ref[...]) + b2v_ref[...]
    vct_ref[0] = jnp.where(keep, vc, 0.0).T.astype(BF16)


def _compress(kcmp, vcmp, cw):
    b, s, _ = kcmp.shape
    nb = s // CMP_STRIDE
    rk = kcmp.reshape(b, nb, CMP_STRIDE * KV_WIDTH)
    rv = vcmp.reshape(b, nb, CMP_STRIDE * KV_WIDTH)
    full = lambda a: pl.BlockSpec(a.shape, lambda bi: (0,) * a.ndim)
    ws = (cw["w1a"], cw["w1b"], cw["pos"], cw["b1"], cw["w2k"], cw["b2k"], cw["w2v"], cw["b2v"])
    return pl.pallas_call(
        _compress_kernel, grid=(b,),
        in_specs=[pl.BlockSpec((1, nb, rk.shape[2]), lambda bi: (bi, 0, 0)),
                  pl.BlockSpec((1, nb, rv.shape[2]), lambda bi: (bi, 0, 0))]
                 + [full(a) for a in ws],
        out_specs=(pl.BlockSpec((1, nb, KV_WIDTH), lambda bi: (bi, 0, 0)),
                   pl.BlockSpec((1, KV_WIDTH, nb), lambda bi: (bi, 0, 0))),
        out_shape=(jax.ShapeDtypeStruct((b, nb, KV_WIDTH), BF16),
                   jax.ShapeDtypeStruct((b, KV_WIDTH, nb), BF16)),
        compiler_params=_params("parallel"), name="compress",
    )(rk, rv, *ws)


def _compress_weights(pos, w1, b1, w2, b2):
    eye = jnp.eye(KV_GROUPS, dtype=F32)
    half = CMP_LEN // 2

    def w1_blocks(w):
        return jnp.einsum("ldh,pg->lpdgh", w, eye).reshape(half * KV_WIDTH, KV_GROUPS * CMP_HIDDEN)

    def pos_rows(p):
        r = jnp.broadcast_to(p[:, None, :], (half, KV_GROUPS, HEAD_DIM)).reshape(1, half * KV_WIDTH)
        return jnp.concatenate([r, jnp.zeros((SUBLANES - 1, half * KV_WIDTH), F32)], axis=0)

    w1a = jnp.stack([w1_blocks(w1[i, :half]) for i in range(2)]).astype(BF16)
    w1b = jnp.stack([w1_blocks(w1[i, half:]) for i in range(2)]).astype(BF16)
    posr = jnp.stack([jnp.stack([pos_rows(pos[i, :half]), pos_rows(pos[i, half:])])
                      for i in range(2)]).astype(BF16)
    b1t = jnp.tile(b1, (1, KV_GROUPS))[:, None, :]
    w2blk = lambda w: jnp.einsum("hd,pg->phgd", w, eye).reshape(KV_GROUPS * CMP_HIDDEN, KV_WIDTH)
    return dict(w1a=w1a, w1b=w1b, pos=posr, b1=b1t,
                w2k=w2blk(w2[0]).astype(BF16), b2k=jnp.tile(b2[0], KV_GROUPS)[None, :],
                w2v=w2blk(w2[1]).astype(BF16), b2v=jnp.tile(b2[1], KV_GROUPS)[None, :])


CMP_PAD = 64
CMP_BAND_BACK = 56
CMP_BAND = CMP_BAND_BACK + T_CMP // CMP_STRIDE
CMP_BAND_DIST0 = CMP_BAND_BACK * CMP_STRIDE - (CMP_LEN - 1)
assert CMP_BAND_DIST0 + CMP_STRIDE >= FAR_DIST and CMP_BAND_BACK <= CMP_PAD


def _cmp_kernel(qa_ref, kc_ref, vct_ref, band_ref, far_ref, wagg_ref, ocmp_ref, qaug_ref,
                st_ref):
    tq = qa_ref.shape[2]
    nb = kc_ref.shape[1]
    ns = wagg_ref.shape[0]
    i = pl.program_id(1)
    g = pl.program_id(0)
    s0 = i * tq
    t = s0 + lax.broadcasted_iota(jnp.int32, (1, tq), 1)
    crow = lax.broadcasted_iota(jnp.int32, (nb, 1), 0)
    valid = crow * CMP_STRIDE + (CMP_LEN - 1) <= t
    band_row0 = pl.multiple_of(s0 // CMP_STRIDE - CMP_BAND_BACK + CMP_PAD, SUBLANES)
    c_lo = s0 // CMP_STRIDE - CMP_BAND_BACK
    kc = kc_ref[0]

    imp = jnp.zeros((nb, tq), F32)
    outs = []
    for h in range(HEADS_PER_GROUP):
        st_ref[CMP_PAD:, :] = _dot_nt(kc, qa_ref[0, h])
        st_ref[pl.ds(band_row0, CMP_BAND), :] += band_ref[h]
        far = far_ref[g * HEADS_PER_GROUP + h]
        s = st_ref[CMP_PAD:, :] + jnp.where(crow < c_lo, far, 0.0)
        s = jnp.where(valid, s, NEG_INIT)
        m = jnp.max(s, axis=0, keepdims=True)
        e = jnp.where(valid, jnp.exp(s - m), 0.0)
        p = e / jnp.maximum(jnp.sum(e, axis=0, keepdims=True), 1.0)
        imp = imp + p
        outs.append(_dot(vct_ref[0], p.astype(BF16)))
    ocmp_ref[0] = jnp.concatenate(outs, axis=0).T

    p_sel = _dot_f32_left(wagg_ref[...], imp)
    j = lax.broadcasted_iota(jnp.int32, (ns, 1), 0)
    rel = jnp.right_shift(t, _log2(SEL_BLOCK)) - j
    forced = (j == 0) | ((rel >= 0) & (rel < N_LOCAL_SEL))
    score = jnp.where(forced, FORCE_SCORE, jnp.where(j * SEL_BLOCK <= t, p_sel, -1.0))
    jfull = jnp.broadcast_to(j, (ns, tq)).astype(F32)
    sel = jnp.zeros((ns, tq), F32)
    for _ in range(SEL_TOPK):
        m = jnp.max(score, axis=0, keepdims=True)
        first = jnp.min(jnp.where(score == m, jfull, float(ns)), axis=0, keepdims=True)
        hit = jfull == first
        sel = jnp.where(hit, 1.0, sel)
        score = jnp.where(hit, -3e38, score)
    notsel = jnp.where(sel.T > 0.5, 0.0, MASK_BIAS).astype(BF16)
    for h in range(HEADS_PER_GROUP):
        qaug_ref[0, h, :, :LANES] = qa_ref[0, h]
        qaug_ref[0, h, :, LANES:] = notsel


def _cmp_attention(qa, kc, vct, band, far, wagg):
    b, _, s, _ = qa.shape
    nb = kc.shape[1]
    tq = T_CMP
    grid = (KV_GROUPS, s // tq, b)
    return pl.pallas_call(
        _cmp_kernel, grid=grid,
        in_specs=[
            pl.BlockSpec((1, HEADS_PER_GROUP, tq, LANES), lambda g, i, bi: (bi, g, i, 0)),
            pl.BlockSpec((1, nb, KV_WIDTH), lambda g, i, bi: (bi, 0, 0)),
            pl.BlockSpec((1, HEAD_DIM, nb), lambda g, i, bi: (bi, g, 0)),
            pl.BlockSpec((HEADS_PER_GROUP, CMP_BAND, tq), lambda g, i, bi: (g, 0, 0)),
            pl.BlockSpec(memory_space=pltpu.SMEM),
            pl.BlockSpec(wagg.shape, lambda g, i, bi: (0, 0)),
        ],
        out_specs=(
            pl.BlockSpec((1, tq, HEADS_PER_GROUP * HEAD_DIM), lambda g, i, bi: (bi, i, g)),
            pl.BlockSpec((1, HEADS_PER_GROUP, tq, 2 * LANES), lambda g, i, bi: (bi, g, i, 0)),
        ),
        out_shape=(jax.ShapeDtypeStruct((b, s, NSA_WIDTH), F32),
                   jax.ShapeDtypeStruct((b, NSA_HEADS, s, 2 * LANES), BF16)),
        scratch_shapes=[pltpu.VMEM((CMP_PAD + nb, tq), F32)],
        compiler_params=_params("parallel", "parallel", "parallel"), name="cmp_attention",
    )(qa, kc, vct, band, far, wagg)


def _cmp_tables(rel_bias, nb):
    jj = np.arange(CMP_BAND)[:, None]
    ii = np.arange(T_CMP)[None, :]
    dist = CMP_BAND_DIST0 + ii - CMP_STRIDE * jj
    bidx = _BUCKETS[np.clip(dist, 0, None)]
    band = jnp.transpose(rel_bias[bidx], (2, 0, 1))
    band = jnp.where(jnp.asarray(dist >= 0)[None], band, 0.0)
    far = rel_bias[REL_BUCKETS - 1]
    ns = nb // (SEL_BLOCK // CMP_STRIDE)
    assert SEL_TOPK <= ns <= LANES
    wagg = np.zeros((LANES, nb), np.float32)
    for m, w in enumerate(SEL_AGG):
        for jb in range(ns):
            c = (SEL_BLOCK // CMP_STRIDE) * jb + m - 1
            if 0 <= c < nb - 1:
                wagg[jb, c] = w
    return band, far, jnp.asarray(wagg, BF16)


N_NEAR = (FAR_DIST + T_ATT - 1) // T_ATT + 1
assert N_NEAR * T_ATT - (T_ATT - 1) >= FAR_DIST
WIN_TILES = (WINDOW - 1 + T_ATT - 1) // T_ATT


def _flash_kernel(q_ref, k_ref, v_ref, nb_ref, o_ref, m_ref, acc_ref, *, windowed):
    t = T_ATT
    rows = HEADS_PER_GROUP * t
    i = pl.program_id(2)
    q = q_ref[0].reshape(rows, q_ref.shape[3])
    m_ref[...] = jnp.full(m_ref.shape, NEG_INIT, F32)
    acc_ref[...] = jnp.zeros(acc_ref.shape, F32)
    r = lax.broadcasted_iota(jnp.int32, (t, t), 0)
    c = lax.broadcasted_iota(jnp.int32, (t, t), 1)

    def step(ki, d, keep):
        start = pl.multiple_of(ki * t, t)
        s = _dot_nt(q, k_ref[0, pl.ds(start, t), :])
        s = s + nb_ref[0, :, d].reshape(rows, t)
        if keep is not None:
            s = jnp.where(jnp.tile(keep, (HEADS_PER_GROUP, 1)), s, MASK_BIAS)
        m_prev = m_ref[...]
        m_new = jnp.maximum(m_prev, jnp.max(s, axis=1, keepdims=True))
        p = jnp.exp(s - jnp.tile(m_new, (1, t // LANES)))
        pv = _dot(p.astype(BF16), v_ref[0, 0, pl.ds(start, t), :])
        acc_ref[...] = jnp.exp(m_prev - m_new) * acc_ref[...] + pv
        m_ref[...] = m_new

    if windowed:
        for d in range(WIN_TILES, 0, -1):
            dist = d * t + r - c
            keep = None if (d * t + t - 1) < WINDOW else dist < WINDOW

            @pl.when(i >= d)
            def _():
                step(i - d, d, keep)
    else:
        def body(ki, carry):
            step(ki, jnp.minimum(i - ki, N_NEAR), None)
            return carry
        lax.fori_loop(0, i, body, 0)
    step(i, 0, r >= c)

    acc = acc_ref[...]
    o = acc / jnp.maximum(acc[:, HEAD_DIM:HEAD_DIM + 1], 1.0)
    o_ref[0] = jnp.concatenate(
        [o[h * t:(h + 1) * t, :HEAD_DIM] for h in range(HEADS_PER_GROUP)], axis=1)


def _flash(q_aug, k, v, nbias, *, windowed):
    b, _, s, _ = q_aug.shape
    kd = k.shape[2]
    t = T_ATT
    grid = (b, KV_GROUPS, s // t)
    return pl.pallas_call(
        functools.partial(_flash_kernel, windowed=windowed), grid=grid,
        in_specs=[
            pl.BlockSpec((1, HEADS_PER_GROUP, t, kd), lambda bi, g, i: (bi, g, i, 0)),
            pl.BlockSpec((1, s, kd), lambda bi, g, i: (bi, 0, 0)),
            pl.BlockSpec((1, 1, s, LANES), lambda bi, g, i: (bi, g, 0, 0)),
            pl.BlockSpec((1, HEADS_PER_GROUP, N_NEAR + 1, t, t), lambda bi, g, i: (g, 0, 0, 0, 0)),
        ],
        out_specs=pl.BlockSpec((1, t, HEADS_PER_GROUP * HEAD_DIM), lambda bi, g, i: (bi, i, g)),
        out_shape=jax.ShapeDtypeStruct((b, s, NSA_WIDTH), F32),
        scratch_shapes=[pltpu.VMEM((HEADS_PER_GROUP * t, LANES), F32),
                        pltpu.VMEM((HEADS_PER_GROUP * t, LANES), F32)],
        compiler_params=_params("parallel", "parallel", "parallel"),
        name="win_attention" if windowed else "sel_attention",
    )(q_aug, k, v, nbias)


def _near_bias_tables(rel_bias):
    t = T_ATT
    d = np.arange(N_NEAR + 1)[:, None, None]
    r = np.arange(t)[None, :, None]
    c = np.arange(t)[None, None, :]
    dist = d * t + r - c
    bidx = _BUCKETS[np.clip(dist, 0, None)]
    tab = jnp.transpose(rel_bias[bidx], (3, 0, 1, 2))
    return tab.reshape(KV_GROUPS, HEADS_PER_GROUP, N_NEAR + 1, t, t)


def _pool_kernel(x_ref, halo_ref, w_ref, scale_ref, o_ref):
    t = x_ref.shape[1]
    i = pl.program_id(1)
    x = x_ref[0]
    halo = jnp.where(i > 0, halo_ref[0], 0.0)
    xs = jnp.concatenate([halo, x], axis=0)
    sums = []
    acc = xs
    span = 1
    for win in POOL_WINDOWS:
        while span < win:
            acc = acc + pltpu.roll(acc, span, 0)
            span *= 2
        sums.append(acc[HALO:])
    lane_grp = jnp.right_shift(lax.broadcasted_iota(jnp.int32, (t, POOL_WIDTH), 1), _log2(POOL_CH))
    tok = i * t + lax.broadcasted_iota(jnp.int32, (t, POOL_WIDTH), 0)
    ssum = sums[-1]
    win = jnp.full((t, POOL_WIDTH), POOL_WINDOWS[-1], jnp.int32)
    for gi in range(len(POOL_WINDOWS) - 2, -1, -1):
        ssum = jnp.where(lane_grp == gi, sums[gi], ssum)
        win = jnp.where(lane_grp == gi, POOL_WINDOWS[gi], win)
    cnt = jnp.minimum(tok + 1, win).astype(F32)
    dlt = (ssum / cnt - x).astype(BF16)
    o_ref[0] = (_dot(dlt, w_ref[...]) * scale_ref[...]).astype(BF16)


def _pool_mixer(p, wblk, scale):
    b, s, c = p.shape
    t = T_MIX
    return pl.pallas_call(
        _pool_kernel, grid=(b, s // t),
        in_specs=[pl.BlockSpec((1, t, c), lambda bi, i: (bi, i, 0)),
                  pl.BlockSpec((1, HALO, c), lambda bi, i: (bi, jnp.maximum(i * (t // HALO) - 1, 0), 0)),
                  pl.BlockSpec((c, c), lambda bi, i: (0, 0)),
                  pl.BlockSpec((1, c), lambda bi, i: (0, 0))],
        out_specs=pl.BlockSpec((1, t, c), lambda bi, i: (bi, i, 0)),
        out_shape=jax.ShapeDtypeStruct((b, s, c), BF16),
        compiler_params=_params("parallel", "parallel"), name="pool_mixer",
    )(p, p, wblk, scale)


def _sgu_kernel(u_ref, v_ref, ng_ref, nb_ref, ws_ref, bs_ref, o_ref):
    t = u_ref.shape[1]
    v = _gelu(v_ref[0])
    mu = jnp.mean(v, axis=-1, keepdims=True)
    var = jnp.mean(jnp.square(v - mu), axis=-1, keepdims=True)
    vn = (v - mu) * lax.rsqrt(var + EPS) * ng_ref[...] + nb_ref[...]
    lane_grp = jnp.right_shift(lax.broadcasted_iota(jnp.int32, (SGU_CHUNK, SGU_WIDTH), 1),
                               _log2(SGU_CH))
    r = lax.broadcasted_iota(jnp.int32, (SGU_CHUNK, SGU_GROUPS * SGU_CHUNK), 0)
    c = lax.broadcasted_iota(jnp.int32, (SGU_CHUNK, SGU_GROUPS * SGU_CHUNK), 1) & (SGU_CHUNK - 1)
    ws = jnp.where(c <= r, ws_ref[...], 0.0).astype(BF16)
    for n in range(t // SGU_CHUNK):
        vc = vn[n * SGU_CHUNK:(n + 1) * SGU_CHUNK]
        stacked = jnp.concatenate(
            [jnp.where(lane_grp == gi, vc, 0.0) for gi in range(SGU_GROUPS)], axis=0).astype(BF16)
        mixed = _dot(ws, stacked) + bs_ref[...]
        u = _gelu(u_ref[0, n * SGU_CHUNK:(n + 1) * SGU_CHUNK, :])
        o_ref[0, n * SGU_CHUNK:(n + 1) * SGU_CHUNK, :] = (u * mixed).astype(BF16)


def _sgu_mixer(u, v, ng, nbias, ws_cat, bs_exp):
    b, s, c = u.shape
    t = T_MIX
    row = pl.BlockSpec((1, t, c), lambda bi, i: (bi, i, 0))
    full = lambda a: pl.BlockSpec(a.shape, lambda bi, i: (0,) * a.ndim)
    return pl.pallas_call(
        _sgu_kernel, grid=(b, s // t),
        in_specs=[row, row, full(ng), full(nbias), full(ws_cat), full(bs_exp)],
        out_specs=row, out_shape=jax.ShapeDtypeStruct((b, s, c), BF16),
        compiler_params=_params("parallel", "parallel"), name="sgu_mixer",
    )(u, v, ng, nbias, ws_cat, bs_exp)


def _out_proj_kernel(oc_ref, os_ref, ow_ref, gate_ref, e_ref, pool_ref, sgu_ref, w_ref, g_ref,
                     x_ref, y_ref):
    gexp = _dot_f32(gate_ref[0], e_ref[...])
    a = (gexp[:, :NSA_WIDTH] * oc_ref[0] + gexp[:, NSA_WIDTH:2 * NSA_WIDTH] * os_ref[0]
         + gexp[:, 2 * NSA_WIDTH:] * ow_ref[0]).astype(BF16)
    mix = (_dot(a, w_ref[:NSA_WIDTH, :])
           + _dot(pool_ref[0], w_ref[NSA_WIDTH:NSA_WIDTH + POOL_WIDTH, :])
           + _dot(sgu_ref[0], w_ref[NSA_WIDTH + POOL_WIDTH:, :]))
    y_ref[0] = x_ref[0] + _rms(mix, g_ref[...])


def _out_proj(o_cmp, o_sel, o_win, gates, e, pool, sgu, w, g, x):
    b, s, d = x.shape
    t = T_OUT
    row = lambda n: pl.BlockSpec((1, t, n), lambda bi, i: (bi, i, 0))
    full = lambda a: pl.BlockSpec(a.shape, lambda bi, i: (0,) * a.ndim)
    return pl.pallas_call(
        _out_proj_kernel, grid=(b, s // t),
        in_specs=[row(NSA_WIDTH), row(NSA_WIDTH), row(NSA_WIDTH), row(LANES), full(e),
                  row(POOL_WIDTH), row(SGU_WIDTH), full(w), full(g), row(d)],
        out_specs=row(d), out_shape=jax.ShapeDtypeStruct((b, s, d), F32),
        compiler_params=_params("parallel", "parallel"), name="out_proj",
    )(o_cmp, o_sel, o_win, gates, e, pool, sgu, w, g, x)


def _gate_expansion():
    e = np.zeros((LANES, 3 * NSA_WIDTH), np.float32)
    for h in range(NSA_HEADS):
        for br in range(3):
            e[h * 3 + br, br * NSA_WIDTH + h * HEAD_DIM: br * NSA_WIDTH + (h + 1) * HEAD_DIM] = 1.0
    return jnp.asarray(e, BF16)


def _mem_kv_kernel(mem_ref, g_ref, wk_ref, wv_ref, k_ref, v_ref):
    m = _rms(mem_ref[0], g_ref[...]).astype(BF16)
    k_ref[0] = _dot(m, wk_ref[...]).astype(BF16)
    v_ref[0] = _dot(m, wv_ref[...]).astype(BF16)


def _mem_kv(mem, g, wk, wv):
    b, m, d = mem.shape
    full = lambda a: pl.BlockSpec(a.shape, lambda bi: (0,) * a.ndim)
    blk = pl.BlockSpec((1, m, d), lambda bi: (bi, 0, 0))
    return pl.pallas_call(
        _mem_kv_kernel, grid=(b,),
        in_specs=[blk, full(g), full(wk), full(wv)],
        out_specs=(blk, blk),
        out_shape=(jax.ShapeDtypeStruct((b, m, d), BF16), jax.ShapeDtypeStruct((b, m, d), BF16)),
        compiler_params=_params("parallel"), name="mem_kv",
    )(mem, g, wk, wv)


def _mem_attn_kernel(x_ref, gpre_ref, wq_ref, k_ref, v_ref, wo_ref, gpost_ref, y_ref):
    x = x_ref[0]
    h = _rms(x, gpre_ref[...]).astype(BF16)
    q = (_dot(h, wq_ref[...]) * (MEM_HEAD_DIM ** -0.5)).astype(BF16)
    outs = []
    for hd in range(MEM_HEADS):
        sl = slice(hd * MEM_HEAD_DIM, (hd + 1) * MEM_HEAD_DIM)
        s = _dot_nt(q[:, sl], k_ref[0, :, sl])
        e = jnp.exp(s - jnp.max(s, axis=-1, keepdims=True))
        p = e / jnp.sum(e, axis=-1, keepdims=True)
        outs.append(_dot(p.astype(BF16), v_ref[0, :, sl]).astype(BF16))
    o = jnp.concatenate(outs, axis=1)
    y_ref[0] = x + _rms(_dot(o, wo_ref[...]), gpost_ref[...])


def _mem_attn(x, gpre, wq, k, v, wo, gpost):
    b, s, d = x.shape
    m = k.shape[1]
    t = T_MEM
    row = pl.BlockSpec((1, t, d), lambda bi, i: (bi, i, 0))
    full = lambda a: pl.BlockSpec(a.shape, lambda bi, i: (0,) * a.ndim)
    kv = pl.BlockSpec((1, m, d), lambda bi, i: (bi, 0, 0))
    return pl.pallas_call(
        _mem_attn_kernel, grid=(b, s // t),
        in_specs=[row, full(gpre), full(wq), kv, kv, full(wo), full(gpost)],
        out_specs=row, out_shape=jax.ShapeDtypeStruct((b, s, d), F32),
        compiler_params=_params("parallel", "parallel"), name="mem_attention",
    )(x, gpre, wq, k, v, wo, gpost)


def _ffn_kernel(x_ref, halo_ref, gpre_ref, wg_ref, wv_ref, cwg_ref, cwv_ref, cbg_ref, cbv_ref,
                wd_ref, gpost_ref, y_ref, h_ref, acc_ref):
    i = pl.program_id(1)
    j = pl.program_id(2)

    @pl.when(j == 0)
    def _():
        halo = jnp.where(i > 0, halo_ref[0], 0.0)
        xs = jnp.concatenate([halo, x_ref[0]], axis=0)
        h_ref[...] = _rms(xs, gpre_ref[...]).astype(BF16)
        acc_ref[...] = jnp.zeros(acc_ref.shape, F32)

    h = h_ref[...]

    def conv(a, cw_ref, cb_ref):
        return (cw_ref[0:1, :] * pltpu.roll(a, 2, 0) + cw_ref[1:2, :] * pltpu.roll(a, 1, 0)
                + cw_ref[2:3, :] * a + cb_ref[...])

    gate = conv(_dot(h, wg_ref[...]), cwg_ref, cbg_ref)
    val = conv(_dot(h, wv_ref[...]), cwv_ref, cbv_ref)
    act = (_gelu(gate) * val)[HALO:].astype(BF16)
    acc_ref[...] += _dot(act, wd_ref[...])

    @pl.when(j == pl.num_programs(2) - 1)
    def _():
        y_ref[0] = x_ref[0] + _rms(acc_ref[...], gpost_ref[...])


def _ffn(x, gpre, w_up, conv_w, conv_b, w_down, gpost):
    b, s, d = x.shape
    f = w_down.shape[0]
    t = T_FFN
    fc = F_CHUNK
    nf = f // fc
    return pl.pallas_call(
        _ffn_kernel, grid=(b, s // t, nf),
        in_specs=[
            pl.BlockSpec((1, t, d), lambda bi, i, j: (bi, i, 0)),
            pl.BlockSpec((1, HALO, d), lambda bi, i, j: (bi, jnp.maximum(i * (t // HALO) - 1, 0), 0)),
            pl.BlockSpec((1, d), lambda bi, i, j: (0, 0)),
            pl.BlockSpec((d, fc), lambda bi, i, j: (0, j)),
            pl.BlockSpec((d, fc), lambda bi, i, j: (0, j + nf)),
            pl.BlockSpec((3, fc), lambda bi, i, j: (0, j)),
            pl.BlockSpec((3, fc), lambda bi, i, j: (0, j + nf)),
            pl.BlockSpec((1, fc), lambda bi, i, j: (0, j)),
            pl.BlockSpec((1, fc), lambda bi, i, j: (0, j + nf)),
            pl.BlockSpec((fc, d), lambda bi, i, j: (j, 0)),
            pl.BlockSpec((1, d), lambda bi, i, j: (0, 0)),
        ],
        out_specs=pl.BlockSpec((1, t, d), lambda bi, i, j: (bi, i, 0)),
        out_shape=jax.ShapeDtypeStruct((b, s, d), F32),
        scratch_shapes=[pltpu.VMEM((t + HALO, d), BF16), pltpu.VMEM((t, d), F32)],
        compiler_params=_params("parallel", "parallel", "arbitrary"), name="conv_ffn",
    )(x, x, gpre, w_up, w_up, conv_w, conv_w, conv_b, conv_b, w_down, gpost)


def kernel(x, mem, rel_bias, mix_norm_pre, mix_norm_post, w_in, cmp_pos, cmp_w1, cmp_b1, cmp_w2,
           cmp_b2, pool_w, pool_scale, sgu_norm_g, sgu_norm_b, sgu_w, sgu_b, w_out, mem_norm_pre,
           mem_norm_kv, mem_norm_post, w_mq, w_mk, w_mv, w_mo, ffn_norm_pre, ffn_norm_post, w_up,
           conv_w, conv_b, w_down):
    depth = w_in.shape[0]
    s = x.shape[1]
    assert s % max(T_IN, T_CMP, T_ATT, T_OUT, T_MIX, T_MEM, T_FFN) == 0
    nb = s // CMP_STRIDE

    w_in_p = (jnp.take(w_in, jnp.asarray(_IN_IDX), axis=2) * jnp.asarray(_IN_MASK)).astype(BF16)
    band, far, wagg = _cmp_tables(rel_bias, nb)
    near = _near_bias_tables(rel_bias)
    e_gate = _gate_expansion()
    row = lambda a: a[:, None, :]
    bf = lambda a: a.astype(BF16)
    w_out_b, w_mq_b, w_mk_b, w_mv_b, w_mo_b = bf(w_out), bf(w_mq), bf(w_mk), bf(w_mv), bf(w_mo)
    w_up_b, w_down_b = bf(w_up), bf(w_down)

    for l in range(depth):
        qa, ksel, kwin, kcmp, vcmp, vsel, vwin, gates, pin, u, v = _in_proj(
            x, row(mix_norm_pre)[l], w_in_p[l])
        cw = _compress_weights(cmp_pos[l], cmp_w1[l], cmp_b1[l], cmp_w2[l], cmp_b2[l])
        kc, vct = _compress(kcmp, vcmp, cw)
        o_cmp, q_aug = _cmp_attention(qa, kc, vct, band, far, wagg)
        o_sel = _flash(q_aug, ksel, vsel, near, windowed=False)
        o_win = _flash(qa, kwin, vwin, near, windowed=True)
        wblk = bf(jax.scipy.linalg.block_diag(*[pool_w[l, gi] for gi in range(len(POOL_WINDOWS))]))
        out_b = _pool_mixer(pin, wblk, row(pool_scale)[l])
        ws_cat = jnp.concatenate([sgu_w[l, gi] for gi in range(SGU_GROUPS)], axis=1)
        bs_exp = jnp.repeat(jnp.transpose(sgu_b[l]), SGU_CH, axis=1)
        out_c = _sgu_mixer(u, v, row(sgu_norm_g)[l], row(sgu_norm_b)[l], ws_cat, bs_exp)
        x = _out_proj(o_cmp, o_sel, o_win, gates, e_gate, out_b, out_c, w_out_b[l],
                      row(mix_norm_post)[l], x)
        km, vm = _mem_kv(mem, row(mem_norm_kv)[l], w_mk_b[l], w_mv_b[l])
        x = _mem_attn(x, row(mem_norm_pre)[l], w_mq_b[l], km, vm, w_mo_b[l], row(mem_norm_post)[l])
        x = _ffn(x, row(ffn_norm_pre)[l], w_up_b[l], conv_w[l], row(conv_b)[l], w_down_b[l],
                 row(ffn_norm_post)[l])
    return x
```

```python
import functools
import math

import numpy as np
import jax
import jax.numpy as jnp
from jax import lax
from jax.experimental import pallas as pl
from jax.experimental.pallas import tpu as pltpu

F32 = jnp.float32
BF16 = jnp.bfloat16

D_MODEL = 1024
NSA_WIDTH = 512
POOL_WIDTH = 256
SGU_WIDTH = 256
NSA_HEADS = 8
KV_GROUPS = 2
HEADS_PER_GROUP = 4
HEAD_DIM = 64
KV_WIDTH = 128
GATE_WIDTH = 24
CMP_STRIDE = 16
CMP_LEN = 32
CMP_HIDDEN = 128
SEL_BLOCK = 64
SEL_TOPK = 16
N_LOCAL_SEL = 2
SEL_AGG = (1.0, 2.0, 2.0, 2.0, 1.0)
FORCE_SCORE = 1e9
WINDOW = 512
REL_BUCKETS = 32
REL_MAX_DIST = 1024
POOL_WINDOWS = (2, 4, 8, 16)
POOL_CH = 64
SGU_GROUPS = 4
SGU_CH = 64
SGU_CHUNK = 128
MEM_HEADS = 4
MEM_HEAD_DIM = 256
FFN_HIDDEN = 2816
EPS = 1e-6

LANES = 128
SUBLANES = 8
BF16_ROWS = 16
VMEM_LIMIT = 56 * 1024 * 1024

T_IN = 512
T_CMP = 256
T_ATT = 256
T_OUT = 512
T_MIX = 512
T_MEM = 512
T_FFN = 512
F_CHUNK = 256
HALO = 16

MASK_BIAS = -float(2 ** 30)
NEG_INIT = -1e30

_Q_OFF = 0
_KSEL_OFF = 1024
_KWIN_OFF = 1152
_KCMP_OFF = 1280
_VCMP_OFF = 1408
_VSEL_OFF = 1536
_VWIN_OFF = 1792
_GATE_OFF = 2048
_POOL_OFF = 2304
_U_OFF = 2560
_V_OFF = 2816
IN_COLS = 3072
MXU_COLS = 256


def _bucket_table(n_max):
    n = np.arange(n_max)
    max_exact = REL_BUCKETS // 2
    nf = np.maximum(n, 1).astype(np.float64)
    large = max_exact + (np.log(nf / max_exact) / math.log(REL_MAX_DIST / max_exact)
                         * (REL_BUCKETS - max_exact)).astype(np.int32)
    large = np.minimum(large, REL_BUCKETS - 1)
    return np.where(n < max_exact, n, large).astype(np.int32)


_BUCKETS = _bucket_table(16384)
FAR_DIST = int(np.argmax(_BUCKETS == REL_BUCKETS - 1))


def _in_proj_columns():
    idx = np.zeros((IN_COLS,), np.int32)
    msk = np.zeros((IN_COLS,), np.float32)

    def put(dst, src, n):
        idx[dst:dst + n] = np.arange(src, src + n)
        msk[dst:dst + n] = 1.0

    for h in range(NSA_HEADS):
        g = h // HEADS_PER_GROUP
        put(_Q_OFF + h * LANES + g * HEAD_DIM, h * HEAD_DIM, HEAD_DIM)
    base = NSA_WIDTH
    put(_KCMP_OFF, base, KV_WIDTH)
    put(_VCMP_OFF, base + KV_WIDTH, KV_WIDTH)
    put(_KSEL_OFF, base + 2 * KV_WIDTH, KV_WIDTH)
    for g in range(KV_GROUPS):
        put(_VSEL_OFF + g * LANES, base + 3 * KV_WIDTH + g * HEAD_DIM, HEAD_DIM)
    put(_KWIN_OFF, base + 4 * KV_WIDTH, KV_WIDTH)
    for g in range(KV_GROUPS):
        put(_VWIN_OFF + g * LANES, base + 5 * KV_WIDTH + g * HEAD_DIM, HEAD_DIM)
    base += 6 * KV_WIDTH
    put(_GATE_OFF, base, GATE_WIDTH)
    base += GATE_WIDTH
    put(_POOL_OFF, base, POOL_WIDTH)
    put(_U_OFF, base + POOL_WIDTH, SGU_WIDTH)
    put(_V_OFF, base + POOL_WIDTH + SGU_WIDTH, SGU_WIDTH)
    return idx, msk


_IN_IDX, _IN_MASK = _in_proj_columns()


def _log2(n):
    assert n & (n - 1) == 0
    return n.bit_length() - 1


def _params(*sem):
    return pltpu.CompilerParams(dimension_semantics=sem, vmem_limit_bytes=VMEM_LIMIT)


def _rms(x, g):
    return x * lax.rsqrt(jnp.mean(x * x, axis=-1, keepdims=True) + EPS) * g


def _gelu(x):
    c = math.sqrt(2.0 / math.pi)
    return x * (0.5 * (1.0 + jnp.tanh(c * (x + 0.044715 * (x * x * x)))))


def _dot(a, b):
    return jnp.dot(a, b, preferred_element_type=F32)


def _dot_nt(a, b):
    return lax.dot_general(a, b, (((1,), (1,)), ((), ())), preferred_element_type=F32)


def _split3(x):
    hi = x.astype(BF16)
    r1 = x - hi.astype(F32)
    mid = r1.astype(BF16)
    lo = (r1 - mid.astype(F32)).astype(BF16)
    return hi, mid, lo


def _dot_f32_left(w_bf16, x):
    hi, mid, lo = _split3(x)
    return _dot(w_bf16, hi) + _dot(w_bf16, mid) + _dot(w_bf16, lo)


def _in_proj_kernel(x_ref, g_ref, w_ref, qa_ref, ksel_ref, kwin_ref, kcmp_ref, vcmp_ref,
                    vsel_ref, vwin_ref, gate_ref, pool_ref, u_ref, v_ref):
    t = x_ref.shape[1]
    h = _rms(x_ref[0], g_ref[...]).astype(BF16)

    def proj(off):
        return _dot(h, w_ref[:, off:off + MXU_COLS])

    def slabs(off):
        p = proj(off)
        return p[:, :LANES], p[:, LANES:]

    scale = HEAD_DIM ** -0.5
    for hd in range(0, NSA_HEADS, 2):
        qa, qb = slabs(_Q_OFF + hd * LANES)
        qa_ref[0, hd] = (qa * scale).astype(BF16)
        qa_ref[0, hd + 1] = (qb * scale).astype(BF16)

    lane = lax.broadcasted_iota(jnp.int32, (t, LANES), 1)
    ksel, kwin = slabs(_KSEL_OFF)
    ksel_ref[0, :LANES, :] = ksel.T.astype(BF16)
    tok = pl.program_id(1) * t + lax.broadcasted_iota(jnp.int32, (LANES, t), 1)
    blk_id = lax.broadcasted_iota(jnp.int32, (LANES, t), 0)
    ksel_ref[0, LANES:, :] = jnp.where(
        jnp.right_shift(tok, _log2(SEL_BLOCK)) == blk_id, 1.0, 0.0).astype(BF16)
    kwin_ref[0] = kwin.T.astype(BF16)
    kcmp, vcmp = slabs(_KCMP_OFF)
    kcmp_ref[0] = kcmp.astype(BF16)
    vcmp_ref[0] = vcmp.astype(BF16)
    for v_out, off in ((vsel_ref, _VSEL_OFF), (vwin_ref, _VWIN_OFF)):
        for g, vg in enumerate(slabs(off)):
            v_out[0, g] = jnp.where(lane == HEAD_DIM, 1.0, vg).astype(BF16)
    gate_ref[0] = jax.nn.sigmoid(slabs(_GATE_OFF)[0])
    pool_ref[0] = proj(_POOL_OFF)
    u_ref[0] = proj(_U_OFF)
    v_ref[0] = proj(_V_OFF)


def _in_proj(x, g, w):
    b, s, d = x.shape
    t = T_IN
    grid = (b, s // t)
    row3 = lambda n: pl.BlockSpec((1, t, n), lambda bi, i: (bi, i, 0))
    out_shape = (
        jax.ShapeDtypeStruct((b, NSA_HEADS, s, LANES), BF16),
        jax.ShapeDtypeStruct((b, 2 * LANES, s), BF16),
        jax.ShapeDtypeStruct((b, LANES, s), BF16),
        jax.ShapeDtypeStruct((b, s, LANES), BF16),
        jax.ShapeDtypeStruct((b, s, LANES), BF16),
        jax.ShapeDtypeStruct((b, KV_GROUPS, s, LANES), BF16),
        jax.ShapeDtypeStruct((b, KV_GROUPS, s, LANES), BF16),
        jax.ShapeDtypeStruct((b, s, LANES), F32),
        jax.ShapeDtypeStruct((b, s, POOL_WIDTH), F32),
        jax.ShapeDtypeStruct((b, s, SGU_WIDTH), F32),
        jax.ShapeDtypeStruct((b, s, SGU_WIDTH), F32),
    )
    out_specs = (
        pl.BlockSpec((1, NSA_HEADS, t, LANES), lambda bi, i: (bi, 0, i, 0)),
        pl.BlockSpec((1, 2 * LANES, t), lambda bi, i: (bi, 0, i)),
        pl.BlockSpec((1, LANES, t), lambda bi, i: (bi, 0, i)),
        row3(LANES), row3(LANES),
        pl.BlockSpec((1, KV_GROUPS, t, LANES), lambda bi, i: (bi, 0, i, 0)),
        pl.BlockSpec((1, KV_GROUPS, t, LANES), lambda bi, i: (bi, 0, i, 0)),
        row3(LANES), row3(POOL_WIDTH), row3(SGU_WIDTH), row3(SGU_WIDTH),
    )
    return pl.pallas_call(
        _in_proj_kernel, grid=grid,
        in_specs=[row3(d),
                  pl.BlockSpec((1, d), lambda bi, i: (0, 0)),
                  pl.BlockSpec((d, IN_COLS), lambda bi, i: (0, 0))],
        out_specs=out_specs, out_shape=out_shape,
        compiler_params=_params("parallel", "parallel"), name="in_proj",
    )(x, g, w)


def _compress_kernel(rk_ref, rv_ref, w1a_ref, w1b_ref, pos_ref, b1_ref, w2k_ref, b2k_ref,
                     w2v_ref, b2v_ref, kc_ref, vct_ref):
    nb = rk_ref.shape[1]
    keep = lax.broadcasted_iota(jnp.int32, (nb, LANES), 0) < nb - 1

    def hidden(r, which):
        a = _dot(r, w1a_ref[which])
        bm = _dot(r, w1b_ref[which])
        pre = a + pltpu.roll(bm, nb - 1, 0)
        posb = (_dot(pos_ref[which, 0], w1a_ref[which])[0:1]
                + _dot(pos_ref[which, 1], w1b_ref[which])[0:1])
        return _gelu(pre + posb + b1_ref[which]).astype(BF16)

    kc = _dot(hidden(rk_ref[0], 0), w2k_ref[...]) + b2k_ref[...]
    kc_ref[0] = jnp.where(keep, kc, 0.0).astype(BF16)
    vc = _dot(hidden(rv_ref[0], 1), w2v_ref[...]) + b2v_ref[...]
    vct_ref[0] = jnp.where(keep, vc, 0.0).T.astype(BF16)


def _compress(kcmp, vcmp, cw):
    b, s, _ = kcmp.shape
    nb = s // CMP_STRIDE
    rk = kcmp.reshape(b, nb, CMP_STRIDE * KV_WIDTH)
    rv = vcmp.reshape(b, nb, CMP_STRIDE * KV_WIDTH)
    full = lambda a: pl.BlockSpec(a.shape, lambda bi: (0,) * a.ndim)
    ws = (cw["w1a"], cw["w1b"], cw["pos"], cw["b1"], cw["w2k"], cw["b2k"], cw["w2v"], cw["b2v"])
    return pl.pallas_call(
        _compress_kernel, grid=(b,),
        in_specs=[pl.BlockSpec((1, nb, rk.shape[2]), lambda bi: (bi, 0, 0)),
                  pl.BlockSpec((1, nb, rv.shape[2]), lambda bi: (bi, 0, 0))]
                 + [full(a) for a in ws],
        out_specs=(pl.BlockSpec((1, nb, KV_WIDTH), lambda bi: (bi, 0, 0)),
                   pl.BlockSpec((1, KV_WIDTH, nb), lambda bi: (bi, 0, 0))),
        out_shape=(jax.ShapeDtypeStruct((b, nb, KV_WIDTH), BF16),
                   jax.ShapeDtypeStruct((b, KV_WIDTH, nb), BF16)),
        compiler_params=_params("parallel"), name="compress",
    )(rk, rv, *ws)


def _compress_weights(pos, w1, b1, w2, b2):
    eye = jnp.eye(KV_GROUPS, dtype=F32)
    half = CMP_LEN // 2

    def w1_blocks(w):
        return jnp.einsum("ldh,pg->lpdgh", w, eye).reshape(half * KV_WIDTH, KV_GROUPS * CMP_HIDDEN)

    def pos_rows(p):
        r = jnp.broadcast_to(p[:, None, :], (half, KV_GROUPS, HEAD_DIM)).reshape(1, half * KV_WIDTH)
        return jnp.concatenate([r, jnp.zeros((SUBLANES - 1, half * KV_WIDTH), F32)], axis=0)

    w1a = jnp.stack([w1_blocks(w1[i, :half]) for i in range(2)]).astype(BF16)
    w1b = jnp.stack([w1_blocks(w1[i, half:]) for i in range(2)]).astype(BF16)
    posr = jnp.stack([jnp.stack([pos_rows(pos[i, :half]), pos_rows(pos[i, half:])])
                      for i in range(2)]).astype(BF16)
    b1t = jnp.tile(b1, (1, KV_GROUPS))[:, None, :]
    w2blk = lambda w: jnp.einsum("hd,pg->phgd", w, eye).reshape(KV_GROUPS * CMP_HIDDEN, KV_WIDTH)
    return dict(w1a=w1a, w1b=w1b, pos=posr, b1=b1t,
                w2k=w2blk(w2[0]).astype(BF16), b2k=jnp.tile(b2[0], KV_GROUPS)[None, :],
                w2v=w2blk(w2[1]).astype(BF16), b2v=jnp.tile(b2[1], KV_GROUPS)[None, :])


CMP_PAD = 64
CMP_BAND_BACK = 56
CMP_BAND = CMP_BAND_BACK + T_CMP // CMP_STRIDE
CMP_BAND_DIST0 = CMP_BAND_BACK * CMP_STRIDE - (CMP_LEN - 1)
assert CMP_BAND_DIST0 + CMP_STRIDE >= FAR_DIST and CMP_BAND_BACK <= CMP_PAD


def _gate_column(head, branch):
    return head * 3 + branch


def _cmp_kernel(gate_ref, qa_ref, kc_ref, vct_ref, band_ref, far_ref, wagg_ref, ocmp_ref, qaug_ref,
                st_ref, gt_ref):
    tq = qa_ref.shape[2]
    nb = kc_ref.shape[1]
    ns = wagg_ref.shape[0]
    i = pl.program_id(1)
    g = pl.program_id(0)
    s0 = i * tq
    t = s0 + lax.broadcasted_iota(jnp.int32, (1, tq), 1)
    crow = lax.broadcasted_iota(jnp.int32, (nb, 1), 0)
    valid = crow * CMP_STRIDE + (CMP_LEN - 1) <= t
    band_row0 = pl.multiple_of(s0 // CMP_STRIDE - CMP_BAND_BACK + CMP_PAD, SUBLANES)
    c_lo = s0 // CMP_STRIDE - CMP_BAND_BACK
    kc = kc_ref[0]
    gt_ref[...] = gate_ref[0].T

    imp = jnp.zeros((nb, tq), F32)
    outs = []
    for h in range(HEADS_PER_GROUP):
        st_ref[CMP_PAD:, :] = _dot_nt(kc, qa_ref[0, h])
        st_ref[pl.ds(band_row0, CMP_BAND), :] += band_ref[h]
        far = far_ref[g * HEADS_PER_GROUP + h]
        s = st_ref[CMP_PAD:, :] + jnp.where(crow < c_lo, far, 0.0)
        s = jnp.where(valid, s, NEG_INIT)
        m = jnp.max(s, axis=0, keepdims=True)
        e = jnp.where(valid, jnp.exp(s - m), 0.0)
        p = e / jnp.maximum(jnp.sum(e, axis=0, keepdims=True), 1.0)
        imp = imp + p
        gate = gt_ref[pl.ds(_gate_column(g * HEADS_PER_GROUP + h, 0), 1), :]
        outs.append(gate * _dot(vct_ref[0], p.astype(BF16)))
    ocmp_ref[0] = jnp.concatenate(outs, axis=0).T

    p_sel = _dot_f32_left(wagg_ref[...], imp)
    j = lax.broadcasted_iota(jnp.int32, (ns, 1), 0)
    rel = jnp.right_shift(t, _log2(SEL_BLOCK)) - j
    forced = (j == 0) | ((rel >= 0) & (rel < N_LOCAL_SEL))
    score = jnp.where(forced, FORCE_SCORE, jnp.where(j * SEL_BLOCK <= t, p_sel, -1.0))
    jfull = jnp.broadcast_to(j, (ns, tq)).astype(F32)
    sel = jnp.zeros((ns, tq), F32)
    for _ in range(SEL_TOPK):
        m = jnp.max(score, axis=0, keepdims=True)
        first = jnp.min(jnp.where(score == m, jfull, float(ns)), axis=0, keepdims=True)
        hit = jfull == first
        sel = jnp.where(hit, 1.0, sel)
        score = jnp.where(hit, -3e38, score)
    notsel = jnp.where(sel.T > 0.5, 0.0, MASK_BIAS).astype(BF16)
    for h in range(HEADS_PER_GROUP):
        qaug_ref[0, h, :, :LANES] = qa_ref[0, h]
        qaug_ref[0, h, :, LANES:] = notsel


def _cmp_attention(gates, qa, kc, vct, band, far, wagg):
    b, _, s, _ = qa.shape
    nb = kc.shape[1]
    tq = T_CMP
    grid = (KV_GROUPS, s // tq, b)
    return pl.pallas_call(
        _cmp_kernel, grid=grid,
        in_specs=[
            pl.BlockSpec((1, tq, LANES), lambda g, i, bi: (bi, i, 0)),
            pl.BlockSpec((1, HEADS_PER_GROUP, tq, LANES), lambda g, i, bi: (bi, g, i, 0)),
            pl.BlockSpec((1, nb, KV_WIDTH), lambda g, i, bi: (bi, 0, 0)),
            pl.BlockSpec((1, HEAD_DIM, nb), lambda g, i, bi: (bi, g, 0)),
            pl.BlockSpec((HEADS_PER_GROUP, CMP_BAND, tq), lambda g, i, bi: (g, 0, 0)),
            pl.BlockSpec(memory_space=pltpu.SMEM),
            pl.BlockSpec(wagg.shape, lambda g, i, bi: (0, 0)),
        ],
        out_specs=(
            pl.BlockSpec((1, tq, HEADS_PER_GROUP * HEAD_DIM), lambda g, i, bi: (bi, i, g)),
            pl.BlockSpec((1, HEADS_PER_GROUP, tq, 2 * LANES), lambda g, i, bi: (bi, g, i, 0)),
        ),
        out_shape=(jax.ShapeDtypeStruct((b, s, NSA_WIDTH), F32),
                   jax.ShapeDtypeStruct((b, NSA_HEADS, s, 2 * LANES), BF16)),
        scratch_shapes=[pltpu.VMEM((CMP_PAD + nb, tq), F32), pltpu.VMEM((LANES, tq), F32)],
        compiler_params=_params("parallel", "parallel", "parallel"), name="cmp_attention",
    )(gates, qa, kc, vct, band, far, wagg)


def _cmp_tables(rel_bias, nb):
    jj = np.arange(CMP_BAND)[:, None]
    ii = np.arange(T_CMP)[None, :]
    dist = CMP_BAND_DIST0 + ii - CMP_STRIDE * jj
    bidx = _BUCKETS[np.clip(dist, 0, None)]
    band = jnp.transpose(rel_bias[bidx], (2, 0, 1))
    band = jnp.where(jnp.asarray(dist >= 0)[None], band, 0.0)
    far = rel_bias[REL_BUCKETS - 1]
    ns = nb // (SEL_BLOCK // CMP_STRIDE)
    assert SEL_TOPK <= ns <= LANES
    wagg = np.zeros((LANES, nb), np.float32)
    for m, w in enumerate(SEL_AGG):
        for jb in range(ns):
            c = (SEL_BLOCK // CMP_STRIDE) * jb + m - 1
            if 0 <= c < nb - 1:
                wagg[jb, c] = w
    return band, far, jnp.asarray(wagg, BF16)


N_NEAR = (FAR_DIST + T_ATT - 1) // T_ATT + 1
assert N_NEAR * T_ATT - (T_ATT - 1) >= FAR_DIST
WIN_TILES = (WINDOW - 1 + T_ATT - 1) // T_ATT
TOEPLITZ_ROWS = 8
assert max(N_NEAR, WIN_TILES + 1) <= TOEPLITZ_ROWS
FAR_KEYS = 2 * T_ATT


def _flash_kernel(gate_ref, q_ref, kt_ref, v_ref, gv_ref, o_ref, m_ref, acc_ref, nb_ref, *,
                  windowed, branch):
    t = T_ATT
    g = pl.program_id(1)
    i = pl.program_id(2)
    n_tab = WIN_TILES + 1 if windowed else N_NEAR

    @pl.when(i == 0)
    def _():
        for h in range(HEADS_PER_GROUP):
            for d in range(n_tab):
                gen = jnp.broadcast_to(gv_ref[0, h, d:d + 1, :], (t, 2 * t))
                nb_ref[h, d] = pltpu.roll(gen, 0, 1, stride=1, stride_axis=0)[:, :t]

    m_ref[...] = jnp.full(m_ref.shape, NEG_INIT, F32)
    acc_ref[...] = jnp.zeros(acc_ref.shape, F32)

    def tile_step(first_tile, n_tiles, dists=(), causal=False):
        heads = range(HEADS_PER_GROUP)
        assert not dists or len(dists) == n_tiles
        nk = t * n_tiles
        start = pl.multiple_of(first_tile * t, t)
        kt = kt_ref[0, :, pl.ds(start, nk)]
        v = v_ref[0, 0, pl.ds(start, nk), :]
        s = [_dot(q_ref[0, h], kt) for h in heads]
        if dists:
            s = [s[h] + jnp.concatenate([nb_ref[h, d] for d in dists], axis=1) for h in heads]
            dist = (dists[0] * t + lax.broadcasted_iota(jnp.int32, (t, nk), 0)
                    - lax.broadcasted_iota(jnp.int32, (t, nk), 1))
            keep = None
            if causal:
                keep = dist >= 0
            elif windowed and dists[0] * t + t - 1 >= WINDOW:
                keep = dist < WINDOW
            if keep is not None:
                s = [jnp.where(keep, s[h], MASK_BIAS) for h in heads]
        m_prev = [m_ref[h] for h in heads]
        m_new = [jnp.maximum(m_prev[h], jnp.max(s[h], axis=1, keepdims=True)) for h in heads]
        p = [jnp.exp(s[h] - jnp.tile(m_new[h], (1, nk // LANES))).astype(BF16) for h in heads]
        pv = [_dot(p[h], v) for h in heads]
        for h in heads:
            acc_ref[h] = jnp.exp(m_prev[h] - m_new[h]) * acc_ref[h] + pv[h]
            m_ref[h] = m_new[h]

    def near_steps(first):
        assert first % 2 == 0

        @pl.when(i >= first)
        def _():
            for d in range(first, 0, -2):
                tile_step(i - d, 2, (d, d - 1))

        @pl.when(i < first)
        def _():
            for d in range(first - 1, 0, -1):
                @pl.when(i >= d)
                def _():
                    tile_step(i - d, 1, (d,))

    if windowed:
        near_steps(WIN_TILES)
    else:
        n_far = jnp.maximum(i - (N_NEAR - 1), 0)
        far_tiles = FAR_KEYS // t

        def far_body(j, carry):
            tile_step(j * far_tiles, far_tiles)
            return carry
        n_far_steps = jnp.right_shift(n_far, _log2(far_tiles))
        lax.fori_loop(0, n_far_steps, far_body, 0)
        for rem in range(far_tiles - 1, 0, -1):
            @pl.when(n_far - n_far_steps * far_tiles >= rem)
            def _():
                tile_step(n_far - rem, 1)
        near_steps(N_NEAR - 1)
    tile_step(i, 1, (0,), causal=True)

    lane = lax.broadcasted_iota(jnp.int32, (t, LANES), 1)
    gates = gate_ref[0]
    outs = []
    for h in range(HEADS_PER_GROUP):
        col = _gate_column(g * HEADS_PER_GROUP + h, branch)
        gate = jnp.sum(jnp.where(lane == col, gates, 0.0), axis=1, keepdims=True)
        acc = acc_ref[h]
        outs.append(acc[:, :HEAD_DIM] / jnp.maximum(acc[:, HEAD_DIM:HEAD_DIM + 1], 1.0) * gate)
    o_ref[0] = jnp.concatenate(outs, axis=1)


def _flash(gates, q_aug, kt, v, gv, *, windowed, branch):
    b, _, s, _ = q_aug.shape
    kd = kt.shape[1]
    t = T_ATT
    grid = (b, KV_GROUPS, s // t)
    return pl.pallas_call(
        functools.partial(_flash_kernel, windowed=windowed, branch=branch), grid=grid,
        in_specs=[
            pl.BlockSpec((1, t, LANES), lambda bi, g, i: (bi, i, 0)),
            pl.BlockSpec((1, HEADS_PER_GROUP, t, kd), lambda bi, g, i: (bi, g, i, 0)),
            pl.BlockSpec((1, kd, s), lambda bi, g, i: (bi, 0, 0)),
            pl.BlockSpec((1, 1, s, LANES), lambda bi, g, i: (bi, g, 0, 0)),
            pl.BlockSpec((1, HEADS_PER_GROUP, TOEPLITZ_ROWS, 2 * t), lambda bi, g, i: (g, 0, 0, 0)),
        ],
        out_specs=pl.BlockSpec((1, t, HEADS_PER_GROUP * HEAD_DIM), lambda bi, g, i: (bi, i, g)),
        out_shape=jax.ShapeDtypeStruct((b, s, NSA_WIDTH), F32),
        scratch_shapes=[pltpu.VMEM((HEADS_PER_GROUP, t, LANES), F32),
                        pltpu.VMEM((HEADS_PER_GROUP, t, LANES), F32),
                        pltpu.VMEM((HEADS_PER_GROUP, TOEPLITZ_ROWS, t, t), F32)],
        compiler_params=_params("parallel", "parallel", "arbitrary"),
        name="win_attention" if windowed else "sel_attention",
    )(gates, q_aug, kt, v, gv)


def _toeplitz_generators(rel_bias):
    t = T_ATT
    d = np.arange(TOEPLITZ_ROWS)[:, None]
    y = np.arange(2 * t)[None, :]
    dist = np.where(y < t, d * t - y, d * t + 2 * t - y)
    tab = rel_bias[_BUCKETS[np.clip(dist, 0, None)]] - rel_bias[REL_BUCKETS - 1]
    tab = jnp.where(jnp.asarray(dist >= 0)[..., None], tab, 0.0)
    return jnp.transpose(tab, (2, 0, 1)).reshape(KV_GROUPS, HEADS_PER_GROUP, TOEPLITZ_ROWS, 2 * t)


def _pool_kernel(x_ref, halo_ref, w_ref, scale_ref, o_ref):
    t = x_ref.shape[1]
    i = pl.program_id(1)
    x = x_ref[0]
    halo = jnp.where(i > 0, halo_ref[0], 0.0)
    xs = jnp.concatenate([halo, x], axis=0)
    sums = []
    acc = xs
    span = 1
    for win in POOL_WINDOWS:
        while span < win:
            acc = acc + pltpu.roll(acc, span, 0)
            span *= 2
        sums.append(acc[HALO:])
    lane_grp = jnp.right_shift(lax.broadcasted_iota(jnp.int32, (t, POOL_WIDTH), 1), _log2(POOL_CH))
    tok = i * t + lax.broadcasted_iota(jnp.int32, (t, POOL_WIDTH), 0)
    ssum = sums[-1]
    win = jnp.full((t, POOL_WIDTH), POOL_WINDOWS[-1], jnp.int32)
    for gi in range(len(POOL_WINDOWS) - 2, -1, -1):
        ssum = jnp.where(lane_grp == gi, sums[gi], ssum)
        win = jnp.where(lane_grp == gi, POOL_WINDOWS[gi], win)
    cnt = jnp.minimum(tok + 1, win).astype(F32)
    dlt = (ssum / cnt - x).astype(BF16)
    o_ref[0] = (_dot(dlt, w_ref[...]) * scale_ref[...]).astype(BF16)


def _pool_mixer(p, wblk, scale):
    b, s, c = p.shape
    t = T_MIX
    return pl.pallas_call(
        _pool_kernel, grid=(b, s // t),
        in_specs=[pl.BlockSpec((1, t, c), lambda bi, i: (bi, i, 0)),
                  pl.BlockSpec((1, HALO, c), lambda bi, i: (bi, jnp.maximum(i * (t // HALO) - 1, 0), 0)),
                  pl.BlockSpec((c, c), lambda bi, i: (0, 0)),
                  pl.BlockSpec((1, c), lambda bi, i: (0, 0))],
        out_specs=pl.BlockSpec((1, t, c), lambda bi, i: (bi, i, 0)),
        out_shape=jax.ShapeDtypeStruct((b, s, c), BF16),
        compiler_params=_params("parallel", "parallel"), name="pool_mixer",
    )(p, p, wblk, scale)


def _sgu_kernel(u_ref, v_ref, ng_ref, nb_ref, ws_ref, bs_ref, o_ref):
    t = u_ref.shape[1]
    v = _gelu(v_ref[0])
    mu = jnp.mean(v, axis=-1, keepdims=True)
    var = jnp.mean(jnp.square(v - mu), axis=-1, keepdims=True)
    vn = (v - mu) * lax.rsqrt(var + EPS) * ng_ref[...] + nb_ref[...]
    lane_grp = jnp.right_shift(lax.broadcasted_iota(jnp.int32, (SGU_CHUNK, SGU_WIDTH), 1),
                               _log2(SGU_CH))
    r = lax.broadcasted_iota(jnp.int32, (SGU_CHUNK, SGU_GROUPS * SGU_CHUNK), 0)
    c = lax.broadcasted_iota(jnp.int32, (SGU_CHUNK, SGU_GROUPS * SGU_CHUNK), 1) & (SGU_CHUNK - 1)
    ws = jnp.where(c <= r, ws_ref[...], 0.0).astype(BF16)
    for n in range(t // SGU_CHUNK):
        vc = vn[n * SGU_CHUNK:(n + 1) * SGU_CHUNK]
        stacked = jnp.concatenate(
            [jnp.where(lane_grp == gi, vc, 0.0) for gi in range(SGU_GROUPS)], axis=0).astype(BF16)
        mixed = _dot(ws, stacked) + bs_ref[...]
        u = _gelu(u_ref[0, n * SGU_CHUNK:(n + 1) * SGU_CHUNK, :])
        o_ref[0, n * SGU_CHUNK:(n + 1) * SGU_CHUNK, :] = (u * mixed).astype(BF16)


def _sgu_mixer(u, v, ng, nbias, ws_cat, bs_exp):
    b, s, c = u.shape
    t = T_MIX
    row = pl.BlockSpec((1, t, c), lambda bi, i: (bi, i, 0))
    full = lambda a: pl.BlockSpec(a.shape, lambda bi, i: (0,) * a.ndim)
    return pl.pallas_call(
        _sgu_kernel, grid=(b, s // t),
        in_specs=[row, row, full(ng), full(nbias), full(ws_cat), full(bs_exp)],
        out_specs=row, out_shape=jax.ShapeDtypeStruct((b, s, c), BF16),
        compiler_params=_params("parallel", "parallel"), name="sgu_mixer",
    )(u, v, ng, nbias, ws_cat, bs_exp)


def _out_proj_kernel(oc_ref, os_ref, ow_ref, pool_ref, sgu_ref, w_ref, g_ref, x_ref, y_ref):
    a = (oc_ref[0] + os_ref[0] + ow_ref[0]).astype(BF16)
    mix = (_dot(a, w_ref[:NSA_WIDTH, :])
           + _dot(pool_ref[0], w_ref[NSA_WIDTH:NSA_WIDTH + POOL_WIDTH, :])
           + _dot(sgu_ref[0], w_ref[NSA_WIDTH + POOL_WIDTH:, :]))
    y_ref[0] = x_ref[0] + _rms(mix, g_ref[...])


def _out_proj(o_cmp, o_sel, o_win, pool, sgu, w, g, x):
    b, s, d = x.shape
    t = T_OUT
    row = lambda n: pl.BlockSpec((1, t, n), lambda bi, i: (bi, i, 0))
    full = lambda a: pl.BlockSpec(a.shape, lambda bi, i: (0,) * a.ndim)
    return pl.pallas_call(
        _out_proj_kernel, grid=(b, s // t),
        in_specs=[row(NSA_WIDTH), row(NSA_WIDTH), row(NSA_WIDTH),
                  row(POOL_WIDTH), row(SGU_WIDTH), full(w), full(g), row(d)],
        out_specs=row(d), out_shape=jax.ShapeDtypeStruct((b, s, d), F32),
        compiler_params=_params("parallel", "parallel"), name="out_proj",
    )(o_cmp, o_sel, o_win, pool, sgu, w, g, x)


def _mem_kv_kernel(mem_ref, g_ref, wk_ref, wv_ref, k_ref, v_ref):
    m = _rms(mem_ref[0], g_ref[...]).astype(BF16)
    k_ref[0] = _dot(m, wk_ref[...]).astype(BF16)
    v_ref[0] = _dot(m, wv_ref[...]).astype(BF16)


def _mem_kv(mem, g, wk, wv):
    b, m, d = mem.shape
    full = lambda a: pl.BlockSpec(a.shape, lambda bi: (0,) * a.ndim)
    blk = pl.BlockSpec((1, m, d), lambda bi: (bi, 0, 0))
    return pl.pallas_call(
        _mem_kv_kernel, grid=(b,),
        in_specs=[blk, full(g), full(wk), full(wv)],
        out_specs=(blk, blk),
        out_shape=(jax.ShapeDtypeStruct((b, m, d), BF16), jax.ShapeDtypeStruct((b, m, d), BF16)),
        compiler_params=_params("parallel"), name="mem_kv",
    )(mem, g, wk, wv)


def _mem_attn_kernel(x_ref, gpre_ref, wq_ref, k_ref, v_ref, wo_ref, gpost_ref, y_ref):
    x = x_ref[0]
    h = _rms(x, gpre_ref[...]).astype(BF16)
    q = (_dot(h, wq_ref[...]) * (MEM_HEAD_DIM ** -0.5)).astype(BF16)
    outs = []
    for hd in range(MEM_HEADS):
        sl = slice(hd * MEM_HEAD_DIM, (hd + 1) * MEM_HEAD_DIM)
        s = _dot_nt(q[:, sl], k_ref[0, :, sl])
        e = jnp.exp(s - jnp.max(s, axis=-1, keepdims=True))
        p = e / jnp.sum(e, axis=-1, keepdims=True)
        outs.append(_dot(p.astype(BF16), v_ref[0, :, sl]).astype(BF16))
    o = jnp.concatenate(outs, axis=1)
    y_ref[0] = x + _rms(_dot(o, wo_ref[...]), gpost_ref[...])


def _mem_attn(x, gpre, wq, k, v, wo, gpost):
    b, s, d = x.shape
    m = k.shape[1]
    t = T_MEM
    row = pl.BlockSpec((1, t, d), lambda bi, i: (bi, i, 0))
    full = lambda a: pl.BlockSpec(a.shape, lambda bi, i: (0,) * a.ndim)
    kv = pl.BlockSpec((1, m, d), lambda bi, i: (bi, 0, 0))
    return pl.pallas_call(
        _mem_attn_kernel, grid=(b, s // t),
        in_specs=[row, full(gpre), full(wq), kv, kv, full(wo), full(gpost)],
        out_specs=row, out_shape=jax.ShapeDtypeStruct((b, s, d), F32),
        compiler_params=_params("parallel", "parallel"), name="mem_attention",
    )(x, gpre, wq, k, v, wo, gpost)


def _ffn_kernel(x_ref, halo_ref, gpre_ref, wup_ref, cw_ref, cb_ref, wd_ref, gpost_ref, y_ref,
                act_ref):
    i = pl.program_id(1)
    f = wd_ref.shape[0]
    halo = jnp.where(i > 0, halo_ref[0], 0.0)
    h = _rms(jnp.concatenate([halo, x_ref[0]], axis=0), gpre_ref[...]).astype(BF16)

    def conv(off):
        a = _dot(h, wup_ref[:, off:off + F_CHUNK])
        cw = cw_ref[:, off:off + F_CHUNK]
        return (cw[0:1] * pltpu.roll(a, 2, 0) + cw[1:2] * pltpu.roll(a, 1, 0) + cw[2:3] * a
                + cb_ref[:, off:off + F_CHUNK])

    for off in range(0, f, F_CHUNK):
        act = _gelu(conv(off)) * conv(f + off)
        act_ref[:, off:off + F_CHUNK] = act[HALO:].astype(BF16)
    y_ref[0] = x_ref[0] + _rms(_dot(act_ref[...], wd_ref[...]), gpost_ref[...])


def _ffn(x, gpre, w_up, conv_w, conv_b, w_down, gpost):
    b, s, d = x.shape
    f = w_down.shape[0]
    t = T_FFN
    assert f % F_CHUNK == 0
    resident = lambda a: pl.BlockSpec(a.shape, lambda bi, i: (0,) * a.ndim,
                                      pipeline_mode=pl.Buffered(1))
    return pl.pallas_call(
        _ffn_kernel, grid=(b, s // t),
        in_specs=[
            pl.BlockSpec((1, t, d), lambda bi, i: (bi, i, 0)),
            pl.BlockSpec((1, HALO, d), lambda bi, i: (bi, jnp.maximum(i * (t // HALO) - 1, 0), 0)),
            resident(gpre), resident(w_up), resident(conv_w), resident(conv_b), resident(w_down),
            resident(gpost),
        ],
        out_specs=pl.BlockSpec((1, t, d), lambda bi, i: (bi, i, 0)),
        out_shape=jax.ShapeDtypeStruct((b, s, d), F32),
        scratch_shapes=[pltpu.VMEM((t, f), BF16)],
        compiler_params=_params("parallel", "parallel"), name="conv_ffn",
    )(x, x, gpre, w_up, conv_w, conv_b, w_down, gpost)


def kernel(x, mem, rel_bias, mix_norm_pre, mix_norm_post, w_in, cmp_pos, cmp_w1, cmp_b1, cmp_w2,
           cmp_b2, pool_w, pool_scale, sgu_norm_g, sgu_norm_b, sgu_w, sgu_b, w_out, mem_norm_pre,
           mem_norm_kv, mem_norm_post, w_mq, w_mk, w_mv, w_mo, ffn_norm_pre, ffn_norm_post, w_up,
           conv_w, conv_b, w_down):
    depth = w_in.shape[0]
    s = x.shape[1]
    assert s % max(T_IN, T_CMP, T_ATT, T_OUT, T_MIX, T_MEM, T_FFN) == 0
    nb = s // CMP_STRIDE

    w_in_p = (jnp.take(w_in, jnp.asarray(_IN_IDX), axis=2) * jnp.asarray(_IN_MASK)).astype(BF16)
    band, far, wagg = _cmp_tables(rel_bias, nb)
    toep = _toeplitz_generators(rel_bias)
    row = lambda a: a[:, None, :]
    bf = lambda a: a.astype(BF16)
    w_out_b, w_mq_b, w_mk_b, w_mv_b, w_mo_b = bf(w_out), bf(w_mq), bf(w_mk), bf(w_mv), bf(w_mo)
    w_up_b, w_down_b = bf(w_up), bf(w_down)

    for l in range(depth):
        qa, ksel, kwin, kcmp, vcmp, vsel, vwin, gates, pin, u, v = _in_proj(
            x, row(mix_norm_pre)[l], w_in_p[l])
        cw = _compress_weights(cmp_pos[l], cmp_w1[l], cmp_b1[l], cmp_w2[l], cmp_b2[l])
        kc, vct = _compress(kcmp, vcmp, cw)
        o_cmp, q_aug = _cmp_attention(gates, qa, kc, vct, band, far, wagg)
        o_sel = _flash(gates, q_aug, ksel, vsel, toep, windowed=False, branch=1)
        o_win = _flash(gates, qa, kwin, vwin, toep, windowed=True, branch=2)
        wblk = bf(jax.scipy.linalg.block_diag(*[pool_w[l, gi] for gi in range(len(POOL_WINDOWS))]))
        out_b = _pool_mixer(pin, wblk, row(pool_scale)[l])
        ws_cat = jnp.concatenate([sgu_w[l, gi] for gi in range(SGU_GROUPS)], axis=1)
        bs_exp = jnp.repeat(jnp.transpose(sgu_b[l]), SGU_CH, axis=1)
        out_c = _sgu_mixer(u, v, row(sgu_norm_g)[l], row(sgu_norm_b)[l], ws_cat, bs_exp)
        x = _out_proj(o_cmp, o_sel, o_win, out_b, out_c, w_out_b[l], row(mix_norm_post)[l], x)
        km, vm = _mem_kv(mem, row(mem_norm_kv)[l], w_mk_b[l], w_mv_b[l])
        x = _mem_attn(x, row(mem_norm_pre)[l], w_mq_b[l], km, vm, w_mo_b[l], row(mem_norm_post)[l])
        x = _ffn(x, row(ffn_norm_pre)[l], w_up_b[l], conv_w[l], row(conv_b)[l], w_down_b[l],
                 row(ffn_norm_post)[l])
    return x
```

```python
import functools
import math

import numpy as np
import jax
import jax.numpy as jnp
from jax import lax
from jax.experimental import pallas as pl
from jax.experimental.pallas import tpu as pltpu

F32 = jnp.float32
BF16 = jnp.bfloat16

D_MODEL = 1024
NSA_WIDTH = 512
POOL_WIDTH = 256
SGU_WIDTH = 256
NSA_HEADS = 8
KV_GROUPS = 2
HEADS_PER_GROUP = 4
HEAD_DIM = 64
KV_WIDTH = 128
GATE_WIDTH = 24
CMP_STRIDE = 16
CMP_LEN = 32
CMP_HIDDEN = 128
SEL_BLOCK = 64
SEL_TOPK = 16
N_LOCAL_SEL = 2
SEL_AGG = (1.0, 2.0, 2.0, 2.0, 1.0)
FORCE_SCORE = 1e9
WINDOW = 512
REL_BUCKETS = 32
REL_MAX_DIST = 1024
POOL_WINDOWS = (2, 4, 8, 16)
POOL_CH = 64
SGU_GROUPS = 4
SGU_CH = 64
SGU_CHUNK = 128
MEM_HEADS = 4
MEM_HEAD_DIM = 256
FFN_HIDDEN = 2816
EPS = 1e-6

LANES = 128
SUBLANES = 8
BF16_ROWS = 16
VMEM_LIMIT = 56 * 1024 * 1024

T_IN = 512
T_CMP = 256
T_ATT = 256
T_OUT = 512
T_MIX = 512
T_MEM = 512
T_FFN = 512
F_CHUNK = 256
HALO = 16

MASK_BIAS = -float(2 ** 30)
NEG_INIT = -1e30

_Q_OFF = 0
_KSEL_OFF = 1024
_KWIN_OFF = 1152
_KCMP_OFF = 1280
_VCMP_OFF = 1408
_VSEL_OFF = 1536
_VWIN_OFF = 1792
_GATE_OFF = 2048
_POOL_OFF = 2304
_U_OFF = 2560
_V_OFF = 2816
IN_COLS = 3072
MXU_COLS = 256


def _bucket_table(n_max):
    n = np.arange(n_max)
    max_exact = REL_BUCKETS // 2
    nf = np.maximum(n, 1).astype(np.float64)
    large = max_exact + (np.log(nf / max_exact) / math.log(REL_MAX_DIST / max_exact)
                         * (REL_BUCKETS - max_exact)).astype(np.int32)
    large = np.minimum(large, REL_BUCKETS - 1)
    return np.where(n < max_exact, n, large).astype(np.int32)


_BUCKETS = _bucket_table(16384)
FAR_DIST = int(np.argmax(_BUCKETS == REL_BUCKETS - 1))


def _in_proj_layout():
    moves = []
    for h in range(NSA_HEADS):
        g = h // HEADS_PER_GROUP
        moves.append((_Q_OFF + h * LANES + g * HEAD_DIM, h * HEAD_DIM, HEAD_DIM))
    base = NSA_WIDTH
    moves.append((_KCMP_OFF, base, KV_WIDTH))
    moves.append((_VCMP_OFF, base + KV_WIDTH, KV_WIDTH))
    moves.append((_KSEL_OFF, base + 2 * KV_WIDTH, KV_WIDTH))
    for g in range(KV_GROUPS):
        moves.append((_VSEL_OFF + g * LANES, base + 3 * KV_WIDTH + g * HEAD_DIM, HEAD_DIM))
    moves.append((_KWIN_OFF, base + 4 * KV_WIDTH, KV_WIDTH))
    for g in range(KV_GROUPS):
        moves.append((_VWIN_OFF + g * LANES, base + 5 * KV_WIDTH + g * HEAD_DIM, HEAD_DIM))
    base += 6 * KV_WIDTH
    moves.append((_GATE_OFF, base, GATE_WIDTH))
    base += GATE_WIDTH
    moves.append((_POOL_OFF, base, POOL_WIDTH))
    moves.append((_U_OFF, base + POOL_WIDTH, SGU_WIDTH))
    moves.append((_V_OFF, base + POOL_WIDTH + SGU_WIDTH, SGU_WIDTH))
    return sorted(moves)


def _in_proj_weights(w_in):
    depth, d, _ = w_in.shape
    pieces, pos = [], 0
    for dst, src, n in _in_proj_layout():
        if dst > pos:
            pieces.append(jnp.zeros((depth, d, dst - pos), BF16))
        pieces.append(w_in[:, :, src:src + n].astype(BF16))
        pos = dst + n
    pieces.append(jnp.zeros((depth, d, IN_COLS - pos), BF16))
    return jnp.concatenate(pieces, axis=2)


def _log2(n):
    assert n & (n - 1) == 0
    return n.bit_length() - 1


def _params(*sem):
    return pltpu.CompilerParams(dimension_semantics=sem, vmem_limit_bytes=VMEM_LIMIT)


def _rms(x, g):
    return x * lax.rsqrt(jnp.mean(x * x, axis=-1, keepdims=True) + EPS) * g


def _gelu(x):
    c = math.sqrt(2.0 / math.pi)
    return x * (0.5 * (1.0 + jnp.tanh(c * (x + 0.044715 * (x * x * x)))))


def _dot(a, b):
    return jnp.dot(a, b, preferred_element_type=F32)


def _dot_nt(a, b):
    return lax.dot_general(a, b, (((1,), (1,)), ((), ())), preferred_element_type=F32)


def _split3(x):
    hi = x.astype(BF16)
    r1 = x - hi.astype(F32)
    mid = r1.astype(BF16)
    lo = (r1 - mid.astype(F32)).astype(BF16)
    return hi, mid, lo


def _dot_f32_left(w_bf16, x):
    hi, mid, lo = _split3(x)
    return _dot(w_bf16, hi) + _dot(w_bf16, mid) + _dot(w_bf16, lo)


def _in_proj_kernel(x_ref, g_ref, w_ref, qa_ref, ksel_ref, kwin_ref, kcmp_ref, vcmp_ref,
                    vsel_ref, vwin_ref, gate_ref, pool_ref, u_ref, v_ref):
    t = x_ref.shape[1]
    h = _rms(x_ref[0], g_ref[...]).astype(BF16)

    def proj(off):
        return _dot(h, w_ref[:, off:off + MXU_COLS])

    def slabs(off):
        p = proj(off)
        return p[:, :LANES], p[:, LANES:]

    scale = HEAD_DIM ** -0.5
    for hd in range(0, NSA_HEADS, 2):
        qa, qb = slabs(_Q_OFF + hd * LANES)
        qa_ref[0, hd] = (qa * scale).astype(BF16)
        qa_ref[0, hd + 1] = (qb * scale).astype(BF16)

    lane = lax.broadcasted_iota(jnp.int32, (t, LANES), 1)
    ksel, kwin = slabs(_KSEL_OFF)
    ksel_ref[0, :LANES, :] = ksel.T.astype(BF16)
    tok = pl.program_id(1) * t + lax.broadcasted_iota(jnp.int32, (LANES, t), 1)
    blk_id = lax.broadcasted_iota(jnp.int32, (LANES, t), 0)
    ksel_ref[0, LANES:, :] = jnp.where(
        jnp.right_shift(tok, _log2(SEL_BLOCK)) == blk_id, 1.0, 0.0).astype(BF16)
    kwin_ref[0] = kwin.T.astype(BF16)
    kcmp, vcmp = slabs(_KCMP_OFF)
    kcmp_ref[0] = kcmp.astype(BF16)
    vcmp_ref[0] = vcmp.astype(BF16)
    for v_out, off in ((vsel_ref, _VSEL_OFF), (vwin_ref, _VWIN_OFF)):
        for g, vg in enumerate(slabs(off)):
            v_out[0, g] = jnp.where(lane == HEAD_DIM, 1.0, vg).astype(BF16)
    gate_ref[0] = jax.nn.sigmoid(slabs(_GATE_OFF)[0])
    pool_ref[0] = proj(_POOL_OFF)
    u_ref[0] = proj(_U_OFF)
    v_ref[0] = proj(_V_OFF)


def _in_proj(x, g, w):
    b, s, d = x.shape
    t = T_IN
    grid = (b, s // t)
    row3 = lambda n: pl.BlockSpec((1, t, n), lambda bi, i: (bi, i, 0))
    out_shape = (
        jax.ShapeDtypeStruct((b, NSA_HEADS, s, LANES), BF16),
        jax.ShapeDtypeStruct((b, 2 * LANES, s), BF16),
        jax.ShapeDtypeStruct((b, LANES, s), BF16),
        jax.ShapeDtypeStruct((b, s, LANES), BF16),
        jax.ShapeDtypeStruct((b, s, LANES), BF16),
        jax.ShapeDtypeStruct((b, KV_GROUPS, s, LANES), BF16),
        jax.ShapeDtypeStruct((b, KV_GROUPS, s, LANES), BF16),
        jax.ShapeDtypeStruct((b, s, LANES), F32),
        jax.ShapeDtypeStruct((b, s, POOL_WIDTH), F32),
        jax.ShapeDtypeStruct((b, s, SGU_WIDTH), F32),
        jax.ShapeDtypeStruct((b, s, SGU_WIDTH), F32),
    )
    out_specs = (
        pl.BlockSpec((1, NSA_HEADS, t, LANES), lambda bi, i: (bi, 0, i, 0)),
        pl.BlockSpec((1, 2 * LANES, t), lambda bi, i: (bi, 0, i)),
        pl.BlockSpec((1, LANES, t), lambda bi, i: (bi, 0, i)),
        row3(LANES), row3(LANES),
        pl.BlockSpec((1, KV_GROUPS, t, LANES), lambda bi, i: (bi, 0, i, 0)),
        pl.BlockSpec((1, KV_GROUPS, t, LANES), lambda bi, i: (bi, 0, i, 0)),
        row3(LANES), row3(POOL_WIDTH), row3(SGU_WIDTH), row3(SGU_WIDTH),
    )
    return pl.pallas_call(
        _in_proj_kernel, grid=grid,
        in_specs=[row3(d),
                  pl.BlockSpec((1, d), lambda bi, i: (0, 0)),
                  pl.BlockSpec((d, IN_COLS), lambda bi, i: (0, 0))],
        out_specs=out_specs, out_shape=out_shape,
        compiler_params=_params("parallel", "parallel"), name="in_proj",
    )(x, g, w)


def _compress_kernel(rk_ref, rv_ref, w1a_ref, w1b_ref, pos_ref, b1_ref, w2k_ref, b2k_ref,
                     w2v_ref, b2v_ref, kc_ref, vct_ref):
    nb = rk_ref.shape[1]
    keep = lax.broadcasted_iota(jnp.int32, (nb, LANES), 0) < nb - 1

    def hidden(r, which):
        a = _dot(r, w1a_ref[which])
        bm = _dot(r, w1b_ref[which])
        pre = a + pltpu.roll(bm, nb - 1, 0)
        posb = (_dot(pos_ref[which, 0], w1a_ref[which])[0:1]
                + _dot(pos_ref[which, 1], w1b_ref[which])[0:1])
        return _gelu(pre + posb + b1_ref[which]).astype(BF16)

    kc = _dot(hidden(rk_ref[0], 0), w2k_ref[...]) + b2k_ref[...]
    kc_ref[0] = jnp.where(keep, kc, 0.0).astype(BF16)
    vc = _dot(hidden(rv_ref[0], 1), w2v_ref[...]) + b2v_ref[...]
    vct_ref[0] = jnp.where(keep, vc, 0.0).T.astype(BF16)


def _compress(kcmp, vcmp, cw):
    b, s, _ = kcmp.shape
    nb = s // CMP_STRIDE
    rk = kcmp.reshape(b, nb, CMP_STRIDE * KV_WIDTH)
    rv = vcmp.reshape(b, nb, CMP_STRIDE * KV_WIDTH)
    full = lambda a: pl.BlockSpec(a.shape, lambda bi: (0,) * a.ndim)
    ws = (cw["w1a"], cw["w1b"], cw["pos"], cw["b1"], cw["w2k"], cw["b2k"], cw["w2v"], cw["b2v"])
    return pl.pallas_call(
        _compress_kernel, grid=(b,),
        in_specs=[pl.BlockSpec((1, nb, rk.shape[2]), lambda bi: (bi, 0, 0)),
                  pl.BlockSpec((1, nb, rv.shape[2]), lambda bi: (bi, 0, 0))]
                 + [full(a) for a in ws],
        out_specs=(pl.BlockSpec((1, nb, KV_WIDTH), lambda bi: (bi, 0, 0)),
                   pl.BlockSpec((1, KV_WIDTH, nb), lambda bi: (bi, 0, 0))),
        out_shape=(jax.ShapeDtypeStruct((b, nb, KV_WIDTH), BF16),
                   jax.ShapeDtypeStruct((b, KV_WIDTH, nb), BF16)),
        compiler_params=_params("parallel"), name="compress",
    )(rk, rv, *ws)


def _compress_weights(pos, w1, b1, w2, b2):
    eye = jnp.eye(KV_GROUPS, dtype=F32)
    half = CMP_LEN // 2

    def w1_blocks(w):
        return jnp.einsum("ldh,pg->lpdgh", w, eye).reshape(half * KV_WIDTH, KV_GROUPS * CMP_HIDDEN)

    def pos_rows(p):
        r = jnp.broadcast_to(p[:, None, :], (half, KV_GROUPS, HEAD_DIM)).reshape(1, half * KV_WIDTH)
        return jnp.concatenate([r, jnp.zeros((SUBLANES - 1, half * KV_WIDTH), F32)], axis=0)

    w1a = jnp.stack([w1_blocks(w1[i, :half]) for i in range(2)]).astype(BF16)
    w1b = jnp.stack([w1_blocks(w1[i, half:]) for i in range(2)]).astype(BF16)
    posr = jnp.stack([jnp.stack([pos_rows(pos[i, :half]), pos_rows(pos[i, half:])])
                      for i in range(2)]).astype(BF16)
    b1t = jnp.tile(b1, (1, KV_GROUPS))[:, None, :]
    w2blk = lambda w: jnp.einsum("hd,pg->phgd", w, eye).reshape(KV_GROUPS * CMP_HIDDEN, KV_WIDTH)
    return dict(w1a=w1a, w1b=w1b, pos=posr, b1=b1t,
                w2k=w2blk(w2[0]).astype(BF16), b2k=jnp.tile(b2[0], KV_GROUPS)[None, :],
                w2v=w2blk(w2[1]).astype(BF16), b2v=jnp.tile(b2[1], KV_GROUPS)[None, :])


CMP_PAD = 64
CMP_BAND_BACK = 56
CMP_BAND = CMP_BAND_BACK + T_CMP // CMP_STRIDE
CMP_BAND_DIST0 = CMP_BAND_BACK * CMP_STRIDE - (CMP_LEN - 1)
assert CMP_BAND_DIST0 + CMP_STRIDE >= FAR_DIST and CMP_BAND_BACK <= CMP_PAD
CMP_ROW_CHUNK = 128
SEL_RATIO = SEL_BLOCK // CMP_STRIDE


def _gate_column(head, branch):
    return head * 3 + branch


def _cmp_kernel(gate_ref, qa_ref, kc_ref, vct_ref, band_ref, wagg_ref, ocmp_ref, qaug_ref,
                st_ref, gt_ref):
    tq = qa_ref.shape[2]
    nb = kc_ref.shape[1]
    i = pl.program_id(1)
    g = pl.program_id(0)
    s0 = i * tq
    t = s0 + lax.broadcasted_iota(jnp.int32, (1, tq), 1)
    n_live = (i + 1) * (tq // CMP_STRIDE)
    band_row0 = pl.multiple_of(s0 // CMP_STRIDE - CMP_BAND_BACK + CMP_PAD, SUBLANES)
    gt_ref[...] = gate_ref[0].T
    st_ref[:CMP_PAD, :] = jnp.zeros((CMP_PAD, tq), F32)
    has_block = t >= CMP_LEN - 1

    def body(rows):
        nsr = rows // SEL_RATIO
        kc = kc_ref[0, :rows, :]
        tail = rows - CMP_ROW_CHUNK
        dead = tail + lax.broadcasted_iota(jnp.int32, (CMP_ROW_CHUNK, 1), 0) >= n_live
        imp = jnp.zeros((rows, tq), F32)
        outs = []
        for h in range(HEADS_PER_GROUP):
            st_ref[CMP_PAD:CMP_PAD + rows, :] = _dot_nt(kc, qa_ref[0, h])
            st_ref[pl.ds(band_row0, CMP_BAND), :] += band_ref[h]
            last = jnp.where(dead, MASK_BIAS, st_ref[CMP_PAD + tail:CMP_PAD + rows, :])
            s = last if tail == 0 else jnp.concatenate([st_ref[CMP_PAD:CMP_PAD + tail, :], last], axis=0)
            e = jnp.exp(s - jnp.max(s, axis=0, keepdims=True))
            denom = jnp.maximum(jnp.sum(e, axis=0, keepdims=True), 1.0)
            rinv = jnp.where(has_block, 1.0 / denom, 0.0)
            imp = imp + e * rinv
            gate = gt_ref[pl.ds(_gate_column(g * HEADS_PER_GROUP + h, 0), 1), :]
            outs.append((gate * rinv) * _dot(vct_ref[0, :, :rows], e.astype(BF16)))
        ocmp_ref[0] = jnp.concatenate(outs, axis=0).T

        p_sel = _dot_f32_left(wagg_ref[:nsr, :rows], imp)
        j = lax.broadcasted_iota(jnp.int32, (nsr, 1), 0)
        rel = jnp.right_shift(t, _log2(SEL_BLOCK)) - j
        forced = (j == 0) | ((rel >= 0) & (rel < N_LOCAL_SEL))
        score = jnp.where(forced, FORCE_SCORE, jnp.where(j * SEL_BLOCK <= t, p_sel, -1.0))
        jfull = jnp.broadcast_to(j, (nsr, tq)).astype(F32)
        sel = jnp.zeros((nsr, tq), F32)
        for _ in range(SEL_TOPK):
            m = jnp.max(score, axis=0, keepdims=True)
            first = jnp.min(jnp.where(score == m, jfull, float(nsr)), axis=0, keepdims=True)
            hit = jfull == first
            sel = jnp.where(hit, 1.0, sel)
            score = jnp.where(hit, -3e38, score)
        notsel = jnp.where(sel > 0.5, 0.0, MASK_BIAS)
        if nsr < LANES:
            notsel = jnp.concatenate(
                [notsel, jnp.full((LANES - nsr, tq), MASK_BIAS, F32)], axis=0)
        notsel = notsel.T.astype(BF16)
        for h in range(HEADS_PER_GROUP):
            qaug_ref[0, h, :, :LANES] = qa_ref[0, h]
            qaug_ref[0, h, :, LANES:] = notsel

    for rows in range(CMP_ROW_CHUNK, nb + 1, CMP_ROW_CHUNK):
        @pl.when((n_live > rows - CMP_ROW_CHUNK) & (n_live <= rows))
        def _():
            body(rows)


def _cmp_attention(gates, qa, kc, vct, band, wagg):
    b, _, s, _ = qa.shape
    nb = kc.shape[1]
    tq = T_CMP
    assert nb % CMP_ROW_CHUNK == 0
    grid = (KV_GROUPS, s // tq, b)
    return pl.pallas_call(
        _cmp_kernel, grid=grid,
        in_specs=[
            pl.BlockSpec((1, tq, LANES), lambda g, i, bi: (bi, i, 0)),
            pl.BlockSpec((1, HEADS_PER_GROUP, tq, LANES), lambda g, i, bi: (bi, g, i, 0)),
            pl.BlockSpec((1, nb, KV_WIDTH), lambda g, i, bi: (bi, 0, 0)),
            pl.BlockSpec((1, HEAD_DIM, nb), lambda g, i, bi: (bi, g, 0)),
            pl.BlockSpec((HEADS_PER_GROUP, CMP_BAND, tq), lambda g, i, bi: (g, 0, 0)),
            pl.BlockSpec(wagg.shape, lambda g, i, bi: (0, 0)),
        ],
        out_specs=(
            pl.BlockSpec((1, tq, HEADS_PER_GROUP * HEAD_DIM), lambda g, i, bi: (bi, i, g)),
            pl.BlockSpec((1, HEADS_PER_GROUP, tq, 2 * LANES), lambda g, i, bi: (bi, g, i, 0)),
        ),
        out_shape=(jax.ShapeDtypeStruct((b, s, NSA_WIDTH), F32),
                   jax.ShapeDtypeStruct((b, NSA_HEADS, s, 2 * LANES), BF16)),
        scratch_shapes=[pltpu.VMEM((CMP_PAD + nb, tq), F32), pltpu.VMEM((LANES, tq), F32)],
        compiler_params=_params("parallel", "parallel", "parallel"), name="cmp_attention",
    )(gates, qa, kc, vct, band, wagg)


def _cmp_tables(rel_bias, nb):
    jj = np.arange(CMP_BAND)[:, None]
    ii = np.arange(T_CMP)[None, :]
    dist = CMP_BAND_DIST0 + ii - CMP_STRIDE * jj
    bidx = _BUCKETS[np.clip(dist, 0, None)]
    band = jnp.transpose(rel_bias[bidx] - rel_bias[REL_BUCKETS - 1], (2, 0, 1))
    band = jnp.where(jnp.asarray(dist >= 0)[None], band, MASK_BIAS)
    ns = nb // SEL_RATIO
    assert SEL_TOPK <= ns <= LANES
    wagg = np.zeros((LANES, nb), np.float32)
    for m, w in enumerate(SEL_AGG):
        for jb in range(ns):
            c = SEL_RATIO * jb + m - 1
            if 0 <= c < nb - 1:
                wagg[jb, c] = w
    return band, jnp.asarray(wagg, BF16)


N_NEAR = (FAR_DIST + T_ATT - 1) // T_ATT + 1
assert N_NEAR * T_ATT - (T_ATT - 1) >= FAR_DIST
WIN_TILES = (WINDOW - 1 + T_ATT - 1) // T_ATT
TOEPLITZ_ROWS = 8
assert max(N_NEAR, WIN_TILES + 1) <= TOEPLITZ_ROWS
FAR_KEYS = 4 * T_ATT


def _flash_kernel(gate_ref, q_ref, kt_ref, v_ref, gv_ref, o_ref, m_ref, acc_ref, nb_ref, *,
                  windowed, branch):
    t = T_ATT
    g = pl.program_id(1)
    i = pl.program_id(2)
    n_tab = WIN_TILES + 1 if windowed else N_NEAR

    @pl.when(i == 0)
    def _():
        for h in range(HEADS_PER_GROUP):
            for d in range(n_tab):
                gen = jnp.broadcast_to(gv_ref[0, h, d:d + 1, :], (t, 2 * t))
                nb_ref[h, d] = pltpu.roll(gen, 0, 1, stride=1, stride_axis=0)[:, :t]

    m_ref[...] = jnp.full(m_ref.shape, NEG_INIT, F32)
    acc_ref[...] = jnp.zeros(acc_ref.shape, F32)

    def tile_step(first_tile, n_tiles, dists=(), causal=False):
        heads = range(HEADS_PER_GROUP)
        assert not dists or len(dists) == n_tiles
        nk = t * n_tiles
        start = pl.multiple_of(first_tile * t, t)
        kt = kt_ref[0, :, pl.ds(start, nk)]
        v = v_ref[0, 0, pl.ds(start, nk), :]
        s = [_dot(q_ref[0, h], kt) for h in heads]
        if dists:
            s = [s[h] + jnp.concatenate([nb_ref[h, d] for d in dists], axis=1) for h in heads]
            dist = (dists[0] * t + lax.broadcasted_iota(jnp.int32, (t, nk), 0)
                    - lax.broadcasted_iota(jnp.int32, (t, nk), 1))
            keep = None
            if causal:
                keep = dist >= 0
            elif windowed and dists[0] * t + t - 1 >= WINDOW:
                keep = dist < WINDOW
            if keep is not None:
                s = [jnp.where(keep, s[h], MASK_BIAS) for h in heads]
        m_prev = [m_ref[h] for h in heads]
        m_new = [jnp.maximum(m_prev[h], jnp.max(s[h], axis=1, keepdims=True)) for h in heads]
        p = [jnp.exp(s[h] - jnp.tile(m_new[h], (1, nk // LANES))).astype(BF16) for h in heads]
        pv = [_dot(p[h], v) for h in heads]
        for h in heads:
            acc_ref[h] = jnp.exp(m_prev[h] - m_new[h]) * acc_ref[h] + pv[h]
            m_ref[h] = m_new[h]

    def near_steps(first):
        assert first % 2 == 0

        @pl.when(i >= first)
        def _():
            for d in range(first, 0, -2):
                tile_step(i - d, 2, (d, d - 1))

        @pl.when(i < first)
        def _():
            for d in range(first - 1, 0, -1):
                @pl.when(i >= d)
                def _():
                    tile_step(i - d, 1, (d,))

    if windowed:
        near_steps(WIN_TILES)
    else:
        n_far = jnp.maximum(i - (N_NEAR - 1), 0)
        far_tiles = FAR_KEYS // t

        def far_body(j, carry):
            tile_step(j * far_tiles, far_tiles)
            return carry
        n_far_steps = jnp.right_shift(n_far, _log2(far_tiles))
        lax.fori_loop(0, n_far_steps, far_body, 0)
        done = n_far_steps * far_tiles
        run = far_tiles // 2
        while run >= 1:
            has_run = jnp.bitwise_and(n_far, run) != 0

            @pl.when(has_run)
            def _():
                tile_step(done, run)
            done = done + jnp.where(has_run, run, 0)
            run //= 2
        near_steps(N_NEAR - 1)
    tile_step(i, 1, (0,), causal=True)

    lane = lax.broadcasted_iota(jnp.int32, (t, LANES), 1)
    gates = gate_ref[0]
    outs = []
    for h in range(HEADS_PER_GROUP):
        col = _gate_column(g * HEADS_PER_GROUP + h, branch)
        gate = jnp.sum(jnp.where(lane == col, gates, 0.0), axis=1, keepdims=True)
        acc = acc_ref[h]
        outs.append(acc[:, :HEAD_DIM] / jnp.maximum(acc[:, HEAD_DIM:HEAD_DIM + 1], 1.0) * gate)
    o_ref[0] = jnp.concatenate(outs, axis=1)


def _flash(gates, q_aug, kt, v, gv, *, windowed, branch):
    b, _, s, kd = q_aug.shape
    assert kt.shape[1] == kd
    t = T_ATT
    grid = (b, KV_GROUPS, s // t)
    return pl.pallas_call(
        functools.partial(_flash_kernel, windowed=windowed, branch=branch), grid=grid,
        in_specs=[
            pl.BlockSpec((1, t, LANES), lambda bi, g, i: (bi, i, 0)),
            pl.BlockSpec((1, HEADS_PER_GROUP, t, kd), lambda bi, g, i: (bi, g, i, 0)),
            pl.BlockSpec((1, kd, s), lambda bi, g, i: (bi, 0, 0)),
            pl.BlockSpec((1, 1, s, LANES), lambda bi, g, i: (bi, g, 0, 0)),
            pl.BlockSpec((1, HEADS_PER_GROUP, TOEPLITZ_ROWS, 2 * t), lambda bi, g, i: (g, 0, 0, 0)),
        ],
        out_specs=pl.BlockSpec((1, t, HEADS_PER_GROUP * HEAD_DIM), lambda bi, g, i: (bi, i, g)),
        out_shape=jax.ShapeDtypeStruct((b, s, NSA_WIDTH), F32),
        scratch_shapes=[pltpu.VMEM((HEADS_PER_GROUP, t, LANES), F32),
                        pltpu.VMEM((HEADS_PER_GROUP, t, LANES), F32),
                        pltpu.VMEM((HEADS_PER_GROUP, TOEPLITZ_ROWS, t, t), F32)],
        compiler_params=_params("parallel", "parallel", "arbitrary"),
        name="win_attention" if windowed else "sel_attention",
    )(gates, q_aug, kt, v, gv)


def _toeplitz_generators(rel_bias):
    t = T_ATT
    d = np.arange(TOEPLITZ_ROWS)[:, None]
    y = np.arange(2 * t)[None, :]
    dist = np.where(y < t, d * t - y, d * t + 2 * t - y)
    tab = rel_bias[_BUCKETS[np.clip(dist, 0, None)]] - rel_bias[REL_BUCKETS - 1]
    tab = jnp.where(jnp.asarray(dist >= 0)[..., None], tab, 0.0)
    return jnp.transpose(tab, (2, 0, 1)).reshape(KV_GROUPS, HEADS_PER_GROUP, TOEPLITZ_ROWS, 2 * t)


def _pool_kernel(x_ref, halo_ref, w_ref, scale_ref, o_ref):
    t = x_ref.shape[1]
    i = pl.program_id(1)
    x = x_ref[0]
    halo = jnp.where(i > 0, halo_ref[0], 0.0)
    xs = jnp.concatenate([halo, x], axis=0)
    sums = []
    acc = xs
    span = 1
    for win in POOL_WINDOWS:
        while span < win:
            acc = acc + pltpu.roll(acc, span, 0)
            span *= 2
        sums.append(acc[HALO:])
    lane_grp = jnp.right_shift(lax.broadcasted_iota(jnp.int32, (t, POOL_WIDTH), 1), _log2(POOL_CH))
    tok = i * t + lax.broadcasted_iota(jnp.int32, (t, POOL_WIDTH), 0)
    ssum = sums[-1]
    win = jnp.full((t, POOL_WIDTH), POOL_WINDOWS[-1], jnp.int32)
    for gi in range(len(POOL_WINDOWS) - 2, -1, -1):
        ssum = jnp.where(lane_grp == gi, sums[gi], ssum)
        win = jnp.where(lane_grp == gi, POOL_WINDOWS[gi], win)
    cnt = jnp.minimum(tok + 1, win).astype(F32)
    dlt = (ssum / cnt - x).astype(BF16)
    o_ref[0] = (_dot(dlt, w_ref[...]) * scale_ref[...]).astype(BF16)


def _pool_mixer(p, wblk, scale):
    b, s, c = p.shape
    t = T_MIX
    return pl.pallas_call(
        _pool_kernel, grid=(b, s // t),
        in_specs=[pl.BlockSpec((1, t, c), lambda bi, i: (bi, i, 0)),
                  pl.BlockSpec((1, HALO, c), lambda bi, i: (bi, jnp.maximum(i * (t // HALO) - 1, 0), 0)),
                  pl.BlockSpec((c, c), lambda bi, i: (0, 0)),
                  pl.BlockSpec((1, c), lambda bi, i: (0, 0))],
        out_specs=pl.BlockSpec((1, t, c), lambda bi, i: (bi, i, 0)),
        out_shape=jax.ShapeDtypeStruct((b, s, c), BF16),
        compiler_params=_params("parallel", "parallel"), name="pool_mixer",
    )(p, p, wblk, scale)


def _sgu_kernel(u_ref, v_ref, ng_ref, nb_ref, ws_ref, bs_ref, o_ref):
    t = u_ref.shape[1]
    v = _gelu(v_ref[0])
    mu = jnp.mean(v, axis=-1, keepdims=True)
    var = jnp.mean(jnp.square(v - mu), axis=-1, keepdims=True)
    vn = (v - mu) * lax.rsqrt(var + EPS) * ng_ref[...] + nb_ref[...]
    lane_grp = jnp.right_shift(lax.broadcasted_iota(jnp.int32, (SGU_CHUNK, SGU_WIDTH), 1),
                               _log2(SGU_CH))
    r = lax.broadcasted_iota(jnp.int32, (SGU_CHUNK, SGU_GROUPS * SGU_CHUNK), 0)
    c = lax.broadcasted_iota(jnp.int32, (SGU_CHUNK, SGU_GROUPS * SGU_CHUNK), 1) & (SGU_CHUNK - 1)
    ws = jnp.where(c <= r, ws_ref[...], 0.0).astype(BF16)
    for n in range(t // SGU_CHUNK):
        vc = vn[n * SGU_CHUNK:(n + 1) * SGU_CHUNK]
        stacked = jnp.concatenate(
            [jnp.where(lane_grp == gi, vc, 0.0) for gi in range(SGU_GROUPS)], axis=0).astype(BF16)
        mixed = _dot(ws, stacked) + bs_ref[...]
        u = _gelu(u_ref[0, n * SGU_CHUNK:(n + 1) * SGU_CHUNK, :])
        o_ref[0, n * SGU_CHUNK:(n + 1) * SGU_CHUNK, :] = (u * mixed).astype(BF16)


def _sgu_mixer(u, v, ng, nbias, ws_cat, bs_exp):
    b, s, c = u.shape
    t = T_MIX
    row = pl.BlockSpec((1, t, c), lambda bi, i: (bi, i, 0))
    full = lambda a: pl.BlockSpec(a.shape, lambda bi, i: (0,) * a.ndim)
    return pl.pallas_call(
        _sgu_kernel, grid=(b, s // t),
        in_specs=[row, row, full(ng), full(nbias), full(ws_cat), full(bs_exp)],
        out_specs=row, out_shape=jax.ShapeDtypeStruct((b, s, c), BF16),
        compiler_params=_params("parallel", "parallel"), name="sgu_mixer",
    )(u, v, ng, nbias, ws_cat, bs_exp)


def _out_proj_kernel(oc_ref, os_ref, ow_ref, pool_ref, sgu_ref, w_ref, g_ref, x_ref, y_ref):
    a = (oc_ref[0] + os_ref[0] + ow_ref[0]).astype(BF16)
    mix = (_dot(a, w_ref[:NSA_WIDTH, :])
           + _dot(pool_ref[0], w_ref[NSA_WIDTH:NSA_WIDTH + POOL_WIDTH, :])
           + _dot(sgu_ref[0], w_ref[NSA_WIDTH + POOL_WIDTH:, :]))
    y_ref[0] = x_ref[0] + _rms(mix, g_ref[...])


def _out_proj(o_cmp, o_sel, o_win, pool, sgu, w, g, x):
    b, s, d = x.shape
    t = T_OUT
    row = lambda n: pl.BlockSpec((1, t, n), lambda bi, i: (bi, i, 0))
    full = lambda a: pl.BlockSpec(a.shape, lambda bi, i: (0,) * a.ndim)
    return pl.pallas_call(
        _out_proj_kernel, grid=(b, s // t),
        in_specs=[row(NSA_WIDTH), row(NSA_WIDTH), row(NSA_WIDTH),
                  row(POOL_WIDTH), row(SGU_WIDTH), full(w), full(g), row(d)],
        out_specs=row(d), out_shape=jax.ShapeDtypeStruct((b, s, d), F32),
        compiler_params=_params("parallel", "parallel"), name="out_proj",
    )(o_cmp, o_sel, o_win, pool, sgu, w, g, x)


def _mem_kv_kernel(mem_ref, g_ref, wk_ref, wv_ref, k_ref, v_ref):
    m = _rms(mem_ref[0], g_ref[...]).astype(BF16)
    k_ref[0] = _dot(m, wk_ref[...]).astype(BF16)
    v_ref[0] = _dot(m, wv_ref[...]).astype(BF16)


def _mem_kv(mem, g, wk, wv):
    b, m, d = mem.shape
    full = lambda a: pl.BlockSpec(a.shape, lambda bi: (0,) * a.ndim)
    blk = pl.BlockSpec((1, m, d), lambda bi: (bi, 0, 0))
    return pl.pallas_call(
        _mem_kv_kernel, grid=(b,),
        in_specs=[blk, full(g), full(wk), full(wv)],
        out_specs=(blk, blk),
        out_shape=(jax.ShapeDtypeStruct((b, m, d), BF16), jax.ShapeDtypeStruct((b, m, d), BF16)),
        compiler_params=_params("parallel"), name="mem_kv",
    )(mem, g, wk, wv)


def _mem_attn_kernel(x_ref, gpre_ref, wq_ref, k_ref, v_ref, wo_ref, gpost_ref, y_ref):
    x = x_ref[0]
    h = _rms(x, gpre_ref[...]).astype(BF16)
    q = (_dot(h, wq_ref[...]) * (MEM_HEAD_DIM ** -0.5)).astype(BF16)
    outs = []
    for hd in range(MEM_HEADS):
        sl = slice(hd * MEM_HEAD_DIM, (hd + 1) * MEM_HEAD_DIM)
        s = _dot_nt(q[:, sl], k_ref[0, :, sl])
        e = jnp.exp(s - jnp.max(s, axis=-1, keepdims=True))
        p = e / jnp.sum(e, axis=-1, keepdims=True)
        outs.append(_dot(p.astype(BF16), v_ref[0, :, sl]).astype(BF16))
    o = jnp.concatenate(outs, axis=1)
    y_ref[0] = x + _rms(_dot(o, wo_ref[...]), gpost_ref[...])


def _mem_attn(x, gpre, wq, k, v, wo, gpost):
    b, s, d = x.shape
    m = k.shape[1]
    t = T_MEM
    row = pl.BlockSpec((1, t, d), lambda bi, i: (bi, i, 0))
    full = lambda a: pl.BlockSpec(a.shape, lambda bi, i: (0,) * a.ndim)
    kv = pl.BlockSpec((1, m, d), lambda bi, i: (bi, 0, 0))
    return pl.pallas_call(
        _mem_attn_kernel, grid=(b, s // t),
        in_specs=[row, full(gpre), full(wq), kv, kv, full(wo), full(gpost)],
        out_specs=row, out_shape=jax.ShapeDtypeStruct((b, s, d), F32),
        compiler_params=_params("parallel", "parallel"), name="mem_attention",
    )(x, gpre, wq, k, v, wo, gpost)


def _ffn_kernel(x_ref, halo_ref, gpre_ref, wup_ref, cw_ref, cb_ref, wd_ref, gpost_ref, y_ref,
                act_ref):
    i = pl.program_id(1)
    f = wd_ref.shape[0]
    halo = jnp.where(i > 0, halo_ref[0], 0.0)
    h = _rms(jnp.concatenate([halo, x_ref[0]], axis=0), gpre_ref[...]).astype(BF16)

    def conv(off):
        a = _dot(h, wup_ref[:, off:off + F_CHUNK])
        cw = cw_ref[:, off:off + F_CHUNK]
        return (cw[0:1] * pltpu.roll(a, 2, 0) + cw[1:2] * pltpu.roll(a, 1, 0) + cw[2:3] * a
                + cb_ref[:, off:off + F_CHUNK])

    for off in range(0, f, F_CHUNK):
        act = _gelu(conv(off)) * conv(f + off)
        act_ref[:, off:off + F_CHUNK] = act[HALO:].astype(BF16)
    y_ref[0] = x_ref[0] + _rms(_dot(act_ref[...], wd_ref[...]), gpost_ref[...])


def _ffn(x, gpre, w_up, conv_w, conv_b, w_down, gpost):
    b, s, d = x.shape
    f = w_down.shape[0]
    t = T_FFN
    assert f % F_CHUNK == 0
    resident = lambda a: pl.BlockSpec(a.shape, lambda bi, i: (0,) * a.ndim,
                                      pipeline_mode=pl.Buffered(1))
    return pl.pallas_call(
        _ffn_kernel, grid=(b, s // t),
        in_specs=[
            pl.BlockSpec((1, t, d), lambda bi, i: (bi, i, 0)),
            pl.BlockSpec((1, HALO, d), lambda bi, i: (bi, jnp.maximum(i * (t // HALO) - 1, 0), 0)),
            resident(gpre), resident(w_up), resident(conv_w), resident(conv_b), resident(w_down),
            resident(gpost),
        ],
        out_specs=pl.BlockSpec((1, t, d), lambda bi, i: (bi, i, 0)),
        out_shape=jax.ShapeDtypeStruct((b, s, d), F32),
        scratch_shapes=[pltpu.VMEM((t, f), BF16)],
        compiler_params=_params("parallel", "parallel"), name="conv_ffn",
    )(x, x, gpre, w_up, conv_w, conv_b, w_down, gpost)


def kernel(x, mem, rel_bias, mix_norm_pre, mix_norm_post, w_in, cmp_pos, cmp_w1, cmp_b1, cmp_w2,
           cmp_b2, pool_w, pool_scale, sgu_norm_g, sgu_norm_b, sgu_w, sgu_b, w_out, mem_norm_pre,
           mem_norm_kv, mem_norm_post, w_mq, w_mk, w_mv, w_mo, ffn_norm_pre, ffn_norm_post, w_up,
           conv_w, conv_b, w_down):
    depth = w_in.shape[0]
    s = x.shape[1]
    assert s % max(T_IN, T_CMP, T_ATT, T_OUT, T_MIX, T_MEM, T_FFN) == 0
    nb = s // CMP_STRIDE

    w_in_p = _in_proj_weights(w_in)
    band, wagg = _cmp_tables(rel_bias, nb)
    toep = _toeplitz_generators(rel_bias)
    row = lambda a: a[:, None, :]
    bf = lambda a: a.astype(BF16)
    w_out_b, w_mq_b, w_mk_b, w_mv_b, w_mo_b = bf(w_out), bf(w_mq), bf(w_mk), bf(w_mv), bf(w_mo)
    w_up_b, w_down_b = bf(w_up), bf(w_down)

    for l in range(depth):
        qa, ksel_t, kwin_t, kcmp, vcmp, vsel, vwin, gates, pin, u, v = _in_proj(
            x, row(mix_norm_pre)[l], w_in_p[l])
        cw = _compress_weights(cmp_pos[l], cmp_w1[l], cmp_b1[l], cmp_w2[l], cmp_b2[l])
        kc, vct = _compress(kcmp, vcmp, cw)
        o_cmp, q_aug = _cmp_attention(gates, qa, kc, vct, band, wagg)
        o_sel = _flash(gates, q_aug, ksel_t, vsel, toep, windowed=False, branch=1)
        o_win = _flash(gates, qa, kwin_t, vwin, toep, windowed=True, branch=2)
        wblk = bf(jax.scipy.linalg.block_diag(*[pool_w[l, gi] for gi in range(len(POOL_WINDOWS))]))
        out_b = _pool_mixer(pin, wblk, row(pool_scale)[l])
        ws_cat = jnp.concatenate([sgu_w[l, gi] for gi in range(SGU_GROUPS)], axis=1)
        bs_exp = jnp.repeat(jnp.transpose(sgu_b[l]), SGU_CH, axis=1)
        out_c = _sgu_mixer(u, v, row(sgu_norm_g)[l], row(sgu_norm_b)[l], ws_cat, bs_exp)
        x = _out_proj(o_cmp, o_sel, o_win, out_b, out_c, w_out_b[l], row(mix_norm_post)[l], x)
        km, vm = _mem_kv(mem, row(mem_norm_kv)[l], w_mk_b[l], w_mv_b[l])
        x = _mem_attn(x, row(mem_norm_pre)[l], w_mq_b[l], km, vm, w_mo_b[l], row(mem_norm_post)[l])
        x = _ffn(x, row(ffn_norm_pre)[l], w_up_b[l], conv_w[l], row(conv_b)[l], w_down_b[l],
                 row(ffn_norm_post)[l])
    return x
```

```python
import functools
import math

import numpy as np
import jax
import jax.numpy as jnp
from jax import lax
from jax.experimental import pallas as pl
from jax.experimental.pallas import tpu as pltpu

F32 = jnp.float32
BF16 = jnp.bfloat16

D_MODEL = 1024
NSA_WIDTH = 512
POOL_WIDTH = 256
SGU_WIDTH = 256
NSA_HEADS = 8
KV_GROUPS = 2
HEADS_PER_GROUP = 4
HEAD_DIM = 64
KV_WIDTH = 128
GATE_WIDTH = 24
CMP_STRIDE = 16
CMP_LEN = 32
CMP_HIDDEN = 128
SEL_BLOCK = 64
SEL_TOPK = 16
N_LOCAL_SEL = 2
SEL_AGG = (1.0, 2.0, 2.0, 2.0, 1.0)
FORCE_SCORE = 1e9
WINDOW = 512
REL_BUCKETS = 32
REL_MAX_DIST = 1024
POOL_WINDOWS = (2, 4, 8, 16)
POOL_CH = 64
SGU_GROUPS = 4
SGU_CH = 64
SGU_CHUNK = 128
MEM_HEADS = 4
MEM_HEAD_DIM = 256
FFN_HIDDEN = 2816
EPS = 1e-6

LANES = 128
SUBLANES = 8
BF16_ROWS = 16
VMEM_LIMIT = 56 * 1024 * 1024

T_IN = 512
T_CMP = 256
T_ATT = 512
T_OUT = 512
T_MIX = 512
T_MEM = 512
T_FFN = 512
F_CHUNK = 256
HALO = 16

MASK_BIAS = -float(2 ** 30)
NEG_INIT = -1e30

_Q_OFF = 0
_KSEL_OFF = 1024
_KWIN_OFF = 1152
_KCMP_OFF = 1280
_VCMP_OFF = 1408
_VSEL_OFF = 1536
_VWIN_OFF = 1792
_GATE_OFF = 2048
_POOL_OFF = 2304
_U_OFF = 2560
_V_OFF = 2816
IN_COLS = 3072
MXU_COLS = 256


def _bucket_table(n_max):
    n = np.arange(n_max)
    max_exact = REL_BUCKETS // 2
    nf = np.maximum(n, 1).astype(np.float64)
    large = max_exact + (np.log(nf / max_exact) / math.log(REL_MAX_DIST / max_exact)
                         * (REL_BUCKETS - max_exact)).astype(np.int32)
    large = np.minimum(large, REL_BUCKETS - 1)
    return np.where(n < max_exact, n, large).astype(np.int32)


_BUCKETS = _bucket_table(16384)
FAR_DIST = int(np.argmax(_BUCKETS == REL_BUCKETS - 1))


def _bias_lookup(rel_bias, buckets):
    onehot = jnp.asarray(np.eye(REL_BUCKETS, dtype=np.float32)[buckets])
    return jnp.einsum("...k,kh->...h", onehot, rel_bias, precision=lax.Precision.HIGHEST)


def _in_proj_layout():
    moves = []
    for h in range(NSA_HEADS):
        g = h // HEADS_PER_GROUP
        moves.append((_Q_OFF + h * LANES + g * HEAD_DIM, h * HEAD_DIM, HEAD_DIM))
    base = NSA_WIDTH
    moves.append((_KCMP_OFF, base, KV_WIDTH))
    moves.append((_VCMP_OFF, base + KV_WIDTH, KV_WIDTH))
    moves.append((_KSEL_OFF, base + 2 * KV_WIDTH, KV_WIDTH))
    for g in range(KV_GROUPS):
        moves.append((_VSEL_OFF + g * LANES, base + 3 * KV_WIDTH + g * HEAD_DIM, HEAD_DIM))
    moves.append((_KWIN_OFF, base + 4 * KV_WIDTH, KV_WIDTH))
    for g in range(KV_GROUPS):
        moves.append((_VWIN_OFF + g * LANES, base + 5 * KV_WIDTH + g * HEAD_DIM, HEAD_DIM))
    base += 6 * KV_WIDTH
    moves.append((_GATE_OFF, base, GATE_WIDTH))
    base += GATE_WIDTH
    moves.append((_POOL_OFF, base, POOL_WIDTH))
    moves.append((_U_OFF, base + POOL_WIDTH, SGU_WIDTH))
    moves.append((_V_OFF, base + POOL_WIDTH + SGU_WIDTH, SGU_WIDTH))
    return sorted(moves)


def _in_proj_weights(w_in):
    depth, d, _ = w_in.shape
    pieces, pos = [], 0
    for dst, src, n in _in_proj_layout():
        if dst > pos:
            pieces.append(jnp.zeros((depth, d, dst - pos), BF16))
        pieces.append(w_in[:, :, src:src + n].astype(BF16))
        pos = dst + n
    pieces.append(jnp.zeros((depth, d, IN_COLS - pos), BF16))
    return jnp.concatenate(pieces, axis=2)


def _log2(n):
    assert n & (n - 1) == 0
    return n.bit_length() - 1


def _params(*sem):
    return pltpu.CompilerParams(dimension_semantics=sem, vmem_limit_bytes=VMEM_LIMIT)


def _layer_spec(a, l, **kw):
    return pl.BlockSpec((None,) + a.shape[1:], lambda *_: (l,) + (0,) * (a.ndim - 1), **kw)


def _rms(x, g):
    return x * lax.rsqrt(jnp.mean(x * x, axis=-1, keepdims=True) + EPS) * g


def _gelu(x):
    c = math.sqrt(2.0 / math.pi)
    return x * (0.5 * (1.0 + jnp.tanh(c * (x + 0.044715 * (x * x * x)))))


def _dot(a, b):
    return jnp.dot(a, b, preferred_element_type=F32)


def _dot_nt(a, b):
    return lax.dot_general(a, b, (((1,), (1,)), ((), ())), preferred_element_type=F32)


def _split3(x):
    hi = x.astype(BF16)
    r1 = x - hi.astype(F32)
    mid = r1.astype(BF16)
    lo = (r1 - mid.astype(F32)).astype(BF16)
    return hi, mid, lo


def _dot_f32_left(w_bf16, x):
    hi, mid, lo = _split3(x)
    return _dot(w_bf16, hi) + _dot(w_bf16, mid) + _dot(w_bf16, lo)


def _in_proj_kernel(x_ref, g_ref, w_ref, qa_ref, ksel_ref, kwin_ref, kcmp_ref, vcmp_ref,
                    vsel_ref, vwin_ref, gate_ref, pool_ref, u_ref, v_ref):
    t = x_ref.shape[1]
    h = _rms(x_ref[0], g_ref[...]).astype(BF16)

    def proj(off):
        return _dot(h, w_ref[:, off:off + MXU_COLS])

    def slabs(off):
        p = proj(off)
        return p[:, :LANES], p[:, LANES:]

    scale = HEAD_DIM ** -0.5
    for hd in range(0, NSA_HEADS, 2):
        qa, qb = slabs(_Q_OFF + hd * LANES)
        qa_ref[0, hd] = (qa * scale).astype(BF16)
        qa_ref[0, hd + 1] = (qb * scale).astype(BF16)

    lane = lax.broadcasted_iota(jnp.int32, (t, LANES), 1)
    ksel, kwin = slabs(_KSEL_OFF)
    ksel_ref[0, :LANES, :] = ksel.T.astype(BF16)
    tok = pl.program_id(1) * t + lax.broadcasted_iota(jnp.int32, (LANES, t), 1)
    blk_id = lax.broadcasted_iota(jnp.int32, (LANES, t), 0)
    ksel_ref[0, LANES:, :] = jnp.where(
        jnp.right_shift(tok, _log2(SEL_BLOCK)) == blk_id, 1.0, 0.0).astype(BF16)
    kwin_ref[0] = kwin.T.astype(BF16)
    kcmp, vcmp = slabs(_KCMP_OFF)
    kcmp_ref[0] = kcmp.astype(BF16)
    vcmp_ref[0] = vcmp.astype(BF16)
    for v_out, off in ((vsel_ref, _VSEL_OFF), (vwin_ref, _VWIN_OFF)):
        for g, vg in enumerate(slabs(off)):
            v_out[0, g] = jnp.where(lane == HEAD_DIM, 1.0, vg).astype(BF16)
    gate_ref[0] = jax.nn.sigmoid(slabs(_GATE_OFF)[0])
    pool_ref[0] = proj(_POOL_OFF)
    u_ref[0] = proj(_U_OFF)
    v_ref[0] = proj(_V_OFF)


def _in_proj(x, g, w, l):
    b, s, d = x.shape
    t = T_IN
    grid = (b, s // t)
    row3 = lambda n: pl.BlockSpec((1, t, n), lambda bi, i: (bi, i, 0))
    out_shape = (
        jax.ShapeDtypeStruct((b, NSA_HEADS, s, LANES), BF16),
        jax.ShapeDtypeStruct((b, 2 * LANES, s), BF16),
        jax.ShapeDtypeStruct((b, LANES, s), BF16),
        jax.ShapeDtypeStruct((b, s, LANES), BF16),
        jax.ShapeDtypeStruct((b, s, LANES), BF16),
        jax.ShapeDtypeStruct((b, KV_GROUPS, s, LANES), BF16),
        jax.ShapeDtypeStruct((b, KV_GROUPS, s, LANES), BF16),
        jax.ShapeDtypeStruct((b, s, LANES), F32),
        jax.ShapeDtypeStruct((b, s, POOL_WIDTH), F32),
        jax.ShapeDtypeStruct((b, s, SGU_WIDTH), F32),
        jax.ShapeDtypeStruct((b, s, SGU_WIDTH), F32),
    )
    out_specs = (
        pl.BlockSpec((1, NSA_HEADS, t, LANES), lambda bi, i: (bi, 0, i, 0)),
        pl.BlockSpec((1, 2 * LANES, t), lambda bi, i: (bi, 0, i)),
        pl.BlockSpec((1, LANES, t), lambda bi, i: (bi, 0, i)),
        row3(LANES), row3(LANES),
        pl.BlockSpec((1, KV_GROUPS, t, LANES), lambda bi, i: (bi, 0, i, 0)),
        pl.BlockSpec((1, KV_GROUPS, t, LANES), lambda bi, i: (bi, 0, i, 0)),
        row3(LANES), row3(POOL_WIDTH), row3(SGU_WIDTH), row3(SGU_WIDTH),
    )
    return pl.pallas_call(
        _in_proj_kernel, grid=grid,
        in_specs=[row3(d), _layer_spec(g, l), _layer_spec(w, l)],
        out_specs=out_specs, out_shape=out_shape,
        compiler_params=_params("parallel", "parallel"), name="in_proj",
    )(x, g, w)


def _compress_kernel(rk_ref, rv_ref, w1a_ref, w1b_ref, pos_ref, b1_ref, w2k_ref, b2k_ref,
                     w2v_ref, b2v_ref, kc_ref, vct_ref):
    nb = rk_ref.shape[1]
    keep = lax.broadcasted_iota(jnp.int32, (nb, LANES), 0) < nb - 1

    def hidden(r, which):
        a = _dot(r, w1a_ref[which])
        bm = _dot(r, w1b_ref[which])
        pre = a + pltpu.roll(bm, nb - 1, 0)
        posb = (_dot(pos_ref[which, 0], w1a_ref[which])[0:1]
                + _dot(pos_ref[which, 1], w1b_ref[which])[0:1])
        return _gelu(pre + posb + b1_ref[which]).astype(BF16)

    kc = _dot(hidden(rk_ref[0], 0), w2k_ref[...]) + b2k_ref[...]
    kc_ref[0] = jnp.where(keep, kc, 0.0).astype(BF16)
    vc = _dot(hidden(rv_ref[0], 1), w2v_ref[...]) + b2v_ref[...]
    vct_ref[0] = jnp.where(keep, vc, 0.0).T.astype(BF16)


def _compress(kcmp, vcmp, cw, l):
    b, s, _ = kcmp.shape
    nb = s // CMP_STRIDE
    rk = kcmp.reshape(b, nb, CMP_STRIDE * KV_WIDTH)
    rv = vcmp.reshape(b, nb, CMP_STRIDE * KV_WIDTH)
    ws = (cw["w1a"], cw["w1b"], cw["pos"], cw["b1"], cw["w2k"], cw["b2k"], cw["w2v"], cw["b2v"])
    return pl.pallas_call(
        _compress_kernel, grid=(b,),
        in_specs=[pl.BlockSpec((1, nb, rk.shape[2]), lambda bi: (bi, 0, 0)),
                  pl.BlockSpec((1, nb, rv.shape[2]), lambda bi: (bi, 0, 0))]
                 + [_layer_spec(a, l) for a in ws],
        out_specs=(pl.BlockSpec((1, nb, KV_WIDTH), lambda bi: (bi, 0, 0)),
                   pl.BlockSpec((1, KV_WIDTH, nb), lambda bi: (bi, 0, 0))),
        out_shape=(jax.ShapeDtypeStruct((b, nb, KV_WIDTH), BF16),
                   jax.ShapeDtypeStruct((b, KV_WIDTH, nb), BF16)),
        compiler_params=_params("parallel"), name="compress",
    )(rk, rv, *ws)


def _compress_weights(pos, w1, b1, w2, b2):
    eye = jnp.eye(KV_GROUPS, dtype=F32)
    half = CMP_LEN // 2

    def w1_blocks(w):
        return jnp.einsum("ldh,pg->lpdgh", w, eye).reshape(half * KV_WIDTH, KV_GROUPS * CMP_HIDDEN)

    def pos_rows(p):
        r = jnp.broadcast_to(p[:, None, :], (half, KV_GROUPS, HEAD_DIM)).reshape(1, half * KV_WIDTH)
        return jnp.concatenate([r, jnp.zeros((SUBLANES - 1, half * KV_WIDTH), F32)], axis=0)

    w1a = jnp.stack([w1_blocks(w1[i, :half]) for i in range(2)]).astype(BF16)
    w1b = jnp.stack([w1_blocks(w1[i, half:]) for i in range(2)]).astype(BF16)
    posr = jnp.stack([jnp.stack([pos_rows(pos[i, :half]), pos_rows(pos[i, half:])])
                      for i in range(2)]).astype(BF16)
    b1t = jnp.tile(b1, (1, KV_GROUPS))[:, None, :]
    w2blk = lambda w: jnp.einsum("hd,pg->phgd", w, eye).reshape(KV_GROUPS * CMP_HIDDEN, KV_WIDTH)
    return dict(w1a=w1a, w1b=w1b, pos=posr, b1=b1t,
                w2k=w2blk(w2[0]).astype(BF16), b2k=jnp.tile(b2[0], KV_GROUPS)[None, :],
                w2v=w2blk(w2[1]).astype(BF16), b2v=jnp.tile(b2[1], KV_GROUPS)[None, :])


CMP_PAD = 64
CMP_BAND_BACK = 56
CMP_BAND = CMP_BAND_BACK + T_CMP // CMP_STRIDE
CMP_BAND_DIST0 = CMP_BAND_BACK * CMP_STRIDE - (CMP_LEN - 1)
assert CMP_BAND_DIST0 + CMP_STRIDE >= FAR_DIST and CMP_BAND_BACK <= CMP_PAD
CMP_ROW_CHUNK = 128
SEL_RATIO = SEL_BLOCK // CMP_STRIDE


def _gate_column(head, branch):
    return head * 3 + branch


def _cmp_kernel(gate_ref, qa_ref, kc_ref, vct_ref, band_ref, wagg_ref, ocmp_ref, qaug_ref,
                st_ref, gt_ref):
    tq = qa_ref.shape[2]
    nb = kc_ref.shape[1]
    i = pl.program_id(1)
    g = pl.program_id(0)
    s0 = i * tq
    t = s0 + lax.broadcasted_iota(jnp.int32, (1, tq), 1)
    n_live = (i + 1) * (tq // CMP_STRIDE)
    band_row0 = pl.multiple_of(s0 // CMP_STRIDE - CMP_BAND_BACK + CMP_PAD, SUBLANES)
    gt_ref[...] = gate_ref[0].T
    st_ref[:CMP_PAD, :] = jnp.zeros((CMP_PAD, tq), F32)
    has_block = t >= CMP_LEN - 1

    def body(rows):
        nsr = rows // SEL_RATIO
        kc = kc_ref[0, :rows, :]
        tail = rows - CMP_ROW_CHUNK
        dead = tail + lax.broadcasted_iota(jnp.int32, (CMP_ROW_CHUNK, 1), 0) >= n_live
        imp = jnp.zeros((rows, tq), F32)
        outs = []
        for h in range(HEADS_PER_GROUP):
            st_ref[CMP_PAD:CMP_PAD + rows, :] = _dot_nt(kc, qa_ref[0, h])
            st_ref[pl.ds(band_row0, CMP_BAND), :] += band_ref[h]
            last = jnp.where(dead, MASK_BIAS, st_ref[CMP_PAD + tail:CMP_PAD + rows, :])
            s = last if tail == 0 else jnp.concatenate([st_ref[CMP_PAD:CMP_PAD + tail, :], last], axis=0)
            e = jnp.exp(s - jnp.max(s, axis=0, keepdims=True))
            denom = jnp.maximum(jnp.sum(e, axis=0, keepdims=True), 1.0)
            rinv = jnp.where(has_block, 1.0 / denom, 0.0)
            imp = imp + e * rinv
            gate = gt_ref[pl.ds(_gate_column(g * HEADS_PER_GROUP + h, 0), 1), :]
            outs.append((gate * rinv) * _dot(vct_ref[0, :, :rows], e.astype(BF16)))
        ocmp_ref[0] = jnp.concatenate(outs, axis=0).T.astype(BF16)

        p_sel = _dot_f32_left(wagg_ref[:nsr, :rows], imp)
        j = lax.broadcasted_iota(jnp.int32, (nsr, 1), 0)
        rel = jnp.right_shift(t, _log2(SEL_BLOCK)) - j
        forced = (j == 0) | ((rel >= 0) & (rel < N_LOCAL_SEL))
        score = jnp.where(forced, FORCE_SCORE, jnp.where(j * SEL_BLOCK <= t, p_sel, -1.0))
        jfull = jnp.broadcast_to(j, (nsr, tq)).astype(F32)
        sel = jnp.zeros((nsr, tq), F32)
        for _ in range(SEL_TOPK):
            m = jnp.max(score, axis=0, keepdims=True)
            first = jnp.min(jnp.where(score == m, jfull, float(nsr)), axis=0, keepdims=True)
            hit = jfull == first
            sel = jnp.where(hit, 1.0, sel)
            score = jnp.where(hit, -3e38, score)
        notsel = jnp.where(sel > 0.5, 0.0, MASK_BIAS)
        if nsr < LANES:
            notsel = jnp.concatenate(
                [notsel, jnp.full((LANES - nsr, tq), MASK_BIAS, F32)], axis=0)
        notsel = notsel.T.astype(BF16)
        for h in range(HEADS_PER_GROUP):
            qaug_ref[0, h, :, :LANES] = qa_ref[0, h]
            qaug_ref[0, h, :, LANES:] = notsel

    for rows in range(CMP_ROW_CHUNK, nb + 1, CMP_ROW_CHUNK):
        @pl.when((n_live > rows - CMP_ROW_CHUNK) & (n_live <= rows))
        def _():
            body(rows)


def _cmp_attention(gates, qa, kc, vct, band, wagg):
    b, _, s, _ = qa.shape
    nb = kc.shape[1]
    tq = T_CMP
    assert nb % CMP_ROW_CHUNK == 0
    grid = (KV_GROUPS, s // tq, b)
    return pl.pallas_call(
        _cmp_kernel, grid=grid,
        in_specs=[
            pl.BlockSpec((1, tq, LANES), lambda g, i, bi: (bi, i, 0)),
            pl.BlockSpec((1, HEADS_PER_GROUP, tq, LANES), lambda g, i, bi: (bi, g, i, 0)),
            pl.BlockSpec((1, nb, KV_WIDTH), lambda g, i, bi: (bi, 0, 0)),
            pl.BlockSpec((1, HEAD_DIM, nb), lambda g, i, bi: (bi, g, 0)),
            pl.BlockSpec((HEADS_PER_GROUP, CMP_BAND, tq), lambda g, i, bi: (g, 0, 0)),
            pl.BlockSpec(wagg.shape, lambda g, i, bi: (0, 0)),
        ],
        out_specs=(
            pl.BlockSpec((1, tq, HEADS_PER_GROUP * HEAD_DIM), lambda g, i, bi: (bi, i, g)),
            pl.BlockSpec((1, HEADS_PER_GROUP, tq, 2 * LANES), lambda g, i, bi: (bi, g, i, 0)),
        ),
        out_shape=(jax.ShapeDtypeStruct((b, s, NSA_WIDTH), BF16),
                   jax.ShapeDtypeStruct((b, NSA_HEADS, s, 2 * LANES), BF16)),
        scratch_shapes=[pltpu.VMEM((CMP_PAD + nb, tq), F32), pltpu.VMEM((LANES, tq), F32)],
        compiler_params=_params("parallel", "parallel", "parallel"), name="cmp_attention",
    )(gates, qa, kc, vct, band, wagg)


def _cmp_tables(rel_bias, nb):
    jj = np.arange(CMP_BAND)[:, None]
    ii = np.arange(T_CMP)[None, :]
    dist = CMP_BAND_DIST0 + ii - CMP_STRIDE * jj
    bidx = _BUCKETS[np.clip(dist, 0, None)]
    band = jnp.transpose(_bias_lookup(rel_bias, bidx) - rel_bias[REL_BUCKETS - 1], (2, 0, 1))
    band = jnp.where(jnp.asarray(dist >= 0)[None], band, MASK_BIAS)
    ns = nb // SEL_RATIO
    assert SEL_TOPK <= ns <= LANES
    wagg = np.zeros((LANES, nb), np.float32)
    for m, w in enumerate(SEL_AGG):
        for jb in range(ns):
            c = SEL_RATIO * jb + m - 1
            if 0 <= c < nb - 1:
                wagg[jb, c] = w
    return band, jnp.asarray(wagg, BF16)


N_NEAR = (FAR_DIST + T_ATT - 1) // T_ATT + 1
assert N_NEAR * T_ATT - (T_ATT - 1) >= FAR_DIST
WIN_TILES = (WINDOW - 1 + T_ATT - 1) // T_ATT
TOEPLITZ_ROWS = 8
assert max(N_NEAR, WIN_TILES + 1) <= TOEPLITZ_ROWS
FAR_KEYS = 4 * T_ATT


def _flash_kernel(gate_ref, q_ref, kt_ref, v_ref, gv_ref, o_ref, m_ref, acc_ref, nb_ref, *,
                  windowed, branch):
    t = T_ATT
    g = pl.program_id(1)
    i = pl.program_id(2)
    n_tab = WIN_TILES + 1 if windowed else N_NEAR

    @pl.when(i == 0)
    def _():
        for h in range(HEADS_PER_GROUP):
            for d in range(n_tab):
                gen = jnp.broadcast_to(gv_ref[0, h, d:d + 1, :], (t, 2 * t))
                nb_ref[h, d] = pltpu.roll(gen, 0, 1, stride=1, stride_axis=0)[:, :t]

    m_ref[...] = jnp.full(m_ref.shape, NEG_INIT, F32)
    acc_ref[...] = jnp.zeros(acc_ref.shape, F32)

    def tile_step(first_tile, n_tiles, dists=(), causal=False):
        heads = range(HEADS_PER_GROUP)
        assert not dists or len(dists) == n_tiles
        nk = t * n_tiles
        start = pl.multiple_of(first_tile * t, t)
        kt = kt_ref[0, :, pl.ds(start, nk)]
        v = v_ref[0, 0, pl.ds(start, nk), :]
        s = [_dot(q_ref[0, h], kt) for h in heads]
        if dists:
            s = [s[h] + jnp.concatenate([nb_ref[h, d] for d in dists], axis=1) for h in heads]
            dist = (dists[0] * t + lax.broadcasted_iota(jnp.int32, (t, nk), 0)
                    - lax.broadcasted_iota(jnp.int32, (t, nk), 1))
            keep = None
            if causal:
                keep = dist >= 0
            elif windowed and dists[0] * t + t - 1 >= WINDOW:
                keep = dist < WINDOW
            if keep is not None:
                s = [jnp.where(keep, s[h], MASK_BIAS) for h in heads]
        m_prev = [m_ref[h] for h in heads]
        m_new = [jnp.maximum(m_prev[h], jnp.max(s[h], axis=1, keepdims=True)) for h in heads]
        p = [jnp.exp(s[h] - jnp.tile(m_new[h], (1, nk // LANES))).astype(BF16) for h in heads]
        pv = [_dot(p[h], v) for h in heads]
        for h in heads:
            acc_ref[h] = jnp.exp(m_prev[h] - m_new[h]) * acc_ref[h] + pv[h]
            m_ref[h] = m_new[h]

    def near_steps(first):
        @pl.when(i >= first)
        def _():
            d = first
            if d % 2:
                tile_step(i - d, 1, (d,))
                d -= 1
            for d in range(d, 0, -2):
                tile_step(i - d, 2, (d, d - 1))

        @pl.when(i < first)
        def _():
            for d in range(first - 1, 0, -1):
                @pl.when(i >= d)
                def _():
                    tile_step(i - d, 1, (d,))

    if windowed:
        near_steps(WIN_TILES)
    else:
        n_far = jnp.maximum(i - (N_NEAR - 1), 0)
        far_tiles = FAR_KEYS // t

        def far_body(j, carry):
            tile_step(j * far_tiles, far_tiles)
            return carry
        n_far_steps = jnp.right_shift(n_far, _log2(far_tiles))
        lax.fori_loop(0, n_far_steps, far_body, 0)
        done = n_far_steps * far_tiles
        run = far_tiles // 2
        while run >= 1:
            has_run = jnp.bitwise_and(n_far, run) != 0

            @pl.when(has_run)
            def _():
                tile_step(done, run)
            done = done + jnp.where(has_run, run, 0)
            run //= 2
        near_steps(N_NEAR - 1)
    tile_step(i, 1, (0,), causal=True)

    lane = lax.broadcasted_iota(jnp.int32, (t, LANES), 1)
    gates = gate_ref[0]
    outs = []
    for h in range(HEADS_PER_GROUP):
        col = _gate_column(g * HEADS_PER_GROUP + h, branch)
        gate = jnp.sum(jnp.where(lane == col, gates, 0.0), axis=1, keepdims=True)
        acc = acc_ref[h]
        outs.append(acc[:, :HEAD_DIM] / jnp.maximum(acc[:, HEAD_DIM:HEAD_DIM + 1], 1.0) * gate)
    o_ref[0] = jnp.concatenate(outs, axis=1).astype(BF16)


def _flash(gates, q_aug, kt, v, gv, *, windowed, branch):
    b, _, s, kd = q_aug.shape
    assert kt.shape[1] == kd
    t = T_ATT
    grid = (b, KV_GROUPS, s // t)
    return pl.pallas_call(
        functools.partial(_flash_kernel, windowed=windowed, branch=branch), grid=grid,
        in_specs=[
            pl.BlockSpec((1, t, LANES), lambda bi, g, i: (bi, i, 0)),
            pl.BlockSpec((1, HEADS_PER_GROUP, t, kd), lambda bi, g, i: (bi, g, i, 0)),
            pl.BlockSpec((1, kd, s), lambda bi, g, i: (bi, 0, 0)),
            pl.BlockSpec((1, 1, s, LANES), lambda bi, g, i: (bi, g, 0, 0)),
            pl.BlockSpec((1, HEADS_PER_GROUP, TOEPLITZ_ROWS, 2 * t), lambda bi, g, i: (g, 0, 0, 0)),
        ],
        out_specs=pl.BlockSpec((1, t, HEADS_PER_GROUP * HEAD_DIM), lambda bi, g, i: (bi, i, g)),
        out_shape=jax.ShapeDtypeStruct((b, s, NSA_WIDTH), BF16),
        scratch_shapes=[pltpu.VMEM((HEADS_PER_GROUP, t, LANES), F32),
                        pltpu.VMEM((HEADS_PER_GROUP, t, LANES), F32),
                        pltpu.VMEM((HEADS_PER_GROUP, WIN_TILES + 1 if windowed else N_NEAR, t, t),
                                   F32)],
        compiler_params=_params("parallel", "parallel", "arbitrary"),
        name="win_attention" if windowed else "sel_attention",
    )(gates, q_aug, kt, v, gv)


def _toeplitz_generators(rel_bias):
    t = T_ATT
    d = np.arange(TOEPLITZ_ROWS)[:, None]
    y = np.arange(2 * t)[None, :]
    dist = np.where(y < t, d * t - y, d * t + 2 * t - y)
    tab = _bias_lookup(rel_bias, _BUCKETS[np.clip(dist, 0, None)]) - rel_bias[REL_BUCKETS - 1]
    tab = jnp.where(jnp.asarray(dist >= 0)[..., None], tab, 0.0)
    return jnp.transpose(tab, (2, 0, 1)).reshape(KV_GROUPS, HEADS_PER_GROUP, TOEPLITZ_ROWS, 2 * t)


def _pool_kernel(x_ref, halo_ref, w_ref, scale_ref, o_ref):
    t = x_ref.shape[1]
    i = pl.program_id(1)
    x = x_ref[0]
    halo = jnp.where(i > 0, halo_ref[0], 0.0)
    xs = jnp.concatenate([halo, x], axis=0)
    sums = []
    acc = xs
    span = 1
    for win in POOL_WINDOWS:
        while span < win:
            acc = acc + pltpu.roll(acc, span, 0)
            span *= 2
        sums.append(acc[HALO:])
    lane_grp = jnp.right_shift(lax.broadcasted_iota(jnp.int32, (t, POOL_WIDTH), 1), _log2(POOL_CH))
    tok = i * t + lax.broadcasted_iota(jnp.int32, (t, POOL_WIDTH), 0)
    ssum = sums[-1]
    win = jnp.full((t, POOL_WIDTH), POOL_WINDOWS[-1], jnp.int32)
    for gi in range(len(POOL_WINDOWS) - 2, -1, -1):
        ssum = jnp.where(lane_grp == gi, sums[gi], ssum)
        win = jnp.where(lane_grp == gi, POOL_WINDOWS[gi], win)
    cnt = jnp.minimum(tok + 1, win).astype(F32)
    dlt = (ssum / cnt - x).astype(BF16)
    o_ref[0] = (_dot(dlt, w_ref[...]) * scale_ref[...]).astype(BF16)


def _pool_mixer(p, wblk, scale, l):
    b, s, c = p.shape
    t = T_MIX
    return pl.pallas_call(
        _pool_kernel, grid=(b, s // t),
        in_specs=[pl.BlockSpec((1, t, c), lambda bi, i: (bi, i, 0)),
                  pl.BlockSpec((1, HALO, c), lambda bi, i: (bi, jnp.maximum(i * (t // HALO) - 1, 0), 0)),
                  _layer_spec(wblk, l), _layer_spec(scale, l)],
        out_specs=pl.BlockSpec((1, t, c), lambda bi, i: (bi, i, 0)),
        out_shape=jax.ShapeDtypeStruct((b, s, c), BF16),
        compiler_params=_params("parallel", "parallel"), name="pool_mixer",
    )(p, p, wblk, scale)


def _sgu_kernel(u_ref, v_ref, ng_ref, nb_ref, ws_ref, bs_ref, o_ref):
    t = u_ref.shape[1]
    v = _gelu(v_ref[0])
    mu = jnp.mean(v, axis=-1, keepdims=True)
    var = jnp.mean(jnp.square(v - mu), axis=-1, keepdims=True)
    vn = (v - mu) * lax.rsqrt(var + EPS) * ng_ref[...] + nb_ref[...]
    lane_grp = jnp.right_shift(lax.broadcasted_iota(jnp.int32, (SGU_CHUNK, SGU_WIDTH), 1),
                               _log2(SGU_CH))
    r = lax.broadcasted_iota(jnp.int32, (SGU_CHUNK, SGU_GROUPS * SGU_CHUNK), 0)
    c = lax.broadcasted_iota(jnp.int32, (SGU_CHUNK, SGU_GROUPS * SGU_CHUNK), 1) & (SGU_CHUNK - 1)
    ws = jnp.where(c <= r, ws_ref[...], 0.0).astype(BF16)
    for n in range(t // SGU_CHUNK):
        vc = vn[n * SGU_CHUNK:(n + 1) * SGU_CHUNK]
        stacked = jnp.concatenate(
            [jnp.where(lane_grp == gi, vc, 0.0) for gi in range(SGU_GROUPS)], axis=0).astype(BF16)
        mixed = _dot(ws, stacked) + bs_ref[...]
        u = _gelu(u_ref[0, n * SGU_CHUNK:(n + 1) * SGU_CHUNK, :])
        o_ref[0, n * SGU_CHUNK:(n + 1) * SGU_CHUNK, :] = (u * mixed).astype(BF16)


def _sgu_mixer(u, v, ng, nbias, ws_cat, bs_exp, l):
    b, s, c = u.shape
    t = T_MIX
    row = pl.BlockSpec((1, t, c), lambda bi, i: (bi, i, 0))
    return pl.pallas_call(
        _sgu_kernel, grid=(b, s // t),
        in_specs=[row, row] + [_layer_spec(a, l) for a in (ng, nbias, ws_cat, bs_exp)],
        out_specs=row, out_shape=jax.ShapeDtypeStruct((b, s, c), BF16),
        compiler_params=_params("parallel", "parallel"), name="sgu_mixer",
    )(u, v, ng, nbias, ws_cat, bs_exp)


def _out_proj_kernel(oc_ref, os_ref, ow_ref, pool_ref, sgu_ref, w_ref, g_ref, x_ref, y_ref):
    a = (oc_ref[0].astype(F32) + os_ref[0].astype(F32) + ow_ref[0].astype(F32)).astype(BF16)
    mix = (_dot(a, w_ref[:NSA_WIDTH, :])
           + _dot(pool_ref[0], w_ref[NSA_WIDTH:NSA_WIDTH + POOL_WIDTH, :])
           + _dot(sgu_ref[0], w_ref[NSA_WIDTH + POOL_WIDTH:, :]))
    y_ref[0] = x_ref[0] + _rms(mix, g_ref[...])


def _out_proj(o_cmp, o_sel, o_win, pool, sgu, w, g, x, l):
    b, s, d = x.shape
    t = T_OUT
    row = lambda n: pl.BlockSpec((1, t, n), lambda bi, i: (bi, i, 0))
    return pl.pallas_call(
        _out_proj_kernel, grid=(b, s // t),
        in_specs=[row(NSA_WIDTH), row(NSA_WIDTH), row(NSA_WIDTH),
                  row(POOL_WIDTH), row(SGU_WIDTH), _layer_spec(w, l), _layer_spec(g, l), row(d)],
        out_specs=row(d), out_shape=jax.ShapeDtypeStruct((b, s, d), F32),
        compiler_params=_params("parallel", "parallel"), name="out_proj",
    )(o_cmp, o_sel, o_win, pool, sgu, w, g, x)


def _mem_kv_kernel(mem_ref, g_ref, wk_ref, wv_ref, k_ref, v_ref):
    m = _rms(mem_ref[0], g_ref[...]).astype(BF16)
    k_ref[0] = _dot(m, wk_ref[...]).astype(BF16)
    v_ref[0] = _dot(m, wv_ref[...]).astype(BF16)


def _mem_kv(mem, g, wk, wv, l):
    b, m, d = mem.shape
    blk = pl.BlockSpec((1, m, d), lambda bi: (bi, 0, 0))
    return pl.pallas_call(
        _mem_kv_kernel, grid=(b,),
        in_specs=[blk] + [_layer_spec(a, l) for a in (g, wk, wv)],
        out_specs=(blk, blk),
        out_shape=(jax.ShapeDtypeStruct((b, m, d), BF16), jax.ShapeDtypeStruct((b, m, d), BF16)),
        compiler_params=_params("parallel"), name="mem_kv",
    )(mem, g, wk, wv)


def _mem_attn_kernel(x_ref, gpre_ref, wq_ref, k_ref, v_ref, wo_ref, gpost_ref, y_ref):
    x = x_ref[0]
    h = _rms(x, gpre_ref[...]).astype(BF16)
    q = (_dot(h, wq_ref[...]) * (MEM_HEAD_DIM ** -0.5)).astype(BF16)
    outs = []
    for hd in range(MEM_HEADS):
        sl = slice(hd * MEM_HEAD_DIM, (hd + 1) * MEM_HEAD_DIM)
        s = _dot_nt(q[:, sl], k_ref[0, :, sl])
        e = jnp.exp(s - jnp.max(s, axis=-1, keepdims=True))
        p = e / jnp.sum(e, axis=-1, keepdims=True)
        outs.append(_dot(p.astype(BF16), v_ref[0, :, sl]).astype(BF16))
    o = jnp.concatenate(outs, axis=1)
    y_ref[0] = x + _rms(_dot(o, wo_ref[...]), gpost_ref[...])


def _mem_attn(x, gpre, wq, k, v, wo, gpost, l):
    b, s, d = x.shape
    m = k.shape[1]
    t = T_MEM
    row = pl.BlockSpec((1, t, d), lambda bi, i: (bi, i, 0))
    kv = pl.BlockSpec((1, m, d), lambda bi, i: (bi, 0, 0))
    return pl.pallas_call(
        _mem_attn_kernel, grid=(b, s // t),
        in_specs=[row, _layer_spec(gpre, l), _layer_spec(wq, l), kv, kv, _layer_spec(wo, l),
                  _layer_spec(gpost, l)],
        out_specs=row, out_shape=jax.ShapeDtypeStruct((b, s, d), F32),
        compiler_params=_params("parallel", "parallel"), name="mem_attention",
    )(x, gpre, wq, k, v, wo, gpost)


def _ffn_kernel(x_ref, halo_ref, gpre_ref, wup_ref, cw_ref, cb_ref, wd_ref, gpost_ref, y_ref,
                act_ref):
    i = pl.program_id(1)
    f = wd_ref.shape[0]
    halo = jnp.where(i > 0, halo_ref[0], 0.0)
    h = _rms(jnp.concatenate([halo, x_ref[0]], axis=0), gpre_ref[...]).astype(BF16)

    def conv(off):
        a = _dot(h, wup_ref[:, off:off + F_CHUNK])
        cw = cw_ref[:, off:off + F_CHUNK]
        return (cw[0:1] * pltpu.roll(a, 2, 0) + cw[1:2] * pltpu.roll(a, 1, 0) + cw[2:3] * a
                + cb_ref[:, off:off + F_CHUNK])

    for off in range(0, f, F_CHUNK):
        act = _gelu(conv(off)) * conv(f + off)
        act_ref[:, off:off + F_CHUNK] = act[HALO:].astype(BF16)
    y_ref[0] = x_ref[0] + _rms(_dot(act_ref[...], wd_ref[...]), gpost_ref[...])


def _ffn(x, gpre, w_up, conv_w, conv_b, w_down, gpost, l):
    b, s, d = x.shape
    f = w_down.shape[1]
    t = T_FFN
    assert f % F_CHUNK == 0
    resident = lambda a: _layer_spec(a, l, pipeline_mode=pl.Buffered(1))
    return pl.pallas_call(
        _ffn_kernel, grid=(b, s // t),
        in_specs=[
            pl.BlockSpec((1, t, d), lambda bi, i: (bi, i, 0)),
            pl.BlockSpec((1, HALO, d), lambda bi, i: (bi, jnp.maximum(i * (t // HALO) - 1, 0), 0)),
            resident(gpre), resident(w_up), resident(conv_w), resident(conv_b), resident(w_down),
            resident(gpost),
        ],
        out_specs=pl.BlockSpec((1, t, d), lambda bi, i: (bi, i, 0)),
        out_shape=jax.ShapeDtypeStruct((b, s, d), F32),
        scratch_shapes=[pltpu.VMEM((t, f), BF16)],
        compiler_params=_params("parallel", "parallel"), name="conv_ffn",
    )(x, x, gpre, w_up, conv_w, conv_b, w_down, gpost)


def kernel(x, mem, rel_bias, mix_norm_pre, mix_norm_post, w_in, cmp_pos, cmp_w1, cmp_b1, cmp_w2,
           cmp_b2, pool_w, pool_scale, sgu_norm_g, sgu_norm_b, sgu_w, sgu_b, w_out, mem_norm_pre,
           mem_norm_kv, mem_norm_post, w_mq, w_mk, w_mv, w_mo, ffn_norm_pre, ffn_norm_post, w_up,
           conv_w, conv_b, w_down):
    depth = w_in.shape[0]
    s = x.shape[1]
    assert s % max(T_IN, T_CMP, T_ATT, T_OUT, T_MIX, T_MEM, T_FFN) == 0
    nb = s // CMP_STRIDE

    w_in_p = _in_proj_weights(w_in)
    band, wagg = _cmp_tables(rel_bias, nb)
    toep = _toeplitz_generators(rel_bias)
    row = lambda a: a[:, None, :]
    bf = lambda a: a.astype(BF16)
    w_out_b, w_mq_b, w_mk_b, w_mv_b, w_mo_b = bf(w_out), bf(w_mq), bf(w_mk), bf(w_mv), bf(w_mo)
    w_up_b, w_down_b = bf(w_up), bf(w_down)
    g_mix_pre, g_mix_post = row(mix_norm_pre), row(mix_norm_post)
    g_mem_pre, g_mem_kv, g_mem_post = row(mem_norm_pre), row(mem_norm_kv), row(mem_norm_post)
    g_ffn_pre, g_ffn_post, conv_b_r = row(ffn_norm_pre), row(ffn_norm_post), row(conv_b)
    cw = jax.vmap(_compress_weights)(cmp_pos, cmp_w1, cmp_b1, cmp_w2, cmp_b2)
    n_pool = len(POOL_WINDOWS)
    pool_blk = bf(jnp.einsum("lgcd,gh->lgchd", pool_w, jnp.eye(n_pool, dtype=F32)).reshape(
        depth, POOL_WIDTH, POOL_WIDTH))
    pool_scale_r = row(pool_scale)
    sgu_g, sgu_bias = row(sgu_norm_g), row(sgu_norm_b)
    ws_cat = jnp.concatenate([sgu_w[:, gi] for gi in range(SGU_GROUPS)], axis=2)
    bs_exp = jnp.repeat(jnp.swapaxes(sgu_b, 1, 2), SGU_CH, axis=2)

    for l in range(depth):
        qa, ksel_t, kwin_t, kcmp, vcmp, vsel, vwin, gates, pin, u, v = _in_proj(
            x, g_mix_pre, w_in_p, l)
        kc, vct = _compress(kcmp, vcmp, cw, l)
        o_cmp, q_aug = _cmp_attention(gates, qa, kc, vct, band, wagg)
        o_sel = _flash(gates, q_aug, ksel_t, vsel, toep, windowed=False, branch=1)
        o_win = _flash(gates, qa, kwin_t, vwin, toep, windowed=True, branch=2)
        out_b = _pool_mixer(pin, pool_blk, pool_scale_r, l)
        out_c = _sgu_mixer(u, v, sgu_g, sgu_bias, ws_cat, bs_exp, l)
        x = _out_proj(o_cmp, o_sel, o_win, out_b, out_c, w_out_b, g_mix_post, x, l)
        km, vm = _mem_kv(mem, g_mem_kv, w_mk_b, w_mv_b, l)
        x = _mem_attn(x, g_mem_pre, w_mq_b, km, vm, w_mo_b, g_mem_post, l)
        x = _ffn(x, g_ffn_pre, w_up_b, conv_w, conv_b_r, w_down_b, g_ffn_post, l)
    return x
```

```python
import functools
import math

import numpy as np
import jax
import jax.numpy as jnp
from jax import lax
from jax.experimental import pallas as pl
from jax.experimental.pallas import tpu as pltpu

F32 = jnp.float32
BF16 = jnp.bfloat16

D_MODEL = 1024
NSA_WIDTH = 512
POOL_WIDTH = 256
SGU_WIDTH = 256
NSA_HEADS = 8
KV_GROUPS = 2
HEADS_PER_GROUP = 4
HEAD_DIM = 64
KV_WIDTH = 128
GATE_WIDTH = 24
CMP_STRIDE = 16
CMP_LEN = 32
CMP_HIDDEN = 128
SEL_BLOCK = 64
SEL_TOPK = 16
N_LOCAL_SEL = 2
SEL_AGG = (1.0, 2.0, 2.0, 2.0, 1.0)
FORCE_SCORE = 1e9
WINDOW = 512
REL_BUCKETS = 32
REL_MAX_DIST = 1024
POOL_WINDOWS = (2, 4, 8, 16)
POOL_CH = 64
SGU_GROUPS = 4
SGU_CH = 64
SGU_CHUNK = 128
MEM_HEADS = 4
MEM_HEAD_DIM = 256
FFN_HIDDEN = 2816
EPS = 1e-6

LANES = 128
SUBLANES = 8
BF16_ROWS = 16
VMEM_LIMIT = 56 * 1024 * 1024

T_IN = 512
T_CMP = 256
T_ATT = 512
T_OUT = 512
T_MEM = 512
T_FFN = 1024
F_CHUNK = 256
HALO = 16

MASK_BIAS = -float(2 ** 30)
NEG_INIT = -1e30

_Q_OFF = 0
_KSEL_OFF = 1024
_KWIN_OFF = 1152
_KCMP_OFF = 1280
_VCMP_OFF = 1408
_VSEL_OFF = 1536
_VWIN_OFF = 1792
_GATE_OFF = 2048
_POOL_OFF = 2304
_U_OFF = 2560
_V_OFF = 2816
IN_COLS = 3072
MXU_COLS = 256


def _bucket_table(n_max):
    n = np.arange(n_max)
    max_exact = REL_BUCKETS // 2
    nf = np.maximum(n, 1).astype(np.float64)
    large = max_exact + (np.log(nf / max_exact) / math.log(REL_MAX_DIST / max_exact)
                         * (REL_BUCKETS - max_exact)).astype(np.int32)
    large = np.minimum(large, REL_BUCKETS - 1)
    return np.where(n < max_exact, n, large).astype(np.int32)


_BUCKETS = _bucket_table(16384)
FAR_DIST = int(np.argmax(_BUCKETS == REL_BUCKETS - 1))


def _bias_lookup(rel_bias, buckets):
    onehot = jnp.asarray(np.eye(REL_BUCKETS, dtype=np.float32)[buckets])
    return jnp.einsum("...k,kh->...h", onehot, rel_bias, precision=lax.Precision.HIGHEST)


def _in_proj_layout():
    moves = []
    for h in range(NSA_HEADS):
        g = h // HEADS_PER_GROUP
        moves.append((_Q_OFF + h * LANES + g * HEAD_DIM, h * HEAD_DIM, HEAD_DIM))
    base = NSA_WIDTH
    moves.append((_KCMP_OFF, base, KV_WIDTH))
    moves.append((_VCMP_OFF, base + KV_WIDTH, KV_WIDTH))
    moves.append((_KSEL_OFF, base + 2 * KV_WIDTH, KV_WIDTH))
    for g in range(KV_GROUPS):
        moves.append((_VSEL_OFF + g * LANES, base + 3 * KV_WIDTH + g * HEAD_DIM, HEAD_DIM))
    moves.append((_KWIN_OFF, base + 4 * KV_WIDTH, KV_WIDTH))
    for g in range(KV_GROUPS):
        moves.append((_VWIN_OFF + g * LANES, base + 5 * KV_WIDTH + g * HEAD_DIM, HEAD_DIM))
    base += 6 * KV_WIDTH
    moves.append((_GATE_OFF, base, GATE_WIDTH))
    base += GATE_WIDTH
    moves.append((_POOL_OFF, base, POOL_WIDTH))
    moves.append((_U_OFF, base + POOL_WIDTH, SGU_WIDTH))
    moves.append((_V_OFF, base + POOL_WIDTH + SGU_WIDTH, SGU_WIDTH))
    return sorted(moves)


def _in_proj_weights(w_in):
    depth, d, _ = w_in.shape
    pieces, pos = [], 0
    for dst, src, n in _in_proj_layout():
        if dst > pos:
            pieces.append(jnp.zeros((depth, d, dst - pos), BF16))
        pieces.append(w_in[:, :, src:src + n].astype(BF16))
        pos = dst + n
    pieces.append(jnp.zeros((depth, d, IN_COLS - pos), BF16))
    return jnp.concatenate(pieces, axis=2)


def _log2(n):
    assert n & (n - 1) == 0
    return n.bit_length() - 1


def _params(*sem):
    return pltpu.CompilerParams(dimension_semantics=sem, vmem_limit_bytes=VMEM_LIMIT)


def _layer_spec(a, l, **kw):
    return pl.BlockSpec((None,) + a.shape[1:], lambda *_: (l,) + (0,) * (a.ndim - 1), **kw)


def _rms(x, g):
    return x * lax.rsqrt(jnp.mean(x * x, axis=-1, keepdims=True) + EPS) * g


def _gelu(x):
    c = math.sqrt(2.0 / math.pi)
    return x * (0.5 * (1.0 + jnp.tanh(c * (x + 0.044715 * (x * x * x)))))


def _dot(a, b):
    return jnp.dot(a, b, preferred_element_type=F32)


def _dot_nt(a, b):
    return lax.dot_general(a, b, (((1,), (1,)), ((), ())), preferred_element_type=F32)


def _split3(x):
    hi = x.astype(BF16)
    r1 = x - hi.astype(F32)
    mid = r1.astype(BF16)
    lo = (r1 - mid.astype(F32)).astype(BF16)
    return hi, mid, lo


def _dot_f32_left(w_bf16, x):
    hi, mid, lo = _split3(x)
    return _dot(w_bf16, hi) + _dot(w_bf16, mid) + _dot(w_bf16, lo)


def _in_proj_kernel(x_ref, g_ref, w_ref, qa_ref, ksel_ref, kwin_ref, kcmp_ref, vcmp_ref,
                    vsel_ref, vwin_ref, gate_ref, pool_ref, u_ref, v_ref):
    t = x_ref.shape[1]
    h = _rms(x_ref[0], g_ref[...]).astype(BF16)

    def proj(off):
        return _dot(h, w_ref[:, off:off + MXU_COLS])

    def slabs(off):
        p = proj(off)
        return p[:, :LANES], p[:, LANES:]

    scale = HEAD_DIM ** -0.5
    for hd in range(0, NSA_HEADS, 2):
        qa, qb = slabs(_Q_OFF + hd * LANES)
        qa_ref[0, hd] = (qa * scale).astype(BF16)
        qa_ref[0, hd + 1] = (qb * scale).astype(BF16)

    lane = lax.broadcasted_iota(jnp.int32, (t, LANES), 1)
    ksel, kwin = slabs(_KSEL_OFF)
    ksel_ref[0, :LANES, :] = ksel.T.astype(BF16)
    tok = pl.program_id(1) * t + lax.broadcasted_iota(jnp.int32, (LANES, t), 1)
    blk_id = lax.broadcasted_iota(jnp.int32, (LANES, t), 0)
    ksel_ref[0, LANES:, :] = jnp.where(
        jnp.right_shift(tok, _log2(SEL_BLOCK)) == blk_id, 1.0, 0.0).astype(BF16)
    kwin_ref[0] = kwin.T.astype(BF16)
    kcmp, vcmp = slabs(_KCMP_OFF)
    kcmp_ref[0] = kcmp.astype(BF16)
    vcmp_ref[0] = vcmp.astype(BF16)
    for v_out, off in ((vsel_ref, _VSEL_OFF), (vwin_ref, _VWIN_OFF)):
        for g, vg in enumerate(slabs(off)):
            v_out[0, g] = jnp.where(lane == HEAD_DIM, 1.0, vg).astype(BF16)
    gate_ref[0] = jax.nn.sigmoid(slabs(_GATE_OFF)[0])
    pool_ref[0] = proj(_POOL_OFF)
    u_ref[0] = proj(_U_OFF)
    v_ref[0] = proj(_V_OFF)


def _in_proj(x, g, w, l):
    b, s, d = x.shape
    t = T_IN
    grid = (b, s // t)
    row3 = lambda n: pl.BlockSpec((1, t, n), lambda bi, i: (bi, i, 0))
    out_shape = (
        jax.ShapeDtypeStruct((b, NSA_HEADS, s, LANES), BF16),
        jax.ShapeDtypeStruct((b, 2 * LANES, s), BF16),
        jax.ShapeDtypeStruct((b, LANES, s), BF16),
        jax.ShapeDtypeStruct((b, s, LANES), BF16),
        jax.ShapeDtypeStruct((b, s, LANES), BF16),
        jax.ShapeDtypeStruct((b, KV_GROUPS, s, LANES), BF16),
        jax.ShapeDtypeStruct((b, KV_GROUPS, s, LANES), BF16),
        jax.ShapeDtypeStruct((b, s, LANES), F32),
        jax.ShapeDtypeStruct((b, s, POOL_WIDTH), F32),
        jax.ShapeDtypeStruct((b, s, SGU_WIDTH), F32),
        jax.ShapeDtypeStruct((b, s, SGU_WIDTH), F32),
    )
    out_specs = (
        pl.BlockSpec((1, NSA_HEADS, t, LANES), lambda bi, i: (bi, 0, i, 0)),
        pl.BlockSpec((1, 2 * LANES, t), lambda bi, i: (bi, 0, i)),
        pl.BlockSpec((1, LANES, t), lambda bi, i: (bi, 0, i)),
        row3(LANES), row3(LANES),
        pl.BlockSpec((1, KV_GROUPS, t, LANES), lambda bi, i: (bi, 0, i, 0)),
        pl.BlockSpec((1, KV_GROUPS, t, LANES), lambda bi, i: (bi, 0, i, 0)),
        row3(LANES), row3(POOL_WIDTH), row3(SGU_WIDTH), row3(SGU_WIDTH),
    )
    return pl.pallas_call(
        _in_proj_kernel, grid=grid,
        in_specs=[row3(d), _layer_spec(g, l), _layer_spec(w, l)],
        out_specs=out_specs, out_shape=out_shape,
        compiler_params=_params("parallel", "parallel"), name="in_proj",
    )(x, g, w)


def _compress_kernel(rk_ref, rv_ref, w1a_ref, w1b_ref, pos_ref, b1_ref, w2k_ref, b2k_ref,
                     w2v_ref, b2v_ref, kc_ref, vct_ref):
    nb = rk_ref.shape[1]
    keep = lax.broadcasted_iota(jnp.int32, (nb, LANES), 0) < nb - 1

    def hidden(r, which):
        a = _dot(r, w1a_ref[which])
        bm = _dot(r, w1b_ref[which])
        pre = a + pltpu.roll(bm, nb - 1, 0)
        posb = (_dot(pos_ref[which, 0], w1a_ref[which])[0:1]
                + _dot(pos_ref[which, 1], w1b_ref[which])[0:1])
        return _gelu(pre + posb + b1_ref[which]).astype(BF16)

    kc = _dot(hidden(rk_ref[0], 0), w2k_ref[...]) + b2k_ref[...]
    kc_ref[0] = jnp.where(keep, kc, 0.0).astype(BF16)
    vc = _dot(hidden(rv_ref[0], 1), w2v_ref[...]) + b2v_ref[...]
    vct_ref[0] = jnp.where(keep, vc, 0.0).T.astype(BF16)


def _compress(kcmp, vcmp, cw, l):
    b, s, _ = kcmp.shape
    nb = s // CMP_STRIDE
    rk = kcmp.reshape(b, nb, CMP_STRIDE * KV_WIDTH)
    rv = vcmp.reshape(b, nb, CMP_STRIDE * KV_WIDTH)
    ws = (cw["w1a"], cw["w1b"], cw["pos"], cw["b1"], cw["w2k"], cw["b2k"], cw["w2v"], cw["b2v"])
    return pl.pallas_call(
        _compress_kernel, grid=(b,),
        in_specs=[pl.BlockSpec((1, nb, rk.shape[2]), lambda bi: (bi, 0, 0)),
                  pl.BlockSpec((1, nb, rv.shape[2]), lambda bi: (bi, 0, 0))]
                 + [_layer_spec(a, l) for a in ws],
        out_specs=(pl.BlockSpec((1, nb, KV_WIDTH), lambda bi: (bi, 0, 0)),
                   pl.BlockSpec((1, KV_WIDTH, nb), lambda bi: (bi, 0, 0))),
        out_shape=(jax.ShapeDtypeStruct((b, nb, KV_WIDTH), BF16),
                   jax.ShapeDtypeStruct((b, KV_WIDTH, nb), BF16)),
        compiler_params=_params("parallel"), name="compress",
    )(rk, rv, *ws)


def _compress_weights(pos, w1, b1, w2, b2):
    eye = jnp.eye(KV_GROUPS, dtype=F32)
    half = CMP_LEN // 2

    def w1_blocks(w):
        return jnp.einsum("ldh,pg->lpdgh", w, eye).reshape(half * KV_WIDTH, KV_GROUPS * CMP_HIDDEN)

    def pos_rows(p):
        r = jnp.broadcast_to(p[:, None, :], (half, KV_GROUPS, HEAD_DIM)).reshape(1, half * KV_WIDTH)
        return jnp.concatenate([r, jnp.zeros((SUBLANES - 1, half * KV_WIDTH), F32)], axis=0)

    w1a = jnp.stack([w1_blocks(w1[i, :half]) for i in range(2)]).astype(BF16)
    w1b = jnp.stack([w1_blocks(w1[i, half:]) for i in range(2)]).astype(BF16)
    posr = jnp.stack([jnp.stack([pos_rows(pos[i, :half]), pos_rows(pos[i, half:])])
                      for i in range(2)]).astype(BF16)
    b1t = jnp.tile(b1, (1, KV_GROUPS))[:, None, :]
    w2blk = lambda w: jnp.einsum("hd,pg->phgd", w, eye).reshape(KV_GROUPS * CMP_HIDDEN, KV_WIDTH)
    return dict(w1a=w1a, w1b=w1b, pos=posr, b1=b1t,
                w2k=w2blk(w2[0]).astype(BF16), b2k=jnp.tile(b2[0], KV_GROUPS)[None, :],
                w2v=w2blk(w2[1]).astype(BF16), b2v=jnp.tile(b2[1], KV_GROUPS)[None, :])


CMP_PAD = 64
CMP_BAND_BACK = 56
CMP_BAND = CMP_BAND_BACK + T_CMP // CMP_STRIDE
CMP_BAND_DIST0 = CMP_BAND_BACK * CMP_STRIDE - (CMP_LEN - 1)
assert CMP_BAND_DIST0 + CMP_STRIDE >= FAR_DIST and CMP_BAND_BACK <= CMP_PAD
CMP_ROW_CHUNK = 128
SEL_RATIO = SEL_BLOCK // CMP_STRIDE


def _gate_column(head, branch):
    return head * 3 + branch


def _cmp_kernel(gate_ref, qa_ref, kc_ref, vct_ref, band_ref, wagg_ref, ocmp_ref, qaug_ref,
                st_ref, gt_ref):
    tq = qa_ref.shape[2]
    nb = kc_ref.shape[1]
    i = pl.program_id(1)
    g = pl.program_id(0)
    s0 = i * tq
    t = s0 + lax.broadcasted_iota(jnp.int32, (1, tq), 1)
    n_live = (i + 1) * (tq // CMP_STRIDE)
    band_row0 = pl.multiple_of(s0 // CMP_STRIDE - CMP_BAND_BACK + CMP_PAD, SUBLANES)
    gt_ref[...] = gate_ref[0].T
    st_ref[:, :CMP_PAD, :] = jnp.zeros((HEADS_PER_GROUP, CMP_PAD, tq), F32)
    has_block = t >= CMP_LEN - 1

    def body(rows):
        nsr = rows // SEL_RATIO
        kc = kc_ref[0, :rows, :]
        tail = rows - CMP_ROW_CHUNK
        dead = tail + lax.broadcasted_iota(jnp.int32, (CMP_ROW_CHUNK, 1), 0) >= n_live
        imp = jnp.zeros((rows, tq), F32)
        outs = []
        for h in range(HEADS_PER_GROUP):
            st_ref[h, CMP_PAD:CMP_PAD + rows, :] = _dot_nt(kc, qa_ref[0, h])
        for h in range(HEADS_PER_GROUP):
            st_ref[h, pl.ds(band_row0, CMP_BAND), :] += band_ref[h]
        for h in range(HEADS_PER_GROUP):
            last = jnp.where(dead, MASK_BIAS, st_ref[h, CMP_PAD + tail:CMP_PAD + rows, :])
            s = last if tail == 0 else jnp.concatenate(
                [st_ref[h, CMP_PAD:CMP_PAD + tail, :], last], axis=0)
            e = jnp.exp(s - jnp.max(s, axis=0, keepdims=True))
            denom = jnp.maximum(jnp.sum(e, axis=0, keepdims=True), 1.0)
            rinv = jnp.where(has_block, 1.0 / denom, 0.0)
            imp = imp + e * rinv
            gate = gt_ref[pl.ds(_gate_column(g * HEADS_PER_GROUP + h, 0), 1), :]
            outs.append((gate * rinv) * _dot(vct_ref[0, :, :rows], e.astype(BF16)))
        ocmp_ref[0] = jnp.concatenate(outs, axis=0).T.astype(BF16)

        p_sel = _dot_f32_left(wagg_ref[:nsr, :rows], imp)
        j = lax.broadcasted_iota(jnp.int32, (nsr, 1), 0)
        rel = jnp.right_shift(t, _log2(SEL_BLOCK)) - j
        forced = (j == 0) | ((rel >= 0) & (rel < N_LOCAL_SEL))
        score = jnp.where(forced, FORCE_SCORE, jnp.where(j * SEL_BLOCK <= t, p_sel, -1.0))
        jfull = jnp.broadcast_to(j, (nsr, tq)).astype(F32)
        sel = jnp.zeros((nsr, tq), F32)
        for _ in range(SEL_TOPK):
            m = jnp.max(score, axis=0, keepdims=True)
            first = jnp.min(jnp.where(score == m, jfull, float(nsr)), axis=0, keepdims=True)
            hit = jfull == first
            sel = jnp.where(hit, 1.0, sel)
            score = jnp.where(hit, -3e38, score)
        notsel = jnp.where(sel > 0.5, 0.0, MASK_BIAS)
        if nsr < LANES:
            notsel = jnp.concatenate(
                [notsel, jnp.full((LANES - nsr, tq), MASK_BIAS, F32)], axis=0)
        notsel = notsel.T.astype(BF16)
        for h in range(HEADS_PER_GROUP):
            qaug_ref[0, h, :, :LANES] = qa_ref[0, h]
            qaug_ref[0, h, :, LANES:] = notsel

    for rows in range(CMP_ROW_CHUNK, nb + 1, CMP_ROW_CHUNK):
        @pl.when((n_live > rows - CMP_ROW_CHUNK) & (n_live <= rows))
        def _():
            body(rows)


def _cmp_attention(gates, qa, kc, vct, band, wagg):
    b, _, s, _ = qa.shape
    nb = kc.shape[1]
    tq = T_CMP
    assert nb % CMP_ROW_CHUNK == 0
    grid = (KV_GROUPS, s // tq, b)
    return pl.pallas_call(
        _cmp_kernel, grid=grid,
        in_specs=[
            pl.BlockSpec((1, tq, LANES), lambda g, i, bi: (bi, i, 0)),
            pl.BlockSpec((1, HEADS_PER_GROUP, tq, LANES), lambda g, i, bi: (bi, g, i, 0)),
            pl.BlockSpec((1, nb, KV_WIDTH), lambda g, i, bi: (bi, 0, 0)),
            pl.BlockSpec((1, HEAD_DIM, nb), lambda g, i, bi: (bi, g, 0)),
            pl.BlockSpec((HEADS_PER_GROUP, CMP_BAND, tq), lambda g, i, bi: (g, 0, 0)),
            pl.BlockSpec(wagg.shape, lambda g, i, bi: (0, 0)),
        ],
        out_specs=(
            pl.BlockSpec((1, tq, HEADS_PER_GROUP * HEAD_DIM), lambda g, i, bi: (bi, i, g)),
            pl.BlockSpec((1, HEADS_PER_GROUP, tq, 2 * LANES), lambda g, i, bi: (bi, g, i, 0)),
        ),
        out_shape=(jax.ShapeDtypeStruct((b, s, NSA_WIDTH), BF16),
                   jax.ShapeDtypeStruct((b, NSA_HEADS, s, 2 * LANES), BF16)),
        scratch_shapes=[pltpu.VMEM((HEADS_PER_GROUP, CMP_PAD + nb, tq), F32),
                        pltpu.VMEM((LANES, tq), F32)],
        compiler_params=_params("parallel", "parallel", "parallel"), name="cmp_attention",
    )(gates, qa, kc, vct, band, wagg)


def _cmp_tables(rel_bias, nb):
    jj = np.arange(CMP_BAND)[:, None]
    ii = np.arange(T_CMP)[None, :]
    dist = CMP_BAND_DIST0 + ii - CMP_STRIDE * jj
    bidx = _BUCKETS[np.clip(dist, 0, None)]
    band = jnp.transpose(_bias_lookup(rel_bias, bidx) - rel_bias[REL_BUCKETS - 1], (2, 0, 1))
    band = jnp.where(jnp.asarray(dist >= 0)[None], band, MASK_BIAS)
    ns = nb // SEL_RATIO
    assert SEL_TOPK <= ns <= LANES
    wagg = np.zeros((LANES, nb), np.float32)
    for m, w in enumerate(SEL_AGG):
        for jb in range(ns):
            c = SEL_RATIO * jb + m - 1
            if 0 <= c < nb - 1:
                wagg[jb, c] = w
    return band, jnp.asarray(wagg, BF16)


N_NEAR = (FAR_DIST + T_ATT - 1) // T_ATT + 1
assert N_NEAR * T_ATT - (T_ATT - 1) >= FAR_DIST
WIN_TILES = (WINDOW - 1 + T_ATT - 1) // T_ATT
TOEPLITZ_ROWS = 8
assert max(N_NEAR, WIN_TILES + 1) <= TOEPLITZ_ROWS
FAR_KEYS = 4 * T_ATT


def _flash_kernel(gate_ref, q_ref, kt_ref, v_ref, gv_ref, o_ref, m_ref, acc_ref, nb_ref, *,
                  windowed, branch):
    t = T_ATT
    g = pl.program_id(1)
    i = pl.program_id(2)
    n_tab = WIN_TILES + 1 if windowed else N_NEAR

    @pl.when(i == 0)
    def _():
        for h in range(HEADS_PER_GROUP):
            for d in range(n_tab):
                gen = jnp.broadcast_to(gv_ref[0, h, d:d + 1, :], (t, 2 * t))
                nb_ref[h, d] = pltpu.roll(gen, 0, 1, stride=1, stride_axis=0)[:, :t]

    m_ref[...] = jnp.full(m_ref.shape, NEG_INIT, F32)
    acc_ref[...] = jnp.zeros(acc_ref.shape, F32)

    def tile_step(first_tile, n_tiles, dists=(), causal=False):
        heads = range(HEADS_PER_GROUP)
        assert not dists or len(dists) == n_tiles
        nk = t * n_tiles
        start = pl.multiple_of(first_tile * t, t)
        kt = kt_ref[0, :, pl.ds(start, nk)]
        v = v_ref[0, 0, pl.ds(start, nk), :]
        s = [_dot(q_ref[0, h], kt) for h in heads]
        if dists:
            s = [s[h] + jnp.concatenate([nb_ref[h, d] for d in dists], axis=1) for h in heads]
            dist = (dists[0] * t + lax.broadcasted_iota(jnp.int32, (t, nk), 0)
                    - lax.broadcasted_iota(jnp.int32, (t, nk), 1))
            keep = None
            if causal:
                keep = dist >= 0
            elif windowed and dists[0] * t + t - 1 >= WINDOW:
                keep = dist < WINDOW
            if keep is not None:
                s = [jnp.where(keep, s[h], MASK_BIAS) for h in heads]
        m_prev = [m_ref[h] for h in heads]
        m_new = [jnp.maximum(m_prev[h], jnp.max(s[h], axis=1, keepdims=True)) for h in heads]
        p = [jnp.exp(s[h] - jnp.tile(m_new[h], (1, nk // LANES))).astype(BF16) for h in heads]
        pv = [_dot(p[h], v) for h in heads]
        for h in heads:
            acc_ref[h] = jnp.exp(m_prev[h] - m_new[h]) * acc_ref[h] + pv[h]
            m_ref[h] = m_new[h]

    def near_steps(first):
        @pl.when(i >= first)
        def _():
            d = first
            if d % 2:
                tile_step(i - d, 1, (d,))
                d -= 1
            for d in range(d, 0, -2):
                tile_step(i - d, 2, (d, d - 1))

        @pl.when(i < first)
        def _():
            for d in range(first - 1, 0, -1):
                @pl.when(i >= d)
                def _():
                    tile_step(i - d, 1, (d,))

    if windowed:
        near_steps(WIN_TILES)
    else:
        n_far = jnp.maximum(i - (N_NEAR - 1), 0)
        far_tiles = FAR_KEYS // t

        def far_body(j, carry):
            tile_step(j * far_tiles, far_tiles)
            return carry
        n_far_steps = jnp.right_shift(n_far, _log2(far_tiles))
        lax.fori_loop(0, n_far_steps, far_body, 0)
        done = n_far_steps * far_tiles
        run = far_tiles // 2
        while run >= 1:
            has_run = jnp.bitwise_and(n_far, run) != 0

            @pl.when(has_run)
            def _():
                tile_step(done, run)
            done = done + jnp.where(has_run, run, 0)
            run //= 2
        near_steps(N_NEAR - 1)
    tile_step(i, 1, (0,), causal=True)

    lane = lax.broadcasted_iota(jnp.int32, (t, LANES), 1)
    gates = gate_ref[0]
    outs = []
    for h in range(HEADS_PER_GROUP):
        col = _gate_column(g * HEADS_PER_GROUP + h, branch)
        gate = jnp.sum(jnp.where(lane == col, gates, 0.0), axis=1, keepdims=True)
        acc = acc_ref[h]
        outs.append(acc[:, :HEAD_DIM] / jnp.maximum(acc[:, HEAD_DIM:HEAD_DIM + 1], 1.0) * gate)
    o_ref[0] = jnp.concatenate(outs, axis=1).astype(BF16)


def _flash(gates, q_aug, kt, v, gv, *, windowed, branch):
    b, _, s, kd = q_aug.shape
    assert kt.shape[1] == kd
    t = T_ATT
    grid = (b, KV_GROUPS, s // t)
    return pl.pallas_call(
        functools.partial(_flash_kernel, windowed=windowed, branch=branch), grid=grid,
        in_specs=[
            pl.BlockSpec((1, t, LANES), lambda bi, g, i: (bi, i, 0)),
            pl.BlockSpec((1, HEADS_PER_GROUP, t, kd), lambda bi, g, i: (bi, g, i, 0)),
            pl.BlockSpec((1, kd, s), lambda bi, g, i: (bi, 0, 0)),
            pl.BlockSpec((1, 1, s, LANES), lambda bi, g, i: (bi, g, 0, 0)),
            pl.BlockSpec((1, HEADS_PER_GROUP, TOEPLITZ_ROWS, 2 * t), lambda bi, g, i: (g, 0, 0, 0)),
        ],
        out_specs=pl.BlockSpec((1, t, HEADS_PER_GROUP * HEAD_DIM), lambda bi, g, i: (bi, i, g)),
        out_shape=jax.ShapeDtypeStruct((b, s, NSA_WIDTH), BF16),
        scratch_shapes=[pltpu.VMEM((HEADS_PER_GROUP, t, LANES), F32),
                        pltpu.VMEM((HEADS_PER_GROUP, t, LANES), F32),
                        pltpu.VMEM((HEADS_PER_GROUP, WIN_TILES + 1 if windowed else N_NEAR, t, t),
                                   F32)],
        compiler_params=_params("parallel", "parallel", "arbitrary"),
        name="win_attention" if windowed else "sel_attention",
    )(gates, q_aug, kt, v, gv)


def _toeplitz_generators(rel_bias):
    t = T_ATT
    d = np.arange(TOEPLITZ_ROWS)[:, None]
    y = np.arange(2 * t)[None, :]
    dist = np.where(y < t, d * t - y, d * t + 2 * t - y)
    tab = _bias_lookup(rel_bias, _BUCKETS[np.clip(dist, 0, None)]) - rel_bias[REL_BUCKETS - 1]
    tab = jnp.where(jnp.asarray(dist >= 0)[..., None], tab, 0.0)
    return jnp.transpose(tab, (2, 0, 1)).reshape(KV_GROUPS, HEADS_PER_GROUP, TOEPLITZ_ROWS, 2 * t)


def _pool_tile(x, halo, i, w, scale):
    t = x.shape[0]
    halo = jnp.where(i > 0, halo, 0.0)
    xs = jnp.concatenate([halo, x], axis=0)
    sums = []
    acc = xs
    span = 1
    for win in POOL_WINDOWS:
        while span < win:
            acc = acc + pltpu.roll(acc, span, 0)
            span *= 2
        sums.append(acc[HALO:])
    lane_grp = jnp.right_shift(lax.broadcasted_iota(jnp.int32, (t, POOL_WIDTH), 1), _log2(POOL_CH))
    tok = i * t + lax.broadcasted_iota(jnp.int32, (t, POOL_WIDTH), 0)
    ssum = sums[-1]
    win = jnp.full((t, POOL_WIDTH), POOL_WINDOWS[-1], jnp.int32)
    for gi in range(len(POOL_WINDOWS) - 2, -1, -1):
        ssum = jnp.where(lane_grp == gi, sums[gi], ssum)
        win = jnp.where(lane_grp == gi, POOL_WINDOWS[gi], win)
    cnt = jnp.minimum(tok + 1, win).astype(F32)
    dlt = (ssum / cnt - x).astype(BF16)
    return (_dot(dlt, w) * scale).astype(BF16)


def _sgu_tile(u, v, norm_g, norm_b, ws_cat, bs_exp):
    t = u.shape[0]
    v = _gelu(v)
    mu = jnp.mean(v, axis=-1, keepdims=True)
    var = jnp.mean(jnp.square(v - mu), axis=-1, keepdims=True)
    vn = (v - mu) * lax.rsqrt(var + EPS) * norm_g + norm_b
    lane_grp = jnp.right_shift(lax.broadcasted_iota(jnp.int32, (SGU_CHUNK, SGU_WIDTH), 1),
                               _log2(SGU_CH))
    r = lax.broadcasted_iota(jnp.int32, (SGU_CHUNK, SGU_GROUPS * SGU_CHUNK), 0)
    c = lax.broadcasted_iota(jnp.int32, (SGU_CHUNK, SGU_GROUPS * SGU_CHUNK), 1) & (SGU_CHUNK - 1)
    ws = jnp.where(c <= r, ws_cat, 0.0).astype(BF16)
    outs = []
    for n in range(t // SGU_CHUNK):
        vc = vn[n * SGU_CHUNK:(n + 1) * SGU_CHUNK]
        stacked = jnp.concatenate(
            [jnp.where(lane_grp == gi, vc, 0.0) for gi in range(SGU_GROUPS)], axis=0).astype(BF16)
        mixed = _dot(ws, stacked) + bs_exp
        outs.append((_gelu(u[n * SGU_CHUNK:(n + 1) * SGU_CHUNK]) * mixed).astype(BF16))
    return jnp.concatenate(outs, axis=0)


def _out_proj_kernel(oc_ref, os_ref, ow_ref, pin_ref, halo_ref, u_ref, v_ref, pw_ref, ps_ref,
                     ng_ref, nb_ref, ws_ref, bs_ref, w_ref, g_ref, x_ref, y_ref):
    a = (oc_ref[0].astype(F32) + os_ref[0].astype(F32) + ow_ref[0].astype(F32)).astype(BF16)
    pool = _pool_tile(pin_ref[0], halo_ref[0], pl.program_id(1), pw_ref[...], ps_ref[...])
    sgu = _sgu_tile(u_ref[0], v_ref[0], ng_ref[...], nb_ref[...], ws_ref[...], bs_ref[...])
    mix = (_dot(a, w_ref[:NSA_WIDTH, :])
           + _dot(pool, w_ref[NSA_WIDTH:NSA_WIDTH + POOL_WIDTH, :])
           + _dot(sgu, w_ref[NSA_WIDTH + POOL_WIDTH:, :]))
    y_ref[0] = x_ref[0] + _rms(mix, g_ref[...])


def _out_proj(o_cmp, o_sel, o_win, pin, u, v, pool_w, pool_scale, sgu_g, sgu_b, ws_cat, bs_exp,
              w, g, x, l):
    b, s, d = x.shape
    t = T_OUT
    assert t % SGU_CHUNK == 0 and t % HALO == 0
    row = lambda n: pl.BlockSpec((1, t, n), lambda bi, i: (bi, i, 0))
    halo = pl.BlockSpec((1, HALO, POOL_WIDTH),
                        lambda bi, i: (bi, jnp.maximum(i * (t // HALO) - 1, 0), 0))
    layer = [_layer_spec(a, l) for a in (pool_w, pool_scale, sgu_g, sgu_b, ws_cat, bs_exp, w, g)]
    return pl.pallas_call(
        _out_proj_kernel, grid=(b, s // t),
        in_specs=[row(NSA_WIDTH), row(NSA_WIDTH), row(NSA_WIDTH), row(POOL_WIDTH), halo,
                  row(SGU_WIDTH), row(SGU_WIDTH)] + layer + [row(d)],
        out_specs=row(d), out_shape=jax.ShapeDtypeStruct((b, s, d), F32),
        compiler_params=_params("parallel", "parallel"), name="out_proj",
    )(o_cmp, o_sel, o_win, pin, pin, u, v, pool_w, pool_scale, sgu_g, sgu_b, ws_cat, bs_exp, w, g, x)


def _mem_kv_kernel(mem_ref, g_ref, wk_ref, wv_ref, kt_ref, v_ref):
    m = _rms(mem_ref[0], g_ref[...]).astype(BF16)
    kt_ref[0] = _dot(m, wk_ref[...]).T.astype(BF16)
    v_ref[0] = _dot(m, wv_ref[...]).astype(BF16)


def _mem_kv(mem, g, wk, wv, l):
    b, m, d = mem.shape
    blk = pl.BlockSpec((1, m, d), lambda bi: (bi, 0, 0))
    blk_t = pl.BlockSpec((1, d, m), lambda bi: (bi, 0, 0))
    return pl.pallas_call(
        _mem_kv_kernel, grid=(b,),
        in_specs=[blk] + [_layer_spec(a, l) for a in (g, wk, wv)],
        out_specs=(blk_t, blk),
        out_shape=(jax.ShapeDtypeStruct((b, d, m), BF16), jax.ShapeDtypeStruct((b, m, d), BF16)),
        compiler_params=_params("parallel"), name="mem_kv",
    )(mem, g, wk, wv)


def _mem_attn_kernel(x_ref, gpre_ref, wq_ref, kt_ref, v_ref, wo_ref, gpost_ref, y_ref):
    x = x_ref[0]
    h = _rms(x, gpre_ref[...]).astype(BF16)
    q = (_dot(h, wq_ref[...]) * (MEM_HEAD_DIM ** -0.5)).astype(BF16)
    heads = range(MEM_HEADS)
    sl = [slice(hd * MEM_HEAD_DIM, (hd + 1) * MEM_HEAD_DIM) for hd in heads]
    s = [_dot(q[:, sl[hd]], kt_ref[0, sl[hd], :]) for hd in heads]
    e = [jnp.exp(s[hd] - jnp.max(s[hd], axis=-1, keepdims=True)) for hd in heads]
    p = [(e[hd] * (1.0 / jnp.sum(e[hd], axis=-1, keepdims=True))).astype(BF16) for hd in heads]
    o = jnp.concatenate([_dot(p[hd], v_ref[0, :, sl[hd]]).astype(BF16) for hd in heads], axis=1)
    y_ref[0] = x + _rms(_dot(o, wo_ref[...]), gpost_ref[...])


def _mem_attn(x, gpre, wq, kt, v, wo, gpost, l):
    b, s, d = x.shape
    m = v.shape[1]
    t = T_MEM
    row = pl.BlockSpec((1, t, d), lambda bi, i: (bi, i, 0))
    return pl.pallas_call(
        _mem_attn_kernel, grid=(b, s // t),
        in_specs=[row, _layer_spec(gpre, l), _layer_spec(wq, l),
                  pl.BlockSpec((1, d, m), lambda bi, i: (bi, 0, 0)),
                  pl.BlockSpec((1, m, d), lambda bi, i: (bi, 0, 0)),
                  _layer_spec(wo, l), _layer_spec(gpost, l)],
        out_specs=row, out_shape=jax.ShapeDtypeStruct((b, s, d), F32),
        compiler_params=_params("parallel", "parallel"), name="mem_attention",
    )(x, gpre, wq, kt, v, wo, gpost)


def _ffn_kernel(x_ref, halo_ref, gpre_ref, wup_ref, cw_ref, cb_ref, wd_ref, gpost_ref, y_ref,
                act_ref):
    i = pl.program_id(1)
    f = wd_ref.shape[0]
    halo = jnp.where(i > 0, halo_ref[0], 0.0)
    h = _rms(jnp.concatenate([halo, x_ref[0]], axis=0), gpre_ref[...]).astype(BF16)

    def conv(off):
        a = _dot(h, wup_ref[:, off:off + F_CHUNK])
        cw = cw_ref[:, off:off + F_CHUNK]
        return (cw[0:1] * pltpu.roll(a, 2, 0) + cw[1:2] * pltpu.roll(a, 1, 0) + cw[2:3] * a
                + cb_ref[:, off:off + F_CHUNK])

    for off in range(0, f, F_CHUNK):
        act = _gelu(conv(off)) * conv(f + off)
        act_ref[:, off:off + F_CHUNK] = act[HALO:].astype(BF16)
    y_ref[0] = x_ref[0] + _rms(_dot(act_ref[...], wd_ref[...]), gpost_ref[...])


def _ffn(x, gpre, w_up, conv_w, conv_b, w_down, gpost, l):
    b, s, d = x.shape
    f = w_down.shape[1]
    t = T_FFN
    assert f % F_CHUNK == 0
    resident = lambda a: _layer_spec(a, l, pipeline_mode=pl.Buffered(1))
    return pl.pallas_call(
        _ffn_kernel, grid=(b, s // t),
        in_specs=[
            pl.BlockSpec((1, t, d), lambda bi, i: (bi, i, 0)),
            pl.BlockSpec((1, HALO, d), lambda bi, i: (bi, jnp.maximum(i * (t // HALO) - 1, 0), 0)),
            resident(gpre), resident(w_up), resident(conv_w), resident(conv_b), resident(w_down),
            resident(gpost),
        ],
        out_specs=pl.BlockSpec((1, t, d), lambda bi, i: (bi, i, 0)),
        out_shape=jax.ShapeDtypeStruct((b, s, d), F32),
        scratch_shapes=[pltpu.VMEM((t, f), BF16)],
        compiler_params=_params("parallel", "parallel"), name="conv_ffn",
    )(x, x, gpre, w_up, conv_w, conv_b, w_down, gpost)


def kernel(x, mem, rel_bias, mix_norm_pre, mix_norm_post, w_in, cmp_pos, cmp_w1, cmp_b1, cmp_w2,
           cmp_b2, pool_w, pool_scale, sgu_norm_g, sgu_norm_b, sgu_w, sgu_b, w_out, mem_norm_pre,
           mem_norm_kv, mem_norm_post, w_mq, w_mk, w_mv, w_mo, ffn_norm_pre, ffn_norm_post, w_up,
           conv_w, conv_b, w_down):
    depth = w_in.shape[0]
    s = x.shape[1]
    assert s % max(T_IN, T_CMP, T_ATT, T_OUT, T_MEM, T_FFN) == 0
    nb = s // CMP_STRIDE

    w_in_p = _in_proj_weights(w_in)
    band, wagg = _cmp_tables(rel_bias, nb)
    toep = _toeplitz_generators(rel_bias)
    row = lambda a: a[:, None, :]
    bf = lambda a: a.astype(BF16)
    w_out_b, w_mq_b, w_mk_b, w_mv_b, w_mo_b = bf(w_out), bf(w_mq), bf(w_mk), bf(w_mv), bf(w_mo)
    w_up_b, w_down_b = bf(w_up), bf(w_down)
    g_mix_pre, g_mix_post = row(mix_norm_pre), row(mix_norm_post)
    g_mem_pre, g_mem_kv, g_mem_post = row(mem_norm_pre), row(mem_norm_kv), row(mem_norm_post)
    g_ffn_pre, g_ffn_post, conv_b_r = row(ffn_norm_pre), row(ffn_norm_post), row(conv_b)
    cw = jax.vmap(_compress_weights)(cmp_pos, cmp_w1, cmp_b1, cmp_w2, cmp_b2)
    n_pool = len(POOL_WINDOWS)
    pool_blk = bf(jnp.einsum("lgcd,gh->lgchd", pool_w, jnp.eye(n_pool, dtype=F32)).reshape(
        depth, POOL_WIDTH, POOL_WIDTH))
    pool_scale_r = row(pool_scale)
    sgu_g, sgu_bias = row(sgu_norm_g), row(sgu_norm_b)
    ws_cat = jnp.concatenate([sgu_w[:, gi] for gi in range(SGU_GROUPS)], axis=2)
    bs_exp = jnp.repeat(jnp.swapaxes(sgu_b, 1, 2), SGU_CH, axis=2)

    for l in range(depth):
        qa, ksel_t, kwin_t, kcmp, vcmp, vsel, vwin, gates, pin, u, v = _in_proj(
            x, g_mix_pre, w_in_p, l)
        kc, vct = _compress(kcmp, vcmp, cw, l)
        o_cmp, q_aug = _cmp_attention(gates, qa, kc, vct, band, wagg)
        o_sel = _flash(gates, q_aug, ksel_t, vsel, toep, windowed=False, branch=1)
        o_win = _flash(gates, qa, kwin_t, vwin, toep, windowed=True, branch=2)
        x = _out_proj(o_cmp, o_sel, o_win, pin, u, v, pool_blk, pool_scale_r, sgu_g, sgu_bias,
                      ws_cat, bs_exp, w_out_b, g_mix_post, x, l)
        km, vm = _mem_kv(mem, g_mem_kv, w_mk_b, w_mv_b, l)
        x = _mem_attn(x, g_mem_pre, w_mq_b, km, vm, w_mo_b, g_mem_post, l)
        x = _ffn(x, g_ffn_pre, w_up_b, conv_w, conv_b_r, w_down_b, g_ffn_post, l)
    return x
```

```python
import functools
import math

import numpy as np
import jax
import jax.numpy as jnp
from jax import lax
from jax.experimental import pallas as pl
from jax.experimental.pallas import tpu as pltpu

F32 = jnp.float32
BF16 = jnp.bfloat16

D_MODEL = 1024
NSA_WIDTH = 512
POOL_WIDTH = 256
SGU_WIDTH = 256
NSA_HEADS = 8
KV_GROUPS = 2
HEADS_PER_GROUP = 4
HEAD_DIM = 64
KV_WIDTH = 128
GATE_WIDTH = 24
CMP_STRIDE = 16
CMP_LEN = 32
CMP_HIDDEN = 128
SEL_BLOCK = 64
SEL_TOPK = 16
N_LOCAL_SEL = 2
SEL_AGG = (1.0, 2.0, 2.0, 2.0, 1.0)
FORCE_SCORE = 1e9
WINDOW = 512
REL_BUCKETS = 32
REL_MAX_DIST = 1024
POOL_WINDOWS = (2, 4, 8, 16)
POOL_CH = 64
SGU_GROUPS = 4
SGU_CH = 64
SGU_CHUNK = 128
MEM_HEADS = 4
MEM_HEAD_DIM = 256
FFN_HIDDEN = 2816
EPS = 1e-6

LANES = 128
SUBLANES = 8
BF16_ROWS = 16
VMEM_LIMIT = 56 * 1024 * 1024

T_IN = 512
T_CMP = 512
T_ATT = 512
T_OUT = 512
T_MEM = 512
T_FFN = 1024
F_CHUNK = 256
HALO = 16

MASK_BIAS = -float(2 ** 30)
NEG_INIT = -1e30

_Q_OFF = 0
_KCMP_OFF = NSA_WIDTH
_VCMP_OFF = _KCMP_OFF + KV_WIDTH
_KSEL_OFF = _VCMP_OFF + KV_WIDTH
_VSEL_OFF = _KSEL_OFF + KV_WIDTH
_KWIN_OFF = _VSEL_OFF + KV_WIDTH
_VWIN_OFF = _KWIN_OFF + KV_WIDTH
_GATE_OFF = _VWIN_OFF + KV_WIDTH
_POOL_OFF = 1536
_U_OFF = _POOL_OFF + POOL_WIDTH
_V_OFF = _U_OFF + SGU_WIDTH
IN_COLS = _V_OFF + SGU_WIDTH
MXU_COLS = 256


def _bucket_table(n_max):
    n = np.arange(n_max)
    max_exact = REL_BUCKETS // 2
    nf = np.maximum(n, 1).astype(np.float64)
    large = max_exact + (np.log(nf / max_exact) / math.log(REL_MAX_DIST / max_exact)
                         * (REL_BUCKETS - max_exact)).astype(np.int32)
    large = np.minimum(large, REL_BUCKETS - 1)
    return np.where(n < max_exact, n, large).astype(np.int32)


_BUCKETS = _bucket_table(16384)
FAR_DIST = int(np.argmax(_BUCKETS == REL_BUCKETS - 1))


def _bias_lookup(rel_bias, buckets):
    onehot = jnp.asarray(np.eye(REL_BUCKETS, dtype=np.float32)[buckets])
    return jnp.einsum("...k,kh->...h", onehot, rel_bias, precision=lax.Precision.HIGHEST)


def _in_proj_weights(w_in):
    depth, d, _ = w_in.shape
    split = _GATE_OFF + GATE_WIDTH
    assert _POOL_OFF >= split and _POOL_OFF % MXU_COLS == 0
    return jnp.concatenate(
        [w_in[:, :, :split].astype(BF16), jnp.zeros((depth, d, _POOL_OFF - split), BF16),
         w_in[:, :, split:].astype(BF16)], axis=2)


def _log2(n):
    assert n & (n - 1) == 0
    return n.bit_length() - 1


def _params(*sem):
    return pltpu.CompilerParams(dimension_semantics=sem, vmem_limit_bytes=VMEM_LIMIT)


def _layer_spec(a, l, **kw):
    return pl.BlockSpec((None,) + a.shape[1:], lambda *_: (l,) + (0,) * (a.ndim - 1), **kw)


def _rms(x, g):
    return x * lax.rsqrt(jnp.mean(x * x, axis=-1, keepdims=True) + EPS) * g


def _gelu(x):
    c = math.sqrt(2.0 / math.pi)
    return x * (0.5 * (1.0 + jnp.tanh(c * (x + 0.044715 * (x * x * x)))))


def _dot(a, b):
    return jnp.dot(a, b, preferred_element_type=F32)


def _dot_nt(a, b):
    return lax.dot_general(a, b, (((1,), (1,)), ((), ())), preferred_element_type=F32)


def _split3(x):
    hi = x.astype(BF16)
    r1 = x - hi.astype(F32)
    mid = r1.astype(BF16)
    lo = (r1 - mid.astype(F32)).astype(BF16)
    return hi, mid, lo


def _dot_f32_left(w_bf16, x):
    hi, mid, lo = _split3(x)
    return _dot(w_bf16, hi) + _dot(w_bf16, mid) + _dot(w_bf16, lo)


def _in_proj_kernel(x_ref, g_ref, w_ref, qa_ref, ksel_ref, kwin_ref, kcmp_ref, vcmp_ref,
                    vsel_ref, vwin_ref, gate_ref, pool_ref, u_ref, v_ref):
    t = x_ref.shape[1]
    h = _rms(x_ref[0], g_ref[...]).astype(BF16)

    def proj(off):
        assert off % MXU_COLS == 0
        return _dot(h, w_ref[:, off:off + MXU_COLS])

    zeros = jnp.zeros((t, HEAD_DIM), F32)

    def group_slab(piece, g):
        return jnp.concatenate([piece, zeros] if g == 0 else [zeros, piece], axis=1)

    scale = HEAD_DIM ** -0.5
    for g in range(KV_GROUPS):
        qg = proj(_Q_OFF + g * MXU_COLS) * scale
        for hg in range(HEADS_PER_GROUP):
            qa_ref[0, g * HEADS_PER_GROUP + hg] = group_slab(
                qg[:, hg * HEAD_DIM:(hg + 1) * HEAD_DIM], g).astype(BF16)

    lane = lax.broadcasted_iota(jnp.int32, (t, LANES), 1)
    kv_cmp = proj(_KCMP_OFF)
    kcmp_ref[0] = kv_cmp[:, :KV_WIDTH].astype(BF16)
    vcmp_ref[0] = kv_cmp[:, KV_WIDTH:].astype(BF16)
    tok = pl.program_id(1) * t + lax.broadcasted_iota(jnp.int32, (LANES, t), 1)
    blk_id = lax.broadcasted_iota(jnp.int32, (LANES, t), 0)
    ksel_ref[0, LANES:, :] = jnp.where(
        jnp.right_shift(tok, _log2(SEL_BLOCK)) == blk_id, 1.0, 0.0).astype(BF16)

    def keys_values(off, v_out):
        kv = proj(off)
        for g in range(KV_GROUPS):
            vg = group_slab(kv[:, KV_WIDTH + g * HEAD_DIM:KV_WIDTH + (g + 1) * HEAD_DIM], 0)
            v_out[0, g] = jnp.where(lane == HEAD_DIM, 1.0, vg).astype(BF16)
        return kv[:, :KV_WIDTH].T.astype(BF16)

    ksel_ref[0, :LANES, :] = keys_values(_KSEL_OFF, vsel_ref)
    kwin_ref[0] = keys_values(_KWIN_OFF, vwin_ref)
    gate_ref[0] = jax.nn.sigmoid(proj(_GATE_OFF)[:, :LANES])
    pool_ref[0] = proj(_POOL_OFF)
    u_ref[0] = proj(_U_OFF)
    v_ref[0] = proj(_V_OFF)


def _in_proj(x, g, w, l):
    b, s, d = x.shape
    t = T_IN
    grid = (b, s // t)
    row3 = lambda n: pl.BlockSpec((1, t, n), lambda bi, i: (bi, i, 0))
    out_shape = (
        jax.ShapeDtypeStruct((b, NSA_HEADS, s, LANES), BF16),
        jax.ShapeDtypeStruct((b, 2 * LANES, s), BF16),
        jax.ShapeDtypeStruct((b, LANES, s), BF16),
        jax.ShapeDtypeStruct((b, s, LANES), BF16),
        jax.ShapeDtypeStruct((b, s, LANES), BF16),
        jax.ShapeDtypeStruct((b, KV_GROUPS, s, LANES), BF16),
        jax.ShapeDtypeStruct((b, KV_GROUPS, s, LANES), BF16),
        jax.ShapeDtypeStruct((b, s, LANES), F32),
        jax.ShapeDtypeStruct((b, s, POOL_WIDTH), F32),
        jax.ShapeDtypeStruct((b, s, SGU_WIDTH), F32),
        jax.ShapeDtypeStruct((b, s, SGU_WIDTH), F32),
    )
    out_specs = (
        pl.BlockSpec((1, NSA_HEADS, t, LANES), lambda bi, i: (bi, 0, i, 0)),
        pl.BlockSpec((1, 2 * LANES, t), lambda bi, i: (bi, 0, i)),
        pl.BlockSpec((1, LANES, t), lambda bi, i: (bi, 0, i)),
        row3(LANES), row3(LANES),
        pl.BlockSpec((1, KV_GROUPS, t, LANES), lambda bi, i: (bi, 0, i, 0)),
        pl.BlockSpec((1, KV_GROUPS, t, LANES), lambda bi, i: (bi, 0, i, 0)),
        row3(LANES), row3(POOL_WIDTH), row3(SGU_WIDTH), row3(SGU_WIDTH),
    )
    return pl.pallas_call(
        _in_proj_kernel, grid=grid,
        in_specs=[row3(d), _layer_spec(g, l), _layer_spec(w, l)],
        out_specs=out_specs, out_shape=out_shape,
        compiler_params=_params("parallel", "parallel"), name="in_proj",
    )(x, g, w)


def _compress_kernel(rk_ref, rv_ref, w1a_ref, w1b_ref, pos_ref, b1_ref, w2k_ref, b2k_ref,
                     w2v_ref, b2v_ref, kc_ref, vct_ref):
    nb = rk_ref.shape[1]
    keep = lax.broadcasted_iota(jnp.int32, (nb, LANES), 0) < nb - 1

    def hidden(r, which):
        a = _dot(r, w1a_ref[which])
        bm = _dot(r, w1b_ref[which])
        pre = a + pltpu.roll(bm, nb - 1, 0)
        posb = (_dot(pos_ref[which, 0], w1a_ref[which])[0:1]
                + _dot(pos_ref[which, 1], w1b_ref[which])[0:1])
        return _gelu(pre + posb + b1_ref[which]).astype(BF16)

    kc = _dot(hidden(rk_ref[0], 0), w2k_ref[...]) + b2k_ref[...]
    kc_ref[0] = jnp.where(keep, kc, 0.0).astype(BF16)
    vc = _dot(hidden(rv_ref[0], 1), w2v_ref[...]) + b2v_ref[...]
    vct_ref[0] = jnp.where(keep, vc, 0.0).T.astype(BF16)


def _compress(kcmp, vcmp, cw, l):
    b, s, _ = kcmp.shape
    nb = s // CMP_STRIDE
    rk = kcmp.reshape(b, nb, CMP_STRIDE * KV_WIDTH)
    rv = vcmp.reshape(b, nb, CMP_STRIDE * KV_WIDTH)
    ws = (cw["w1a"], cw["w1b"], cw["pos"], cw["b1"], cw["w2k"], cw["b2k"], cw["w2v"], cw["b2v"])
    return pl.pallas_call(
        _compress_kernel, grid=(b,),
        in_specs=[pl.BlockSpec((1, nb, rk.shape[2]), lambda bi: (bi, 0, 0)),
                  pl.BlockSpec((1, nb, rv.shape[2]), lambda bi: (bi, 0, 0))]
                 + [_layer_spec(a, l) for a in ws],
        out_specs=(pl.BlockSpec((1, nb, KV_WIDTH), lambda bi: (bi, 0, 0)),
                   pl.BlockSpec((1, KV_WIDTH, nb), lambda bi: (bi, 0, 0))),
        out_shape=(jax.ShapeDtypeStruct((b, nb, KV_WIDTH), BF16),
                   jax.ShapeDtypeStruct((b, KV_WIDTH, nb), BF16)),
        compiler_params=_params("parallel"), name="compress",
    )(rk, rv, *ws)


def _compress_weights(pos, w1, b1, w2, b2):
    eye = jnp.eye(KV_GROUPS, dtype=F32)
    half = CMP_LEN // 2

    def w1_blocks(w):
        return jnp.einsum("ldh,pg->lpdgh", w, eye).reshape(half * KV_WIDTH, KV_GROUPS * CMP_HIDDEN)

    def pos_rows(p):
        r = jnp.broadcast_to(p[:, None, :], (half, KV_GROUPS, HEAD_DIM)).reshape(1, half * KV_WIDTH)
        return jnp.concatenate([r, jnp.zeros((SUBLANES - 1, half * KV_WIDTH), F32)], axis=0)

    w1a = jnp.stack([w1_blocks(w1[i, :half]) for i in range(2)]).astype(BF16)
    w1b = jnp.stack([w1_blocks(w1[i, half:]) for i in range(2)]).astype(BF16)
    posr = jnp.stack([jnp.stack([pos_rows(pos[i, :half]), pos_rows(pos[i, half:])])
                      for i in range(2)]).astype(BF16)
    b1t = jnp.tile(b1, (1, KV_GROUPS))[:, None, :]
    w2blk = lambda w: jnp.einsum("hd,pg->phgd", w, eye).reshape(KV_GROUPS * CMP_HIDDEN, KV_WIDTH)
    return dict(w1a=w1a, w1b=w1b, pos=posr, b1=b1t,
                w2k=w2blk(w2[0]).astype(BF16), b2k=jnp.tile(b2[0], KV_GROUPS)[None, :],
                w2v=w2blk(w2[1]).astype(BF16), b2v=jnp.tile(b2[1], KV_GROUPS)[None, :])


CMP_PAD = 64
CMP_BAND_BACK = 56
CMP_BAND = CMP_BAND_BACK + T_CMP // CMP_STRIDE
CMP_BAND_DIST0 = CMP_BAND_BACK * CMP_STRIDE - (CMP_LEN - 1)
assert CMP_BAND_DIST0 + CMP_STRIDE >= FAR_DIST and CMP_BAND_BACK <= CMP_PAD
CMP_ROW_CHUNK = 128
SEL_RATIO = SEL_BLOCK // CMP_STRIDE


def _gate_column(head, branch):
    return head * 3 + branch


def _cmp_kernel(gate_ref, qa_ref, kc_ref, vct_ref, band_ref, wagg_ref, ocmp_ref, qaug_ref,
                st_ref, gt_ref):
    tq = qa_ref.shape[2]
    nb = kc_ref.shape[1]
    i = pl.program_id(1)
    g = pl.program_id(0)
    s0 = i * tq
    t = s0 + lax.broadcasted_iota(jnp.int32, (1, tq), 1)
    n_live = (i + 1) * (tq // CMP_STRIDE)
    band_row0 = pl.multiple_of(s0 // CMP_STRIDE - CMP_BAND_BACK + CMP_PAD, SUBLANES)
    gt_ref[...] = gate_ref[0].T
    st_ref[:, :CMP_PAD, :] = jnp.zeros((HEADS_PER_GROUP, CMP_PAD, tq), F32)
    has_block = t >= CMP_LEN - 1

    def body(rows):
        nsr = rows // SEL_RATIO
        kc = kc_ref[0, :rows, :]
        tail = rows - CMP_ROW_CHUNK
        dead = tail + lax.broadcasted_iota(jnp.int32, (CMP_ROW_CHUNK, 1), 0) >= n_live
        imp = jnp.zeros((rows, tq), F32)
        outs = []
        for h in range(HEADS_PER_GROUP):
            st_ref[h, CMP_PAD:CMP_PAD + rows, :] = _dot_nt(kc, qa_ref[0, h])
        for h in range(HEADS_PER_GROUP):
            st_ref[h, pl.ds(band_row0, CMP_BAND), :] += band_ref[h]
        for h in range(HEADS_PER_GROUP):
            last = jnp.where(dead, MASK_BIAS, st_ref[h, CMP_PAD + tail:CMP_PAD + rows, :])
            s = last if tail == 0 else jnp.concatenate(
                [st_ref[h, CMP_PAD:CMP_PAD + tail, :], last], axis=0)
            e = jnp.exp(s - jnp.max(s, axis=0, keepdims=True))
            denom = jnp.maximum(jnp.sum(e, axis=0, keepdims=True), 1.0)
            rinv = jnp.where(has_block, 1.0 / denom, 0.0)
            imp = imp + e * rinv
            gate = gt_ref[pl.ds(_gate_column(g * HEADS_PER_GROUP + h, 0), 1), :]
            outs.append((gate * rinv) * _dot(vct_ref[0, :, :rows], e.astype(BF16)))
        ocmp_ref[0] = jnp.concatenate(outs, axis=0).T.astype(BF16)

        p_sel = _dot_f32_left(wagg_ref[:nsr, :rows], imp)
        j = lax.broadcasted_iota(jnp.int32, (nsr, 1), 0)
        rel = jnp.right_shift(t, _log2(SEL_BLOCK)) - j
        forced = (j == 0) | ((rel >= 0) & (rel < N_LOCAL_SEL))
        score = jnp.where(forced, FORCE_SCORE, jnp.where(j * SEL_BLOCK <= t, p_sel, -1.0))
        jfull = jnp.broadcast_to(j, (nsr, tq)).astype(F32)
        sel = jnp.zeros((nsr, tq), F32)
        for _ in range(SEL_TOPK):
            m = jnp.max(score, axis=0, keepdims=True)
            first = jnp.min(jnp.where(score == m, jfull, float(nsr)), axis=0, keepdims=True)
            hit = jfull == first
            sel = jnp.where(hit, 1.0, sel)
            score = jnp.where(hit, -3e38, score)
        notsel = jnp.where(sel > 0.5, 0.0, MASK_BIAS)
        if nsr < LANES:
            notsel = jnp.concatenate(
                [notsel, jnp.full((LANES - nsr, tq), MASK_BIAS, F32)], axis=0)
        notsel = notsel.T.astype(BF16)
        for h in range(HEADS_PER_GROUP):
            qaug_ref[0, h, :, :LANES] = qa_ref[0, h]
            qaug_ref[0, h, :, LANES:] = notsel

    for rows in range(CMP_ROW_CHUNK, nb + 1, CMP_ROW_CHUNK):
        @pl.when((n_live > rows - CMP_ROW_CHUNK) & (n_live <= rows))
        def _():
            body(rows)


def _cmp_attention(gates, qa, kc, vct, band, wagg):
    b, _, s, _ = qa.shape
    nb = kc.shape[1]
    tq = T_CMP
    assert nb % CMP_ROW_CHUNK == 0
    grid = (KV_GROUPS, s // tq, b)
    return pl.pallas_call(
        _cmp_kernel, grid=grid,
        in_specs=[
            pl.BlockSpec((1, tq, LANES), lambda g, i, bi: (bi, i, 0)),
            pl.BlockSpec((1, HEADS_PER_GROUP, tq, LANES), lambda g, i, bi: (bi, g, i, 0)),
            pl.BlockSpec((1, nb, KV_WIDTH), lambda g, i, bi: (bi, 0, 0)),
            pl.BlockSpec((1, HEAD_DIM, nb), lambda g, i, bi: (bi, g, 0)),
            pl.BlockSpec((HEADS_PER_GROUP, CMP_BAND, tq), lambda g, i, bi: (g, 0, 0)),
            pl.BlockSpec(wagg.shape, lambda g, i, bi: (0, 0)),
        ],
        out_specs=(
            pl.BlockSpec((1, tq, HEADS_PER_GROUP * HEAD_DIM), lambda g, i, bi: (bi, i, g)),
            pl.BlockSpec((1, HEADS_PER_GROUP, tq, 2 * LANES), lambda g, i, bi: (bi, g, i, 0)),
        ),
        out_shape=(jax.ShapeDtypeStruct((b, s, NSA_WIDTH), BF16),
                   jax.ShapeDtypeStruct((b, NSA_HEADS, s, 2 * LANES), BF16)),
        scratch_shapes=[pltpu.VMEM((HEADS_PER_GROUP, CMP_PAD + nb, tq), F32),
                        pltpu.VMEM((LANES, tq), F32)],
        compiler_params=_params("parallel", "parallel", "parallel"), name="cmp_attention",
    )(gates, qa, kc, vct, band, wagg)


def _cmp_tables(rel_bias, nb):
    jj = np.arange(CMP_BAND)[:, None]
    ii = np.arange(T_CMP)[None, :]
    dist = CMP_BAND_DIST0 + ii - CMP_STRIDE * jj
    bidx = _BUCKETS[np.clip(dist, 0, None)]
    band = jnp.transpose(_bias_lookup(rel_bias, bidx) - rel_bias[REL_BUCKETS - 1], (2, 0, 1))
    band = jnp.where(jnp.asarray(dist >= 0)[None], band, MASK_BIAS)
    ns = nb // SEL_RATIO
    assert SEL_TOPK <= ns <= LANES
    wagg = np.zeros((LANES, nb), np.float32)
    for m, w in enumerate(SEL_AGG):
        for jb in range(ns):
            c = SEL_RATIO * jb + m - 1
            if 0 <= c < nb - 1:
                wagg[jb, c] = w
    return band, jnp.asarray(wagg, BF16)


N_NEAR = (FAR_DIST + T_ATT - 1) // T_ATT + 1
assert N_NEAR * T_ATT - (T_ATT - 1) >= FAR_DIST
WIN_TILES = (WINDOW - 1 + T_ATT - 1) // T_ATT
TOEPLITZ_ROWS = 8
assert max(N_NEAR, WIN_TILES + 1) <= TOEPLITZ_ROWS
FAR_KEYS = 4 * T_ATT


def _flash_kernel(gate_ref, q_ref, kt_ref, v_ref, gv_ref, o_ref, m_ref, acc_ref, nb_ref, *,
                  windowed, branch):
    t = T_ATT
    g = pl.program_id(1)
    i = pl.program_id(2)
    n_tab = WIN_TILES + 1 if windowed else N_NEAR

    @pl.when(i == 0)
    def _():
        for h in range(HEADS_PER_GROUP):
            for d in range(n_tab):
                gen = jnp.broadcast_to(gv_ref[0, h, d:d + 1, :], (t, 2 * t))
                nb_ref[h, d] = pltpu.roll(gen, 0, 1, stride=1, stride_axis=0)[:, :t]

    m_ref[...] = jnp.full(m_ref.shape, NEG_INIT, F32)
    acc_ref[...] = jnp.zeros(acc_ref.shape, F32)

    def tile_step(first_tile, n_tiles, dists=()):
        heads = range(HEADS_PER_GROUP)
        assert not dists or len(dists) == n_tiles
        nk = t * n_tiles
        start = pl.multiple_of(first_tile * t, t)
        kt = kt_ref[0, :, pl.ds(start, nk)]
        v = v_ref[0, 0, pl.ds(start, nk), :]
        s = [_dot(q_ref[0, h], kt) for h in heads]
        if dists:
            s = [s[h] + jnp.concatenate([nb_ref[h, d] for d in dists], axis=1) for h in heads]
            dist = (dists[0] * t + lax.broadcasted_iota(jnp.int32, (t, nk), 0)
                    - lax.broadcasted_iota(jnp.int32, (t, nk), 1))
            causal = dists[-1] == 0
            banded = windowed and dists[0] * t + t - 1 >= WINDOW
            keep = None
            if causal and banded:
                keep = lax.bitcast_convert_type(dist, jnp.uint32) < WINDOW
            elif causal:
                keep = dist >= 0
            elif banded:
                keep = dist < WINDOW
            if keep is not None:
                s = [jnp.where(keep, s[h], MASK_BIAS) for h in heads]
        m_prev = [m_ref[h] for h in heads]
        m_new = [jnp.maximum(m_prev[h], jnp.max(s[h], axis=1, keepdims=True)) for h in heads]
        p = [jnp.exp(s[h] - jnp.tile(m_new[h], (1, nk // LANES))).astype(BF16) for h in heads]
        pv = [_dot(p[h], v) for h in heads]
        for h in heads:
            acc_ref[h] = jnp.exp(m_prev[h] - m_new[h]) * acc_ref[h] + pv[h]
            m_ref[h] = m_new[h]

    def near_steps(first):
        @pl.when(i >= first)
        def _():
            tile_step(i - first, first + 1, tuple(range(first, -1, -1)))

        @pl.when(i < first)
        def _():
            for d in range(first - 1, -1, -1):
                @pl.when(i >= d)
                def _():
                    tile_step(i - d, 1, (d,))

    if windowed:
        near_steps(WIN_TILES)
    else:
        n_far = jnp.maximum(i - (N_NEAR - 1), 0)
        far_tiles = FAR_KEYS // t

        def far_body(j, carry):
            tile_step(j * far_tiles, far_tiles)
            return carry
        n_far_steps = jnp.right_shift(n_far, _log2(far_tiles))
        lax.fori_loop(0, n_far_steps, far_body, 0)
        done = n_far_steps * far_tiles
        run = far_tiles // 2
        while run >= 1:
            has_run = jnp.bitwise_and(n_far, run) != 0

            @pl.when(has_run)
            def _():
                tile_step(done, run)
            done = done + jnp.where(has_run, run, 0)
            run //= 2
        near_steps(N_NEAR - 1)

    lane = lax.broadcasted_iota(jnp.int32, (t, LANES), 1)
    gates = gate_ref[0]
    outs = []
    for h in range(HEADS_PER_GROUP):
        col = _gate_column(g * HEADS_PER_GROUP + h, branch)
        gate = jnp.sum(jnp.where(lane == col, gates, 0.0), axis=1, keepdims=True)
        acc = acc_ref[h]
        outs.append(acc[:, :HEAD_DIM] / jnp.maximum(acc[:, HEAD_DIM:HEAD_DIM + 1], 1.0) * gate)
    o_ref[0] = jnp.concatenate(outs, axis=1).astype(BF16)


def _flash(gates, q_aug, kt, v, gv, *, windowed, branch):
    b, _, s, kd = q_aug.shape
    assert kt.shape[1] == kd
    t = T_ATT
    grid = (b, KV_GROUPS, s // t)
    return pl.pallas_call(
        functools.partial(_flash_kernel, windowed=windowed, branch=branch), grid=grid,
        in_specs=[
            pl.BlockSpec((1, t, LANES), lambda bi, g, i: (bi, i, 0)),
            pl.BlockSpec((1, HEADS_PER_GROUP, t, kd), lambda bi, g, i: (bi, g, i, 0)),
            pl.BlockSpec((1, kd, s), lambda bi, g, i: (bi, 0, 0)),
            pl.BlockSpec((1, 1, s, LANES), lambda bi, g, i: (bi, g, 0, 0)),
            pl.BlockSpec((1, HEADS_PER_GROUP, TOEPLITZ_ROWS, 2 * t), lambda bi, g, i: (g, 0, 0, 0)),
        ],
        out_specs=pl.BlockSpec((1, t, HEADS_PER_GROUP * HEAD_DIM), lambda bi, g, i: (bi, i, g)),
        out_shape=jax.ShapeDtypeStruct((b, s, NSA_WIDTH), BF16),
        scratch_shapes=[pltpu.VMEM((HEADS_PER_GROUP, t, LANES), F32),
                        pltpu.VMEM((HEADS_PER_GROUP, t, LANES), F32),
                        pltpu.VMEM((HEADS_PER_GROUP, WIN_TILES + 1 if windowed else N_NEAR, t, t),
                                   F32)],
        compiler_params=_params("parallel", "parallel", "arbitrary"),
        name="win_attention" if windowed else "sel_attention",
    )(gates, q_aug, kt, v, gv)


def _toeplitz_generators(rel_bias):
    t = T_ATT
    d = np.arange(TOEPLITZ_ROWS)[:, None]
    y = np.arange(2 * t)[None, :]
    dist = np.where(y < t, d * t - y, d * t + 2 * t - y)
    tab = _bias_lookup(rel_bias, _BUCKETS[np.clip(dist, 0, None)]) - rel_bias[REL_BUCKETS - 1]
    tab = jnp.where(jnp.asarray(dist >= 0)[..., None], tab, 0.0)
    return jnp.transpose(tab, (2, 0, 1)).reshape(KV_GROUPS, HEADS_PER_GROUP, TOEPLITZ_ROWS, 2 * t)


def _pool_tile(x, halo, i, w, scale):
    t = x.shape[0]
    halo = jnp.where(i > 0, halo, 0.0)
    xs = jnp.concatenate([halo, x], axis=0)
    sums = []
    acc = xs
    span = 1
    for win in POOL_WINDOWS:
        while span < win:
            acc = acc + pltpu.roll(acc, span, 0)
            span *= 2
        sums.append(acc[HALO:])
    lane_grp = jnp.right_shift(lax.broadcasted_iota(jnp.int32, (t, POOL_WIDTH), 1), _log2(POOL_CH))
    tok = i * t + lax.broadcasted_iota(jnp.int32, (t, POOL_WIDTH), 0)
    ssum = sums[-1]
    win = jnp.full((t, POOL_WIDTH), POOL_WINDOWS[-1], jnp.int32)
    for gi in range(len(POOL_WINDOWS) - 2, -1, -1):
        ssum = jnp.where(lane_grp == gi, sums[gi], ssum)
        win = jnp.where(lane_grp == gi, POOL_WINDOWS[gi], win)
    cnt = jnp.minimum(tok + 1, win).astype(F32)
    dlt = (ssum / cnt - x).astype(BF16)
    return (_dot(dlt, w) * scale).astype(BF16)


def _sgu_tile(u, v, norm_g, norm_b, ws_cat, bs_exp):
    t = u.shape[0]
    v = _gelu(v)
    mu = jnp.mean(v, axis=-1, keepdims=True)
    var = jnp.mean(jnp.square(v - mu), axis=-1, keepdims=True)
    vn = (v - mu) * lax.rsqrt(var + EPS) * norm_g + norm_b
    lane_grp = jnp.right_shift(lax.broadcasted_iota(jnp.int32, (SGU_CHUNK, SGU_WIDTH), 1),
                               _log2(SGU_CH))
    r = lax.broadcasted_iota(jnp.int32, (SGU_CHUNK, SGU_GROUPS * SGU_CHUNK), 0)
    c = lax.broadcasted_iota(jnp.int32, (SGU_CHUNK, SGU_GROUPS * SGU_CHUNK), 1) & (SGU_CHUNK - 1)
    ws = jnp.where(c <= r, ws_cat, 0.0).astype(BF16)
    outs = []
    for n in range(t // SGU_CHUNK):
        vc = vn[n * SGU_CHUNK:(n + 1) * SGU_CHUNK]
        stacked = jnp.concatenate(
            [jnp.where(lane_grp == gi, vc, 0.0) for gi in range(SGU_GROUPS)], axis=0).astype(BF16)
        mixed = _dot(ws, stacked) + bs_exp
        outs.append((_gelu(u[n * SGU_CHUNK:(n + 1) * SGU_CHUNK]) * mixed).astype(BF16))
    return jnp.concatenate(outs, axis=0)


def _out_proj_kernel(oc_ref, os_ref, ow_ref, pin_ref, halo_ref, u_ref, v_ref, pw_ref, ps_ref,
                     ng_ref, nb_ref, ws_ref, bs_ref, w_ref, g_ref, x_ref, y_ref):
    a = (oc_ref[0].astype(F32) + os_ref[0].astype(F32) + ow_ref[0].astype(F32)).astype(BF16)
    pool = _pool_tile(pin_ref[0], halo_ref[0], pl.program_id(1), pw_ref[...], ps_ref[...])
    sgu = _sgu_tile(u_ref[0], v_ref[0], ng_ref[...], nb_ref[...], ws_ref[...], bs_ref[...])
    mix = (_dot(a, w_ref[:NSA_WIDTH, :])
           + _dot(pool, w_ref[NSA_WIDTH:NSA_WIDTH + POOL_WIDTH, :])
           + _dot(sgu, w_ref[NSA_WIDTH + POOL_WIDTH:, :]))
    y_ref[0] = x_ref[0] + _rms(mix, g_ref[...])


def _out_proj(o_cmp, o_sel, o_win, pin, u, v, pool_w, pool_scale, sgu_g, sgu_b, ws_cat, bs_exp,
              w, g, x, l):
    b, s, d = x.shape
    t = T_OUT
    assert t % SGU_CHUNK == 0 and t % HALO == 0
    row = lambda n: pl.BlockSpec((1, t, n), lambda bi, i: (bi, i, 0))
    halo = pl.BlockSpec((1, HALO, POOL_WIDTH),
                        lambda bi, i: (bi, jnp.maximum(i * (t // HALO) - 1, 0), 0))
    layer = [_layer_spec(a, l) for a in (pool_w, pool_scale, sgu_g, sgu_b, ws_cat, bs_exp, w, g)]
    return pl.pallas_call(
        _out_proj_kernel, grid=(b, s // t),
        in_specs=[row(NSA_WIDTH), row(NSA_WIDTH), row(NSA_WIDTH), row(POOL_WIDTH), halo,
                  row(SGU_WIDTH), row(SGU_WIDTH)] + layer + [row(d)],
        out_specs=row(d), out_shape=jax.ShapeDtypeStruct((b, s, d), F32),
        compiler_params=_params("parallel", "parallel"), name="out_proj",
    )(o_cmp, o_sel, o_win, pin, pin, u, v, pool_w, pool_scale, sgu_g, sgu_b, ws_cat, bs_exp, w, g, x)


def _mem_kv_kernel(mem_ref, g_ref, wk_ref, wv_ref, kt_ref, v_ref):
    m = _rms(mem_ref[0], g_ref[...]).astype(BF16)
    kt_ref[0] = _dot(m, wk_ref[...]).T.astype(BF16)
    v_ref[0] = _dot(m, wv_ref[...]).astype(BF16)


def _mem_kv(mem, g, wk, wv, l):
    b, m, d = mem.shape
    blk = pl.BlockSpec((1, m, d), lambda bi: (bi, 0, 0))
    blk_t = pl.BlockSpec((1, d, m), lambda bi: (bi, 0, 0))
    return pl.pallas_call(
        _mem_kv_kernel, grid=(b,),
        in_specs=[blk] + [_layer_spec(a, l) for a in (g, wk, wv)],
        out_specs=(blk_t, blk),
        out_shape=(jax.ShapeDtypeStruct((b, d, m), BF16), jax.ShapeDtypeStruct((b, m, d), BF16)),
        compiler_params=_params("parallel"), name="mem_kv",
    )(mem, g, wk, wv)


def _mem_attn_kernel(x_ref, gpre_ref, wq_ref, kt_ref, v_ref, wo_ref, gpost_ref, y_ref):
    x = x_ref[0]
    h = _rms(x, gpre_ref[...]).astype(BF16)
    q = (_dot(h, wq_ref[...]) * (MEM_HEAD_DIM ** -0.5)).astype(BF16)
    heads = range(MEM_HEADS)
    sl = [slice(hd * MEM_HEAD_DIM, (hd + 1) * MEM_HEAD_DIM) for hd in heads]
    s = [_dot(q[:, sl[hd]], kt_ref[0, sl[hd], :]) for hd in heads]
    e = [jnp.exp(s[hd] - jnp.max(s[hd], axis=-1, keepdims=True)) for hd in heads]
    p = [(e[hd] * (1.0 / jnp.sum(e[hd], axis=-1, keepdims=True))).astype(BF16) for hd in heads]
    o = jnp.concatenate([_dot(p[hd], v_ref[0, :, sl[hd]]).astype(BF16) for hd in heads], axis=1)
    y_ref[0] = x + _rms(_dot(o, wo_ref[...]), gpost_ref[...])


def _mem_attn(x, gpre, wq, kt, v, wo, gpost, l):
    b, s, d = x.shape
    m = v.shape[1]
    t = T_MEM
    row = pl.BlockSpec((1, t, d), lambda bi, i: (bi, i, 0))
    return pl.pallas_call(
        _mem_attn_kernel, grid=(b, s // t),
        in_specs=[row, _layer_spec(gpre, l), _layer_spec(wq, l),
                  pl.BlockSpec((1, d, m), lambda bi, i: (bi, 0, 0)),
                  pl.BlockSpec((1, m, d), lambda bi, i: (bi, 0, 0)),
                  _layer_spec(wo, l), _layer_spec(gpost, l)],
        out_specs=row, out_shape=jax.ShapeDtypeStruct((b, s, d), F32),
        compiler_params=_params("parallel", "parallel"), name="mem_attention",
    )(x, gpre, wq, kt, v, wo, gpost)


def _ffn_kernel(x_ref, halo_ref, gpre_ref, wup_ref, cw_ref, cb_ref, wd_ref, gpost_ref, y_ref,
                act_ref):
    i = pl.program_id(1)
    f = wd_ref.shape[0]
    halo = jnp.where(i > 0, halo_ref[0], 0.0)
    h = _rms(jnp.concatenate([halo, x_ref[0]], axis=0), gpre_ref[...]).astype(BF16)

    def conv(off):
        a = _dot(h, wup_ref[:, off:off + F_CHUNK])
        cw = cw_ref[:, off:off + F_CHUNK]
        return (cw[0:1] * pltpu.roll(a, 2, 0) + cw[1:2] * pltpu.roll(a, 1, 0) + cw[2:3] * a
                + cb_ref[:, off:off + F_CHUNK])

    for off in range(0, f, F_CHUNK):
        act = _gelu(conv(off)) * conv(f + off)
        act_ref[:, off:off + F_CHUNK] = act[HALO:].astype(BF16)
    y_ref[0] = x_ref[0] + _rms(_dot(act_ref[...], wd_ref[...]), gpost_ref[...])


def _ffn(x, gpre, w_up, conv_w, conv_b, w_down, gpost, l):
    b, s, d = x.shape
    f = w_down.shape[1]
    t = T_FFN
    assert f % F_CHUNK == 0
    resident = lambda a: _layer_spec(a, l, pipeline_mode=pl.Buffered(1))
    return pl.pallas_call(
        _ffn_kernel, grid=(b, s // t),
        in_specs=[
            pl.BlockSpec((1, t, d), lambda bi, i: (bi, i, 0)),
            pl.BlockSpec((1, HALO, d), lambda bi, i: (bi, jnp.maximum(i * (t // HALO) - 1, 0), 0)),
            resident(gpre), resident(w_up), resident(conv_w), resident(conv_b), resident(w_down),
            resident(gpost),
        ],
        out_specs=pl.BlockSpec((1, t, d), lambda bi, i: (bi, i, 0)),
        out_shape=jax.ShapeDtypeStruct((b, s, d), F32),
        scratch_shapes=[pltpu.VMEM((t, f), BF16)],
        compiler_params=_params("parallel", "parallel"), name="conv_ffn",
    )(x, x, gpre, w_up, conv_w, conv_b, w_down, gpost)


def kernel(x, mem, rel_bias, mix_norm_pre, mix_norm_post, w_in, cmp_pos, cmp_w1, cmp_b1, cmp_w2,
           cmp_b2, pool_w, pool_scale, sgu_norm_g, sgu_norm_b, sgu_w, sgu_b, w_out, mem_norm_pre,
           mem_norm_kv, mem_norm_post, w_mq, w_mk, w_mv, w_mo, ffn_norm_pre, ffn_norm_post, w_up,
           conv_w, conv_b, w_down):
    depth = w_in.shape[0]
    s = x.shape[1]
    assert s % max(T_IN, T_CMP, T_ATT, T_OUT, T_MEM, T_FFN) == 0
    nb = s // CMP_STRIDE

    w_in_p = _in_proj_weights(w_in)
    band, wagg = _cmp_tables(rel_bias, nb)
    toep = _toeplitz_generators(rel_bias)
    row = lambda a: a[:, None, :]
    bf = lambda a: a.astype(BF16)
    w_out_b, w_mq_b, w_mk_b, w_mv_b, w_mo_b = bf(w_out), bf(w_mq), bf(w_mk), bf(w_mv), bf(w_mo)
    w_up_b, w_down_b = bf(w_up), bf(w_down)
    g_mix_pre, g_mix_post = row(mix_norm_pre), row(mix_norm_post)
    g_mem_pre, g_mem_kv, g_mem_post = row(mem_norm_pre), row(mem_norm_kv), row(mem_norm_post)
    g_ffn_pre, g_ffn_post, conv_b_r = row(ffn_norm_pre), row(ffn_norm_post), row(conv_b)
    cw = jax.vmap(_compress_weights)(cmp_pos, cmp_w1, cmp_b1, cmp_w2, cmp_b2)
    n_pool = len(POOL_WINDOWS)
    pool_blk = bf(jnp.einsum("lgcd,gh->lgchd", pool_w, jnp.eye(n_pool, dtype=F32)).reshape(
        depth, POOL_WIDTH, POOL_WIDTH))
    pool_scale_r = row(pool_scale)
    sgu_g, sgu_bias = row(sgu_norm_g), row(sgu_norm_b)
    ws_cat = jnp.concatenate([sgu_w[:, gi] for gi in range(SGU_GROUPS)], axis=2)
    bs_exp = jnp.repeat(jnp.swapaxes(sgu_b, 1, 2), SGU_CH, axis=2)

    for l in range(depth):
        qa, ksel_t, kwin_t, kcmp, vcmp, vsel, vwin, gates, pin, u, v = _in_proj(
            x, g_mix_pre, w_in_p, l)
        kc, vct = _compress(kcmp, vcmp, cw, l)
        o_cmp, q_aug = _cmp_attention(gates, qa, kc, vct, band, wagg)
        o_sel = _flash(gates, q_aug, ksel_t, vsel, toep, windowed=False, branch=1)
        o_win = _flash(gates, qa, kwin_t, vwin, toep, windowed=True, branch=2)
        x = _out_proj(o_cmp, o_sel, o_win, pin, u, v, pool_blk, pool_scale_r, sgu_g, sgu_bias,
                      ws_cat, bs_exp, w_out_b, g_mix_post, x, l)
        km, vm = _mem_kv(mem, g_mem_kv, w_mk_b, w_mv_b, l)
        x = _mem_attn(x, g_mem_pre, w_mq_b, km, vm, w_mo_b, g_mem_post, l)
        x = _ffn(x, g_ffn_pre, w_up_b, conv_w, conv_b_r, w_down_b, g_ffn_post, l)
    return x
```

```python
import functools
import math

import numpy as np
import jax
import jax.numpy as jnp
from jax import lax
from jax.experimental import pallas as pl
from jax.experimental.pallas import tpu as pltpu

F32 = jnp.float32
BF16 = jnp.bfloat16

D_MODEL = 1024
NSA_WIDTH = 512
POOL_WIDTH = 256
SGU_WIDTH = 256
NSA_HEADS = 8
KV_GROUPS = 2
HEADS_PER_GROUP = 4
HEAD_DIM = 64
KV_WIDTH = 128
GATE_WIDTH = 24
CMP_STRIDE = 16
CMP_LEN = 32
CMP_HIDDEN = 128
SEL_BLOCK = 64
SEL_TOPK = 16
N_LOCAL_SEL = 2
SEL_AGG = (1.0, 2.0, 2.0, 2.0, 1.0)
FORCE_SCORE = 1e9
WINDOW = 512
REL_BUCKETS = 32
REL_MAX_DIST = 1024
POOL_WINDOWS = (2, 4, 8, 16)
POOL_CH = 64
SGU_GROUPS = 4
SGU_CH = 64
SGU_CHUNK = 128
MEM_HEADS = 4
MEM_HEAD_DIM = 256
FFN_HIDDEN = 2816
EPS = 1e-6

LANES = 128
SUBLANES = 8
BF16_ROWS = 16
VMEM_LIMIT = 56 * 1024 * 1024

T_IN = 512
T_CMP = 512
T_ATT = 512
T_OUT = 512
T_MEM = 512
T_FFN = 1024
F_CHUNK = 256
HALO = 16

MASK_BIAS = -float(2 ** 30)
NEG_INIT = -1e30

_Q_OFF = 0
_KCMP_OFF = NSA_WIDTH
_VCMP_OFF = _KCMP_OFF + KV_WIDTH
_KSEL_OFF = _VCMP_OFF + KV_WIDTH
_VSEL_OFF = _KSEL_OFF + KV_WIDTH
_KWIN_OFF = _VSEL_OFF + KV_WIDTH
_VWIN_OFF = _KWIN_OFF + KV_WIDTH
_GATE_OFF = _VWIN_OFF + KV_WIDTH
_POOL_OFF = 1536
_U_OFF = _POOL_OFF + POOL_WIDTH
_V_OFF = _U_OFF + SGU_WIDTH
IN_COLS = _V_OFF + SGU_WIDTH
MXU_COLS = 256


def _bucket_table(n_max):
    n = np.arange(n_max)
    max_exact = REL_BUCKETS // 2
    nf = np.maximum(n, 1).astype(np.float64)
    large = max_exact + (np.log(nf / max_exact) / math.log(REL_MAX_DIST / max_exact)
                         * (REL_BUCKETS - max_exact)).astype(np.int32)
    large = np.minimum(large, REL_BUCKETS - 1)
    return np.where(n < max_exact, n, large).astype(np.int32)


_BUCKETS = _bucket_table(16384)
FAR_DIST = int(np.argmax(_BUCKETS == REL_BUCKETS - 1))


def _bias_lookup(rel_bias, buckets):
    onehot = jnp.asarray(np.eye(REL_BUCKETS, dtype=np.float32)[buckets])
    return jnp.einsum("...k,kh->...h", onehot, rel_bias, precision=lax.Precision.HIGHEST)


def _in_proj_weights(w_in):
    depth, d, _ = w_in.shape
    split = _GATE_OFF + GATE_WIDTH
    assert _POOL_OFF >= split and _POOL_OFF % MXU_COLS == 0
    return jnp.concatenate(
        [w_in[:, :, :split].astype(BF16), jnp.zeros((depth, d, _POOL_OFF - split), BF16),
         w_in[:, :, split:].astype(BF16)], axis=2)


def _log2(n):
    assert n & (n - 1) == 0
    return n.bit_length() - 1


def _params(*sem):
    return pltpu.CompilerParams(dimension_semantics=sem, vmem_limit_bytes=VMEM_LIMIT)


def _layer_spec(a, l, **kw):
    return pl.BlockSpec((None,) + a.shape[1:], lambda *_: (l,) + (0,) * (a.ndim - 1), **kw)


def _rms(x, g):
    return x * lax.rsqrt(jnp.mean(x * x, axis=-1, keepdims=True) + EPS) * g


def _gelu(x):
    c = math.sqrt(2.0 / math.pi)
    return x * (0.5 * (1.0 + jnp.tanh(c * (x + 0.044715 * (x * x * x)))))


def _dot(a, b):
    return jnp.dot(a, b, preferred_element_type=F32)


def _dot_nt(a, b):
    return lax.dot_general(a, b, (((1,), (1,)), ((), ())), preferred_element_type=F32)


def _split3(x):
    hi = x.astype(BF16)
    r1 = x - hi.astype(F32)
    mid = r1.astype(BF16)
    lo = (r1 - mid.astype(F32)).astype(BF16)
    return hi, mid, lo


def _dot_f32_left(w_bf16, x):
    hi, mid, lo = _split3(x)
    return _dot(w_bf16, hi) + _dot(w_bf16, mid) + _dot(w_bf16, lo)


def _in_proj_kernel(x_ref, g_ref, w_ref, qa_ref, ksel_ref, kwin_ref, kcmp_ref, vcmp_ref,
                    vsel_ref, vwin_ref, gate_ref, pool_ref, u_ref, v_ref):
    t = x_ref.shape[1]
    h = _rms(x_ref[0], g_ref[...]).astype(BF16)

    def proj(off):
        assert off % MXU_COLS == 0
        return _dot(h, w_ref[:, off:off + MXU_COLS])

    zeros = jnp.zeros((t, HEAD_DIM), F32)

    def group_slab(piece, g):
        return jnp.concatenate([piece, zeros] if g == 0 else [zeros, piece], axis=1)

    scale = HEAD_DIM ** -0.5
    for g in range(KV_GROUPS):
        qg = proj(_Q_OFF + g * MXU_COLS) * scale
        for hg in range(HEADS_PER_GROUP):
            qa_ref[0, g * HEADS_PER_GROUP + hg] = group_slab(
                qg[:, hg * HEAD_DIM:(hg + 1) * HEAD_DIM], g).astype(BF16)

    lane = lax.broadcasted_iota(jnp.int32, (t, LANES), 1)
    kv_cmp = proj(_KCMP_OFF)
    kcmp_ref[0] = kv_cmp[:, :KV_WIDTH].astype(BF16)
    vcmp_ref[0] = kv_cmp[:, KV_WIDTH:].astype(BF16)
    tok = pl.program_id(1) * t + lax.broadcasted_iota(jnp.int32, (LANES, t), 1)
    blk_id = lax.broadcasted_iota(jnp.int32, (LANES, t), 0)
    ksel_ref[0, LANES:, :] = jnp.where(
        jnp.right_shift(tok, _log2(SEL_BLOCK)) == blk_id, 1.0, 0.0).astype(BF16)

    def keys_values(off, v_out):
        kv = proj(off)
        for g in range(KV_GROUPS):
            vg = group_slab(kv[:, KV_WIDTH + g * HEAD_DIM:KV_WIDTH + (g + 1) * HEAD_DIM], 0)
            v_out[0, g] = jnp.where(lane == HEAD_DIM, 1.0, vg).astype(BF16)
        return kv[:, :KV_WIDTH].T.astype(BF16)

    ksel_ref[0, :LANES, :] = keys_values(_KSEL_OFF, vsel_ref)
    kwin_ref[0] = keys_values(_KWIN_OFF, vwin_ref)
    gate_ref[0] = jax.nn.sigmoid(proj(_GATE_OFF)[:, :LANES])
    pool_ref[0] = proj(_POOL_OFF)
    u_ref[0] = proj(_U_OFF)
    v_ref[0] = proj(_V_OFF)


def _in_proj(x, g, w, l):
    b, s, d = x.shape
    t = T_IN
    grid = (b, s // t)
    row3 = lambda n: pl.BlockSpec((1, t, n), lambda bi, i: (bi, i, 0))
    out_shape = (
        jax.ShapeDtypeStruct((b, NSA_HEADS, s, LANES), BF16),
        jax.ShapeDtypeStruct((b, 2 * LANES, s), BF16),
        jax.ShapeDtypeStruct((b, LANES, s), BF16),
        jax.ShapeDtypeStruct((b, s, LANES), BF16),
        jax.ShapeDtypeStruct((b, s, LANES), BF16),
        jax.ShapeDtypeStruct((b, KV_GROUPS, s, LANES), BF16),
        jax.ShapeDtypeStruct((b, KV_GROUPS, s, LANES), BF16),
        jax.ShapeDtypeStruct((b, s, LANES), F32),
        jax.ShapeDtypeStruct((b, s, POOL_WIDTH), F32),
        jax.ShapeDtypeStruct((b, s, SGU_WIDTH), F32),
        jax.ShapeDtypeStruct((b, s, SGU_WIDTH), F32),
    )
    out_specs = (
        pl.BlockSpec((1, NSA_HEADS, t, LANES), lambda bi, i: (bi, 0, i, 0)),
        pl.BlockSpec((1, 2 * LANES, t), lambda bi, i: (bi, 0, i)),
        pl.BlockSpec((1, LANES, t), lambda bi, i: (bi, 0, i)),
        row3(LANES), row3(LANES),
        pl.BlockSpec((1, KV_GROUPS, t, LANES), lambda bi, i: (bi, 0, i, 0)),
        pl.BlockSpec((1, KV_GROUPS, t, LANES), lambda bi, i: (bi, 0, i, 0)),
        row3(LANES), row3(POOL_WIDTH), row3(SGU_WIDTH), row3(SGU_WIDTH),
    )
    return pl.pallas_call(
        _in_proj_kernel, grid=grid,
        in_specs=[row3(d), _layer_spec(g, l), _layer_spec(w, l)],
        out_specs=out_specs, out_shape=out_shape,
        compiler_params=_params("parallel", "parallel"), name="in_proj",
    )(x, g, w)


def _compress_kernel(rk_ref, rv_ref, w1a_ref, w1b_ref, pos_ref, b1_ref, w2k_ref, b2k_ref,
                     w2v_ref, b2v_ref, kc_ref, vct_ref):
    nb = rk_ref.shape[1]
    keep = lax.broadcasted_iota(jnp.int32, (nb, LANES), 0) < nb - 1

    def hidden(r, which):
        a = _dot(r, w1a_ref[which])
        bm = _dot(r, w1b_ref[which])
        pre = a + pltpu.roll(bm, nb - 1, 0)
        posb = (_dot(pos_ref[which, 0], w1a_ref[which])[0:1]
                + _dot(pos_ref[which, 1], w1b_ref[which])[0:1])
        return _gelu(pre + posb + b1_ref[which]).astype(BF16)

    kc = _dot(hidden(rk_ref[0], 0), w2k_ref[...]) + b2k_ref[...]
    kc_ref[0] = jnp.where(keep, kc, 0.0).astype(BF16)
    vc = _dot(hidden(rv_ref[0], 1), w2v_ref[...]) + b2v_ref[...]
    vct_ref[0] = jnp.where(keep, vc, 0.0).T.astype(BF16)


def _compress(kcmp, vcmp, cw, l):
    b, s, _ = kcmp.shape
    nb = s // CMP_STRIDE
    rk = kcmp.reshape(b, nb, CMP_STRIDE * KV_WIDTH)
    rv = vcmp.reshape(b, nb, CMP_STRIDE * KV_WIDTH)
    ws = (cw["w1a"], cw["w1b"], cw["pos"], cw["b1"], cw["w2k"], cw["b2k"], cw["w2v"], cw["b2v"])
    return pl.pallas_call(
        _compress_kernel, grid=(b,),
        in_specs=[pl.BlockSpec((1, nb, rk.shape[2]), lambda bi: (bi, 0, 0)),
                  pl.BlockSpec((1, nb, rv.shape[2]), lambda bi: (bi, 0, 0))]
                 + [_layer_spec(a, l) for a in ws],
        out_specs=(pl.BlockSpec((1, nb, KV_WIDTH), lambda bi: (bi, 0, 0)),
                   pl.BlockSpec((1, KV_WIDTH, nb), lambda bi: (bi, 0, 0))),
        out_shape=(jax.ShapeDtypeStruct((b, nb, KV_WIDTH), BF16),
                   jax.ShapeDtypeStruct((b, KV_WIDTH, nb), BF16)),
        compiler_params=_params("parallel"), name="compress",
    )(rk, rv, *ws)


def _compress_weights(pos, w1, b1, w2, b2):
    half = CMP_LEN // 2

    def group_diag(w):
        z = jnp.zeros_like(w)
        return jnp.stack([jnp.concatenate([w if p == g else z for g in range(KV_GROUPS)], axis=-1)
                          for p in range(KV_GROUPS)], axis=-3)

    def w1_blocks(w):
        return group_diag(w).reshape(half * KV_WIDTH, KV_GROUPS * CMP_HIDDEN)

    def pos_rows(p):
        r = jnp.broadcast_to(p[:, None, :], (half, KV_GROUPS, HEAD_DIM)).reshape(1, half * KV_WIDTH)
        return jnp.concatenate([r, jnp.zeros((SUBLANES - 1, half * KV_WIDTH), F32)], axis=0)

    w1a = jnp.stack([w1_blocks(w1[i, :half]) for i in range(2)]).astype(BF16)
    w1b = jnp.stack([w1_blocks(w1[i, half:]) for i in range(2)]).astype(BF16)
    posr = jnp.stack([jnp.stack([pos_rows(pos[i, :half]), pos_rows(pos[i, half:])])
                      for i in range(2)]).astype(BF16)
    b1t = jnp.tile(b1, (1, KV_GROUPS))[:, None, :]
    w2blk = lambda w: group_diag(w).reshape(KV_GROUPS * CMP_HIDDEN, KV_WIDTH)
    return dict(w1a=w1a, w1b=w1b, pos=posr, b1=b1t,
                w2k=w2blk(w2[0]).astype(BF16), b2k=jnp.tile(b2[0], KV_GROUPS)[None, :],
                w2v=w2blk(w2[1]).astype(BF16), b2v=jnp.tile(b2[1], KV_GROUPS)[None, :])


CMP_PAD = 64
CMP_BAND_BACK = 56
CMP_BAND = CMP_BAND_BACK + T_CMP // CMP_STRIDE
CMP_BAND_DIST0 = CMP_BAND_BACK * CMP_STRIDE - (CMP_LEN - 1)
assert CMP_BAND_DIST0 + CMP_STRIDE >= FAR_DIST and CMP_BAND_BACK <= CMP_PAD
CMP_ROW_CHUNK = 128
SEL_RATIO = SEL_BLOCK // CMP_STRIDE


def _gate_column(head, branch):
    return head * 3 + branch


def _cmp_kernel(gate_ref, qa_ref, kc_ref, vct_ref, band_ref, wagg_ref, ocmp_ref, qaug_ref,
                st_ref, gt_ref):
    tq = qa_ref.shape[2]
    nb = kc_ref.shape[1]
    i = pl.program_id(1)
    g = pl.program_id(0)
    s0 = i * tq
    t = s0 + lax.broadcasted_iota(jnp.int32, (1, tq), 1)
    n_live = (i + 1) * (tq // CMP_STRIDE)
    band_row0 = pl.multiple_of(s0 // CMP_STRIDE - CMP_BAND_BACK + CMP_PAD, SUBLANES)
    gt_ref[...] = gate_ref[0].T
    st_ref[:, :CMP_PAD, :] = jnp.zeros((HEADS_PER_GROUP, CMP_PAD, tq), F32)
    has_block = t >= CMP_LEN - 1

    def body(rows):
        nsr = rows // SEL_RATIO
        kc = kc_ref[0, :rows, :]
        tail = rows - CMP_ROW_CHUNK
        dead = tail + lax.broadcasted_iota(jnp.int32, (CMP_ROW_CHUNK, 1), 0) >= n_live
        imp = jnp.zeros((rows, tq), F32)
        outs = []
        for h in range(HEADS_PER_GROUP):
            st_ref[h, CMP_PAD:CMP_PAD + rows, :] = _dot_nt(kc, qa_ref[0, h])
        for h in range(HEADS_PER_GROUP):
            st_ref[h, pl.ds(band_row0, CMP_BAND), :] += band_ref[h]
        for h in range(HEADS_PER_GROUP):
            last = jnp.where(dead, MASK_BIAS, st_ref[h, CMP_PAD + tail:CMP_PAD + rows, :])
            s = last if tail == 0 else jnp.concatenate(
                [st_ref[h, CMP_PAD:CMP_PAD + tail, :], last], axis=0)
            e = jnp.exp(s - jnp.max(s, axis=0, keepdims=True))
            denom = jnp.maximum(jnp.sum(e, axis=0, keepdims=True), 1.0)
            rinv = jnp.where(has_block, 1.0 / denom, 0.0)
            imp = imp + e * rinv
            gate = gt_ref[pl.ds(_gate_column(g * HEADS_PER_GROUP + h, 0), 1), :]
            outs.append((gate * rinv) * _dot(vct_ref[0, :, :rows], e.astype(BF16)))
        ocmp_ref[0] = jnp.concatenate(outs, axis=0).T.astype(BF16)

        p_sel = _dot_f32_left(wagg_ref[:nsr, :rows], imp)
        j = lax.broadcasted_iota(jnp.int32, (nsr, 1), 0)
        rel = jnp.right_shift(t, _log2(SEL_BLOCK)) - j
        forced = (j == 0) | ((rel >= 0) & (rel < N_LOCAL_SEL))
        score = jnp.where(forced, FORCE_SCORE, jnp.where(j * SEL_BLOCK <= t, p_sel, -1.0))
        jfull = jnp.broadcast_to(j, (nsr, tq)).astype(F32)
        sel = jnp.zeros((nsr, tq), F32)
        for _ in range(SEL_TOPK):
            m = jnp.max(score, axis=0, keepdims=True)
            first = jnp.min(jnp.where(score == m, jfull, float(nsr)), axis=0, keepdims=True)
            hit = jfull == first
            sel = jnp.where(hit, 1.0, sel)
            score = jnp.where(hit, -3e38, score)
        notsel = jnp.where(sel > 0.5, 0.0, MASK_BIAS)
        if nsr < LANES:
            notsel = jnp.concatenate(
                [notsel, jnp.full((LANES - nsr, tq), MASK_BIAS, F32)], axis=0)
        notsel = notsel.T.astype(BF16)
        for h in range(HEADS_PER_GROUP):
            qaug_ref[0, h, :, :LANES] = qa_ref[0, h]
            qaug_ref[0, h, :, LANES:] = notsel

    for rows in range(CMP_ROW_CHUNK, nb + 1, CMP_ROW_CHUNK):
        @pl.when((n_live > rows - CMP_ROW_CHUNK) & (n_live <= rows))
        def _():
            body(rows)


def _cmp_attention(gates, qa, kc, vct, band, wagg):
    b, _, s, _ = qa.shape
    nb = kc.shape[1]
    tq = T_CMP
    assert nb % CMP_ROW_CHUNK == 0
    grid = (KV_GROUPS, s // tq, b)
    return pl.pallas_call(
        _cmp_kernel, grid=grid,
        in_specs=[
            pl.BlockSpec((1, tq, LANES), lambda g, i, bi: (bi, i, 0)),
            pl.BlockSpec((1, HEADS_PER_GROUP, tq, LANES), lambda g, i, bi: (bi, g, i, 0)),
            pl.BlockSpec((1, nb, KV_WIDTH), lambda g, i, bi: (bi, 0, 0)),
            pl.BlockSpec((1, HEAD_DIM, nb), lambda g, i, bi: (bi, g, 0)),
            pl.BlockSpec((HEADS_PER_GROUP, CMP_BAND, tq), lambda g, i, bi: (g, 0, 0)),
            pl.BlockSpec(wagg.shape, lambda g, i, bi: (0, 0)),
        ],
        out_specs=(
            pl.BlockSpec((1, tq, HEADS_PER_GROUP * HEAD_DIM), lambda g, i, bi: (bi, i, g)),
            pl.BlockSpec((1, HEADS_PER_GROUP, tq, 2 * LANES), lambda g, i, bi: (bi, g, i, 0)),
        ),
        out_shape=(jax.ShapeDtypeStruct((b, s, NSA_WIDTH), BF16),
                   jax.ShapeDtypeStruct((b, NSA_HEADS, s, 2 * LANES), BF16)),
        scratch_shapes=[pltpu.VMEM((HEADS_PER_GROUP, CMP_PAD + nb, tq), F32),
                        pltpu.VMEM((LANES, tq), F32)],
        compiler_params=_params("parallel", "parallel", "parallel"), name="cmp_attention",
    )(gates, qa, kc, vct, band, wagg)


def _cmp_tables(rel_bias, nb):
    jj = np.arange(CMP_BAND)[:, None]
    ii = np.arange(T_CMP)[None, :]
    dist = CMP_BAND_DIST0 + ii - CMP_STRIDE * jj
    bidx = _BUCKETS[np.clip(dist, 0, None)]
    band = jnp.transpose(_bias_lookup(rel_bias, bidx) - rel_bias[REL_BUCKETS - 1], (2, 0, 1))
    band = jnp.where(jnp.asarray(dist >= 0)[None], band, MASK_BIAS)
    ns = nb // SEL_RATIO
    assert SEL_TOPK <= ns <= LANES
    wagg = np.zeros((LANES, nb), np.float32)
    for m, w in enumerate(SEL_AGG):
        for jb in range(ns):
            c = SEL_RATIO * jb + m - 1
            if 0 <= c < nb - 1:
                wagg[jb, c] = w
    return band, jnp.asarray(wagg, BF16)


N_NEAR = (FAR_DIST + T_ATT - 1) // T_ATT + 1
assert N_NEAR * T_ATT - (T_ATT - 1) >= FAR_DIST
WIN_TILES = (WINDOW - 1 + T_ATT - 1) // T_ATT
TOEPLITZ_ROWS = 8
assert max(N_NEAR, WIN_TILES + 1) <= TOEPLITZ_ROWS
FAR_KEYS = 4 * T_ATT


def _flash_kernel(gate_ref, q_ref, kt_ref, v_ref, gv_ref, o_ref, m_ref, acc_ref, nb_ref, *,
                  windowed, branch):
    t = T_ATT
    g = pl.program_id(1)
    i = pl.program_id(2)
    n_tab = WIN_TILES + 1 if windowed else N_NEAR

    @pl.when(i == 0)
    def _():
        for h in range(HEADS_PER_GROUP):
            for d in range(n_tab):
                gen = jnp.broadcast_to(gv_ref[0, h, d:d + 1, :], (t, 2 * t))
                nb_ref[h, d] = pltpu.roll(gen, 0, 1, stride=1, stride_axis=0)[:, :t]

    m_ref[...] = jnp.full(m_ref.shape, NEG_INIT, F32)
    acc_ref[...] = jnp.zeros(acc_ref.shape, F32)

    def tile_step(first_tile, n_tiles, dists=()):
        heads = range(HEADS_PER_GROUP)
        assert not dists or len(dists) == n_tiles
        nk = t * n_tiles
        start = pl.multiple_of(first_tile * t, t)
        kt = kt_ref[0, :, pl.ds(start, nk)]
        v = v_ref[0, 0, pl.ds(start, nk), :]
        s = [_dot(q_ref[0, h], kt) for h in heads]
        if dists:
            s = [s[h] + jnp.concatenate([nb_ref[h, d] for d in dists], axis=1) for h in heads]
            dist = (dists[0] * t + lax.broadcasted_iota(jnp.int32, (t, nk), 0)
                    - lax.broadcasted_iota(jnp.int32, (t, nk), 1))
            causal = dists[-1] == 0
            banded = windowed and dists[0] * t + t - 1 >= WINDOW
            assert not (causal and banded)
            keep = None
            if causal:
                keep = dist >= 0
            elif banded:
                keep = dist < WINDOW
            if keep is not None:
                s = [jnp.where(keep, s[h], MASK_BIAS) for h in heads]
        m_prev = [m_ref[h] for h in heads]
        m_new = [jnp.maximum(m_prev[h], jnp.max(s[h], axis=1, keepdims=True)) for h in heads]
        p = [jnp.exp(s[h] - jnp.tile(m_new[h], (1, nk // LANES))).astype(BF16) for h in heads]
        pv = [_dot(p[h], v) for h in heads]
        for h in heads:
            acc_ref[h] = jnp.exp(m_prev[h] - m_new[h]) * acc_ref[h] + pv[h]
            m_ref[h] = m_new[h]

    def near_steps(first):
        @pl.when(i >= first)
        def _():
            d = first
            if d % 2:
                tile_step(i - d, 1, (d,))
                d -= 1
            for d in range(d, 0, -2):
                tile_step(i - d, 2, (d, d - 1))

        @pl.when(i < first)
        def _():
            for d in range(first - 1, 0, -1):
                @pl.when(i >= d)
                def _():
                    tile_step(i - d, 1, (d,))
        tile_step(i, 1, (0,))

    if windowed:
        near_steps(WIN_TILES)
    else:
        n_far = jnp.maximum(i - (N_NEAR - 1), 0)
        far_tiles = FAR_KEYS // t

        def far_body(j, carry):
            tile_step(j * far_tiles, far_tiles)
            return carry
        n_far_steps = jnp.right_shift(n_far, _log2(far_tiles))
        lax.fori_loop(0, n_far_steps, far_body, 0)
        done = n_far_steps * far_tiles
        run = far_tiles // 2
        while run >= 1:
            has_run = jnp.bitwise_and(n_far, run) != 0

            @pl.when(has_run)
            def _():
                tile_step(done, run)
            done = done + jnp.where(has_run, run, 0)
            run //= 2
        near_steps(N_NEAR - 1)

    lane = lax.broadcasted_iota(jnp.int32, (t, LANES), 1)
    gates = gate_ref[0]
    outs = []
    for h in range(HEADS_PER_GROUP):
        col = _gate_column(g * HEADS_PER_GROUP + h, branch)
        gate = jnp.sum(jnp.where(lane == col, gates, 0.0), axis=1, keepdims=True)
        acc = acc_ref[h]
        outs.append(acc[:, :HEAD_DIM] / jnp.maximum(acc[:, HEAD_DIM:HEAD_DIM + 1], 1.0) * gate)
    o_ref[0] = jnp.concatenate(outs, axis=1).astype(BF16)


def _flash(gates, q_aug, kt, v, gv, *, windowed, branch):
    b, _, s, kd = q_aug.shape
    assert kt.shape[1] == kd
    t = T_ATT
    grid = (b, KV_GROUPS, s // t)
    return pl.pallas_call(
        functools.partial(_flash_kernel, windowed=windowed, branch=branch), grid=grid,
        in_specs=[
            pl.BlockSpec((1, t, LANES), lambda bi, g, i: (bi, i, 0)),
            pl.BlockSpec((1, HEADS_PER_GROUP, t, kd), lambda bi, g, i: (bi, g, i, 0)),
            pl.BlockSpec((1, kd, s), lambda bi, g, i: (bi, 0, 0)),
            pl.BlockSpec((1, 1, s, LANES), lambda bi, g, i: (bi, g, 0, 0)),
            pl.BlockSpec((1, HEADS_PER_GROUP, TOEPLITZ_ROWS, 2 * t), lambda bi, g, i: (g, 0, 0, 0)),
        ],
        out_specs=pl.BlockSpec((1, t, HEADS_PER_GROUP * HEAD_DIM), lambda bi, g, i: (bi, i, g)),
        out_shape=jax.ShapeDtypeStruct((b, s, NSA_WIDTH), BF16),
        scratch_shapes=[pltpu.VMEM((HEADS_PER_GROUP, t, LANES), F32),
                        pltpu.VMEM((HEADS_PER_GROUP, t, LANES), F32),
                        pltpu.VMEM((HEADS_PER_GROUP, WIN_TILES + 1 if windowed else N_NEAR, t, t),
                                   F32)],
        compiler_params=_params("parallel", "parallel", "arbitrary"),
        name="win_attention" if windowed else "sel_attention",
    )(gates, q_aug, kt, v, gv)


def _toeplitz_generators(rel_bias):
    t = T_ATT
    d = np.arange(TOEPLITZ_ROWS)[:, None]
    y = np.arange(2 * t)[None, :]
    dist = np.where(y < t, d * t - y, d * t + 2 * t - y)
    tab = _bias_lookup(rel_bias, _BUCKETS[np.clip(dist, 0, None)]) - rel_bias[REL_BUCKETS - 1]
    tab = jnp.where(jnp.asarray(dist >= 0)[..., None], tab, 0.0)
    return jnp.transpose(tab, (2, 0, 1)).reshape(KV_GROUPS, HEADS_PER_GROUP, TOEPLITZ_ROWS, 2 * t)


def _pool_tile(x, halo, i, w, scale):
    t = x.shape[0]
    halo = jnp.where(i > 0, halo, 0.0)
    xs = jnp.concatenate([halo, x], axis=0)
    sums = []
    acc = xs
    span = 1
    for win in POOL_WINDOWS:
        while span < win:
            acc = acc + pltpu.roll(acc, span, 0)
            span *= 2
        sums.append(acc[HALO:])
    lane_grp = jnp.right_shift(lax.broadcasted_iota(jnp.int32, (t, POOL_WIDTH), 1), _log2(POOL_CH))
    tok = i * t + lax.broadcasted_iota(jnp.int32, (t, POOL_WIDTH), 0)
    ssum = sums[-1]
    win = jnp.full((t, POOL_WIDTH), POOL_WINDOWS[-1], jnp.int32)
    for gi in range(len(POOL_WINDOWS) - 2, -1, -1):
        ssum = jnp.where(lane_grp == gi, sums[gi], ssum)
        win = jnp.where(lane_grp == gi, POOL_WINDOWS[gi], win)
    cnt = jnp.minimum(tok + 1, win).astype(F32)
    dlt = (ssum / cnt - x).astype(BF16)
    return (_dot(dlt, w) * scale).astype(BF16)


def _sgu_tile(u, v, norm_g, norm_b, ws_cat, bs_exp):
    t = u.shape[0]
    v = _gelu(v)
    mu = jnp.mean(v, axis=-1, keepdims=True)
    var = jnp.mean(jnp.square(v - mu), axis=-1, keepdims=True)
    vn = (v - mu) * lax.rsqrt(var + EPS) * norm_g + norm_b
    lane_grp = jnp.right_shift(lax.broadcasted_iota(jnp.int32, (SGU_CHUNK, SGU_WIDTH), 1),
                               _log2(SGU_CH))
    r = lax.broadcasted_iota(jnp.int32, (SGU_CHUNK, SGU_GROUPS * SGU_CHUNK), 0)
    c = lax.broadcasted_iota(jnp.int32, (SGU_CHUNK, SGU_GROUPS * SGU_CHUNK), 1) & (SGU_CHUNK - 1)
    ws = jnp.where(c <= r, ws_cat, 0.0).astype(BF16)
    outs = []
    for n in range(t // SGU_CHUNK):
        vc = vn[n * SGU_CHUNK:(n + 1) * SGU_CHUNK]
        stacked = jnp.concatenate(
            [jnp.where(lane_grp == gi, vc, 0.0) for gi in range(SGU_GROUPS)], axis=0).astype(BF16)
        mixed = _dot(ws, stacked) + bs_exp
        outs.append((_gelu(u[n * SGU_CHUNK:(n + 1) * SGU_CHUNK]) * mixed).astype(BF16))
    return jnp.concatenate(outs, axis=0)


def _out_proj_kernel(oc_ref, os_ref, ow_ref, pin_ref, halo_ref, u_ref, v_ref, pw_ref, ps_ref,
                     ng_ref, nb_ref, ws_ref, bs_ref, w_ref, g_ref, x_ref, y_ref):
    a = (oc_ref[0].astype(F32) + os_ref[0].astype(F32) + ow_ref[0].astype(F32)).astype(BF16)
    pool = _pool_tile(pin_ref[0], halo_ref[0], pl.program_id(1), pw_ref[...], ps_ref[...])
    sgu = _sgu_tile(u_ref[0], v_ref[0], ng_ref[...], nb_ref[...], ws_ref[...], bs_ref[...])
    mix = (_dot(a, w_ref[:NSA_WIDTH, :])
           + _dot(pool, w_ref[NSA_WIDTH:NSA_WIDTH + POOL_WIDTH, :])
           + _dot(sgu, w_ref[NSA_WIDTH + POOL_WIDTH:, :]))
    y_ref[0] = x_ref[0] + _rms(mix, g_ref[...])


def _out_proj(o_cmp, o_sel, o_win, pin, u, v, pool_w, pool_scale, sgu_g, sgu_b, ws_cat, bs_exp,
              w, g, x, l):
    b, s, d = x.shape
    t = T_OUT
    assert t % SGU_CHUNK == 0 and t % HALO == 0
    row = lambda n: pl.BlockSpec((1, t, n), lambda bi, i: (bi, i, 0))
    halo = pl.BlockSpec((1, HALO, POOL_WIDTH),
                        lambda bi, i: (bi, jnp.maximum(i * (t // HALO) - 1, 0), 0))
    layer = [_layer_spec(a, l) for a in (pool_w, pool_scale, sgu_g, sgu_b, ws_cat, bs_exp, w, g)]
    return pl.pallas_call(
        _out_proj_kernel, grid=(b, s // t),
        in_specs=[row(NSA_WIDTH), row(NSA_WIDTH), row(NSA_WIDTH), row(POOL_WIDTH), halo,
                  row(SGU_WIDTH), row(SGU_WIDTH)] + layer + [row(d)],
        out_specs=row(d), out_shape=jax.ShapeDtypeStruct((b, s, d), F32),
        compiler_params=_params("parallel", "parallel"), name="out_proj",
    )(o_cmp, o_sel, o_win, pin, pin, u, v, pool_w, pool_scale, sgu_g, sgu_b, ws_cat, bs_exp, w, g, x)


def _mem_kv_kernel(mem_ref, g_ref, wk_ref, wv_ref, kt_ref, v_ref):
    m = _rms(mem_ref[0], g_ref[...]).astype(BF16)
    kt_ref[0] = _dot(m, wk_ref[...]).T.astype(BF16)
    v_ref[0] = _dot(m, wv_ref[...]).astype(BF16)


def _mem_kv(mem, g, wk, wv, l):
    b, m, d = mem.shape
    blk = pl.BlockSpec((1, m, d), lambda bi: (bi, 0, 0))
    blk_t = pl.BlockSpec((1, d, m), lambda bi: (bi, 0, 0))
    return pl.pallas_call(
        _mem_kv_kernel, grid=(b,),
        in_specs=[blk] + [_layer_spec(a, l) for a in (g, wk, wv)],
        out_specs=(blk_t, blk),
        out_shape=(jax.ShapeDtypeStruct((b, d, m), BF16), jax.ShapeDtypeStruct((b, m, d), BF16)),
        compiler_params=_params("parallel"), name="mem_kv",
    )(mem, g, wk, wv)


def _mem_attn_kernel(x_ref, gpre_ref, wq_ref, kt_ref, v_ref, wo_ref, gpost_ref, y_ref):
    x = x_ref[0]
    h = _rms(x, gpre_ref[...]).astype(BF16)
    q = (_dot(h, wq_ref[...]) * (MEM_HEAD_DIM ** -0.5)).astype(BF16)
    heads = range(MEM_HEADS)
    sl = [slice(hd * MEM_HEAD_DIM, (hd + 1) * MEM_HEAD_DIM) for hd in heads]
    s = [_dot(q[:, sl[hd]], kt_ref[0, sl[hd], :]) for hd in heads]
    e = [jnp.exp(s[hd] - jnp.max(s[hd], axis=-1, keepdims=True)) for hd in heads]
    p = [(e[hd] * (1.0 / jnp.sum(e[hd], axis=-1, keepdims=True))).astype(BF16) for hd in heads]
    o = jnp.concatenate([_dot(p[hd], v_ref[0, :, sl[hd]]).astype(BF16) for hd in heads], axis=1)
    y_ref[0] = x + _rms(_dot(o, wo_ref[...]), gpost_ref[...])


def _mem_attn(x, gpre, wq, kt, v, wo, gpost, l):
    b, s, d = x.shape
    m = v.shape[1]
    t = T_MEM
    row = pl.BlockSpec((1, t, d), lambda bi, i: (bi, i, 0))
    return pl.pallas_call(
        _mem_attn_kernel, grid=(b, s // t),
        in_specs=[row, _layer_spec(gpre, l), _layer_spec(wq, l),
                  pl.BlockSpec((1, d, m), lambda bi, i: (bi, 0, 0)),
                  pl.BlockSpec((1, m, d), lambda bi, i: (bi, 0, 0)),
                  _layer_spec(wo, l), _layer_spec(gpost, l)],
        out_specs=row, out_shape=jax.ShapeDtypeStruct((b, s, d), F32),
        compiler_params=_params("parallel", "parallel"), name="mem_attention",
    )(x, gpre, wq, kt, v, wo, gpost)


def _ffn_kernel(x_ref, halo_ref, gpre_ref, wup_ref, cw_ref, cb_ref, wd_ref, gpost_ref, y_ref,
                act_ref):
    i = pl.program_id(1)
    f = wd_ref.shape[0]
    halo = jnp.where(i > 0, halo_ref[0], 0.0)
    h = _rms(jnp.concatenate([halo, x_ref[0]], axis=0), gpre_ref[...]).astype(BF16)

    def conv(off):
        a = _dot(h, wup_ref[:, off:off + F_CHUNK])
        cw = cw_ref[:, off:off + F_CHUNK]
        return (cw[0:1] * pltpu.roll(a, 2, 0) + cw[1:2] * pltpu.roll(a, 1, 0) + cw[2:3] * a
                + cb_ref[:, off:off + F_CHUNK])

    for off in range(0, f, F_CHUNK):
        act = _gelu(conv(off)) * conv(f + off)
        act_ref[:, off:off + F_CHUNK] = act[HALO:].astype(BF16)
    y_ref[0] = x_ref[0] + _rms(_dot(act_ref[...], wd_ref[...]), gpost_ref[...])


def _ffn(x, gpre, w_up, conv_w, conv_b, w_down, gpost, l):
    b, s, d = x.shape
    f = w_down.shape[1]
    t = T_FFN
    assert f % F_CHUNK == 0
    resident = lambda a: _layer_spec(a, l, pipeline_mode=pl.Buffered(1))
    return pl.pallas_call(
        _ffn_kernel, grid=(b, s // t),
        in_specs=[
            pl.BlockSpec((1, t, d), lambda bi, i: (bi, i, 0)),
            pl.BlockSpec((1, HALO, d), lambda bi, i: (bi, jnp.maximum(i * (t // HALO) - 1, 0), 0)),
            resident(gpre), resident(w_up), resident(conv_w), resident(conv_b), resident(w_down),
            resident(gpost),
        ],
        out_specs=pl.BlockSpec((1, t, d), lambda bi, i: (bi, i, 0)),
        out_shape=jax.ShapeDtypeStruct((b, s, d), F32),
        scratch_shapes=[pltpu.VMEM((t, f), BF16)],
        compiler_params=_params("parallel", "parallel"), name="conv_ffn",
    )(x, x, gpre, w_up, conv_w, conv_b, w_down, gpost)


def kernel(x, mem, rel_bias, mix_norm_pre, mix_norm_post, w_in, cmp_pos, cmp_w1, cmp_b1, cmp_w2,
           cmp_b2, pool_w, pool_scale, sgu_norm_g, sgu_norm_b, sgu_w, sgu_b, w_out, mem_norm_pre,
           mem_norm_kv, mem_norm_post, w_mq, w_mk, w_mv, w_mo, ffn_norm_pre, ffn_norm_post, w_up,
           conv_w, conv_b, w_down):
    depth = w_in.shape[0]
    s = x.shape[1]
    assert s % max(T_IN, T_CMP, T_ATT, T_OUT, T_MEM, T_FFN) == 0
    nb = s // CMP_STRIDE

    w_in_p = _in_proj_weights(w_in)
    band, wagg = _cmp_tables(rel_bias, nb)
    toep = _toeplitz_generators(rel_bias)
    row = lambda a: a[:, None, :]
    bf = lambda a: a.astype(BF16)
    w_out_b, w_mq_b, w_mk_b, w_mv_b, w_mo_b = bf(w_out), bf(w_mq), bf(w_mk), bf(w_mv), bf(w_mo)
    w_up_b, w_down_b = bf(w_up), bf(w_down)
    g_mix_pre, g_mix_post = row(mix_norm_pre), row(mix_norm_post)
    g_mem_pre, g_mem_kv, g_mem_post = row(mem_norm_pre), row(mem_norm_kv), row(mem_norm_post)
    g_ffn_pre, g_ffn_post, conv_b_r = row(ffn_norm_pre), row(ffn_norm_post), row(conv_b)
    cw = jax.vmap(_compress_weights)(cmp_pos, cmp_w1, cmp_b1, cmp_w2, cmp_b2)
    n_pool = len(POOL_WINDOWS)
    pool_blk = bf(jnp.einsum("lgcd,gh->lgchd", pool_w, jnp.eye(n_pool, dtype=F32)).reshape(
        depth, POOL_WIDTH, POOL_WIDTH))
    pool_scale_r = row(pool_scale)
    sgu_g, sgu_bias = row(sgu_norm_g), row(sgu_norm_b)
    ws_cat = jnp.concatenate([sgu_w[:, gi] for gi in range(SGU_GROUPS)], axis=2)
    bs_exp = jnp.repeat(jnp.swapaxes(sgu_b, 1, 2), SGU_CH, axis=2)

    for l in range(depth):
        qa, ksel_t, kwin_t, kcmp, vcmp, vsel, vwin, gates, pin, u, v = _in_proj(
            x, g_mix_pre, w_in_p, l)
        kc, vct = _compress(kcmp, vcmp, cw, l)
        o_cmp, q_aug = _cmp_attention(gates, qa, kc, vct, band, wagg)
        o_sel = _flash(gates, q_aug, ksel_t, vsel, toep, windowed=False, branch=1)
        o_win = _flash(gates, qa, kwin_t, vwin, toep, windowed=True, branch=2)
        x = _out_proj(o_cmp, o_sel, o_win, pin, u, v, pool_blk, pool_scale_r, sgu_g, sgu_bias,
                      ws_cat, bs_exp, w_out_b, g_mix_post, x, l)
        km, vm = _mem_kv(mem, g_mem_kv, w_mk_b, w_mv_b, l)
        x = _mem_attn(x, g_mem_pre, w_mq_b, km, vm, w_mo_b, g_mem_post, l)
        x = _ffn(x, g_ffn_pre, w_up_b, conv_w, conv_b_r, w_down_b, g_ffn_post, l)
    return x
```

```python
import functools
import math

import numpy as np
import jax
import jax.numpy as jnp
from jax import lax
from jax.experimental import pallas as pl
from jax.experimental.pallas import tpu as pltpu

F32 = jnp.float32
BF16 = jnp.bfloat16

D_MODEL = 1024
NSA_WIDTH = 512
POOL_WIDTH = 256
SGU_WIDTH = 256
NSA_HEADS = 8
KV_GROUPS = 2
HEADS_PER_GROUP = 4
HEAD_DIM = 64
KV_WIDTH = 128
GATE_WIDTH = 24
CMP_STRIDE = 16
CMP_LEN = 32
CMP_HIDDEN = 128
SEL_BLOCK = 64
SEL_TOPK = 16
N_LOCAL_SEL = 2
SEL_AGG = (1.0, 2.0, 2.0, 2.0, 1.0)
FORCE_SCORE = 1e9
WINDOW = 512
REL_BUCKETS = 32
REL_MAX_DIST = 1024
POOL_WINDOWS = (2, 4, 8, 16)
POOL_CH = 64
SGU_GROUPS = 4
SGU_CH = 64
SGU_CHUNK = 128
MEM_HEADS = 4
MEM_HEAD_DIM = 256
FFN_HIDDEN = 2816
EPS = 1e-6

LANES = 128
SUBLANES = 8
BF16_ROWS = 16
VMEM_LIMIT = 56 * 1024 * 1024

T_IN = 512
T_CMP = 512
T_ATT = 512
T_OUT = 512
T_MEM = 512
T_FFN = 1024
F_CHUNK = 256
HALO = 16

MASK_BIAS = -float(2 ** 30)
NEG_INIT = -1e30

_Q_OFF = 0
_KCMP_OFF = NSA_WIDTH
_VCMP_OFF = _KCMP_OFF + KV_WIDTH
_KSEL_OFF = _VCMP_OFF + KV_WIDTH
_VSEL_OFF = _KSEL_OFF + KV_WIDTH
_KWIN_OFF = _VSEL_OFF + KV_WIDTH
_VWIN_OFF = _KWIN_OFF + KV_WIDTH
_GATE_OFF = _VWIN_OFF + KV_WIDTH
_POOL_OFF = 1536
_U_OFF = _POOL_OFF + POOL_WIDTH
_V_OFF = _U_OFF + SGU_WIDTH
IN_COLS = _V_OFF + SGU_WIDTH
MXU_COLS = 256


def _bucket_table(n_max):
    n = np.arange(n_max)
    max_exact = REL_BUCKETS // 2
    nf = np.maximum(n, 1).astype(np.float64)
    large = max_exact + (np.log(nf / max_exact) / math.log(REL_MAX_DIST / max_exact)
                         * (REL_BUCKETS - max_exact)).astype(np.int32)
    large = np.minimum(large, REL_BUCKETS - 1)
    return np.where(n < max_exact, n, large).astype(np.int32)


_BUCKETS = _bucket_table(16384)
FAR_DIST = int(np.argmax(_BUCKETS == REL_BUCKETS - 1))


def _bias_lookup(rel_bias, buckets):
    onehot = jnp.asarray(np.eye(REL_BUCKETS, dtype=np.float32)[buckets])
    return jnp.einsum("...k,kh->...h", onehot, rel_bias, precision=lax.Precision.HIGHEST)


def _in_proj_weights(w_in):
    depth, d, _ = w_in.shape
    split = _GATE_OFF + GATE_WIDTH
    assert _POOL_OFF >= split and _POOL_OFF % MXU_COLS == 0
    return jnp.concatenate(
        [w_in[:, :, :split].astype(BF16), jnp.zeros((depth, d, _POOL_OFF - split), BF16),
         w_in[:, :, split:].astype(BF16)], axis=2)


def _log2(n):
    assert n & (n - 1) == 0
    return n.bit_length() - 1


def _params(*sem):
    return pltpu.CompilerParams(dimension_semantics=sem, vmem_limit_bytes=VMEM_LIMIT)


def _layer_spec(a, l, **kw):
    return pl.BlockSpec((None,) + a.shape[1:], lambda *_: (l,) + (0,) * (a.ndim - 1), **kw)


def _rms(x, g):
    return x * lax.rsqrt(jnp.mean(x * x, axis=-1, keepdims=True) + EPS) * g


def _gelu(x):
    c = math.sqrt(2.0 / math.pi)
    return x * (0.5 * (1.0 + jnp.tanh(c * (x + 0.044715 * (x * x * x)))))


def _dot(a, b):
    return jnp.dot(a, b, preferred_element_type=F32)


def _dot_nt(a, b):
    return lax.dot_general(a, b, (((1,), (1,)), ((), ())), preferred_element_type=F32)


def _split3(x):
    hi = x.astype(BF16)
    r1 = x - hi.astype(F32)
    mid = r1.astype(BF16)
    lo = (r1 - mid.astype(F32)).astype(BF16)
    return hi, mid, lo


def _dot_f32_left(w_bf16, x):
    hi, mid, lo = _split3(x)
    return _dot(w_bf16, hi) + _dot(w_bf16, mid) + _dot(w_bf16, lo)


def _in_proj_kernel(x_ref, g_ref, w_ref, qa_ref, ksel_ref, kwin_ref, kcmp_ref, vcmp_ref,
                    vsel_ref, vwin_ref, gate_ref, pool_ref, u_ref, v_ref):
    t = x_ref.shape[1]
    h = _rms(x_ref[0], g_ref[...]).astype(BF16)

    def proj(off):
        assert off % MXU_COLS == 0
        return _dot(h, w_ref[:, off:off + MXU_COLS])

    zeros = jnp.zeros((t, HEAD_DIM), F32)

    def group_slab(piece, g):
        return jnp.concatenate([piece, zeros] if g == 0 else [zeros, piece], axis=1)

    scale = HEAD_DIM ** -0.5
    for g in range(KV_GROUPS):
        qg = proj(_Q_OFF + g * MXU_COLS) * scale
        for hg in range(HEADS_PER_GROUP):
            qa_ref[0, g * HEADS_PER_GROUP + hg] = group_slab(
                qg[:, hg * HEAD_DIM:(hg + 1) * HEAD_DIM], g).astype(BF16)

    lane = lax.broadcasted_iota(jnp.int32, (t, LANES), 1)
    kv_cmp = proj(_KCMP_OFF)
    kcmp_ref[0] = kv_cmp[:, :KV_WIDTH].astype(BF16)
    vcmp_ref[0] = kv_cmp[:, KV_WIDTH:].astype(BF16)
    tok = pl.program_id(1) * t + lax.broadcasted_iota(jnp.int32, (LANES, t), 1)
    blk_id = lax.broadcasted_iota(jnp.int32, (LANES, t), 0)
    ksel_ref[0, LANES:, :] = jnp.where(
        jnp.right_shift(tok, _log2(SEL_BLOCK)) == blk_id, 1.0, 0.0).astype(BF16)

    def keys_values(off, v_out):
        kv = proj(off)
        for g in range(KV_GROUPS):
            vg = group_slab(kv[:, KV_WIDTH + g * HEAD_DIM:KV_WIDTH + (g + 1) * HEAD_DIM], 0)
            v_out[0, g] = jnp.where(lane == HEAD_DIM, 1.0, vg).astype(BF16)
        return kv[:, :KV_WIDTH].T.astype(BF16)

    ksel_ref[0, :LANES, :] = keys_values(_KSEL_OFF, vsel_ref)
    kwin_ref[0] = keys_values(_KWIN_OFF, vwin_ref)
    gate_ref[0] = jax.nn.sigmoid(proj(_GATE_OFF)[:, :LANES])
    pool_ref[0] = proj(_POOL_OFF)
    u_ref[0] = proj(_U_OFF)
    v_ref[0] = proj(_V_OFF)


def _in_proj(x, g, w, l):
    b, s, d = x.shape
    t = T_IN
    grid = (b, s // t)
    row3 = lambda n: pl.BlockSpec((1, t, n), lambda bi, i: (bi, i, 0))
    out_shape = (
        jax.ShapeDtypeStruct((b, NSA_HEADS, s, LANES), BF16),
        jax.ShapeDtypeStruct((b, 2 * LANES, s), BF16),
        jax.ShapeDtypeStruct((b, LANES, s), BF16),
        jax.ShapeDtypeStruct((b, s, LANES), BF16),
        jax.ShapeDtypeStruct((b, s, LANES), BF16),
        jax.ShapeDtypeStruct((b, KV_GROUPS, s, LANES), BF16),
        jax.ShapeDtypeStruct((b, KV_GROUPS, s, LANES), BF16),
        jax.ShapeDtypeStruct((b, s, LANES), F32),
        jax.ShapeDtypeStruct((b, s, POOL_WIDTH), F32),
        jax.ShapeDtypeStruct((b, s, SGU_WIDTH), F32),
        jax.ShapeDtypeStruct((b, s, SGU_WIDTH), F32),
    )
    out_specs = (
        pl.BlockSpec((1, NSA_HEADS, t, LANES), lambda bi, i: (bi, 0, i, 0)),
        pl.BlockSpec((1, 2 * LANES, t), lambda bi, i: (bi, 0, i)),
        pl.BlockSpec((1, LANES, t), lambda bi, i: (bi, 0, i)),
        row3(LANES), row3(LANES),
        pl.BlockSpec((1, KV_GROUPS, t, LANES), lambda bi, i: (bi, 0, i, 0)),
        pl.BlockSpec((1, KV_GROUPS, t, LANES), lambda bi, i: (bi, 0, i, 0)),
        row3(LANES), row3(POOL_WIDTH), row3(SGU_WIDTH), row3(SGU_WIDTH),
    )
    return pl.pallas_call(
        _in_proj_kernel, grid=grid,
        in_specs=[row3(d), _layer_spec(g, l), _layer_spec(w, l)],
        out_specs=out_specs, out_shape=out_shape,
        compiler_params=_params("parallel", "parallel"), name="in_proj",
    )(x, g, w)


def _compress_kernel(rk_ref, rv_ref, w1a_ref, w1b_ref, pos_ref, b1_ref, w2k_ref, b2k_ref,
                     w2v_ref, b2v_ref, kc_ref, vct_ref):
    nb = rk_ref.shape[1]
    keep = lax.broadcasted_iota(jnp.int32, (nb, LANES), 0) < nb - 1

    def hidden(r, which):
        a = _dot(r, w1a_ref[which])
        bm = _dot(r, w1b_ref[which])
        pre = a + pltpu.roll(bm, nb - 1, 0)
        posb = (_dot(pos_ref[which, 0], w1a_ref[which])[0:1]
                + _dot(pos_ref[which, 1], w1b_ref[which])[0:1])
        return _gelu(pre + posb + b1_ref[which]).astype(BF16)

    kc = _dot(hidden(rk_ref[0], 0), w2k_ref[...]) + b2k_ref[...]
    kc_ref[0] = jnp.where(keep, kc, 0.0).astype(BF16)
    vc = _dot(hidden(rv_ref[0], 1), w2v_ref[...]) + b2v_ref[...]
    vct_ref[0] = jnp.where(keep, vc, 0.0).T.astype(BF16)


def _compress(kcmp, vcmp, cw, l):
    b, s, _ = kcmp.shape
    nb = s // CMP_STRIDE
    rk = kcmp.reshape(b, nb, CMP_STRIDE * KV_WIDTH)
    rv = vcmp.reshape(b, nb, CMP_STRIDE * KV_WIDTH)
    ws = (cw["w1a"], cw["w1b"], cw["pos"], cw["b1"], cw["w2k"], cw["b2k"], cw["w2v"], cw["b2v"])
    return pl.pallas_call(
        _compress_kernel, grid=(b,),
        in_specs=[pl.BlockSpec((1, nb, rk.shape[2]), lambda bi: (bi, 0, 0)),
                  pl.BlockSpec((1, nb, rv.shape[2]), lambda bi: (bi, 0, 0))]
                 + [_layer_spec(a, l) for a in ws],
        out_specs=(pl.BlockSpec((1, nb, KV_WIDTH), lambda bi: (bi, 0, 0)),
                   pl.BlockSpec((1, KV_WIDTH, nb), lambda bi: (bi, 0, 0))),
        out_shape=(jax.ShapeDtypeStruct((b, nb, KV_WIDTH), BF16),
                   jax.ShapeDtypeStruct((b, KV_WIDTH, nb), BF16)),
        compiler_params=_params("parallel"), name="compress",
    )(rk, rv, *ws)


def _compress_weights(pos, w1, b1, w2, b2):
    half = CMP_LEN // 2

    def group_diag(w):
        z = jnp.zeros_like(w)
        return jnp.stack([jnp.concatenate([w if p == g else z for g in range(KV_GROUPS)], axis=-1)
                          for p in range(KV_GROUPS)], axis=-3)

    def w1_blocks(w):
        return group_diag(w).reshape(half * KV_WIDTH, KV_GROUPS * CMP_HIDDEN)

    def pos_rows(p):
        r = jnp.broadcast_to(p[:, None, :], (half, KV_GROUPS, HEAD_DIM)).reshape(1, half * KV_WIDTH)
        return jnp.concatenate([r, jnp.zeros((SUBLANES - 1, half * KV_WIDTH), F32)], axis=0)

    w1a = jnp.stack([w1_blocks(w1[i, :half]) for i in range(2)]).astype(BF16)
    w1b = jnp.stack([w1_blocks(w1[i, half:]) for i in range(2)]).astype(BF16)
    posr = jnp.stack([jnp.stack([pos_rows(pos[i, :half]), pos_rows(pos[i, half:])])
                      for i in range(2)]).astype(BF16)
    b1t = jnp.tile(b1, (1, KV_GROUPS))[:, None, :]
    w2blk = lambda w: group_diag(w).reshape(KV_GROUPS * CMP_HIDDEN, KV_WIDTH)
    return dict(w1a=w1a, w1b=w1b, pos=posr, b1=b1t,
                w2k=w2blk(w2[0]).astype(BF16), b2k=jnp.tile(b2[0], KV_GROUPS)[None, :],
                w2v=w2blk(w2[1]).astype(BF16), b2v=jnp.tile(b2[1], KV_GROUPS)[None, :])


CMP_PAD = 64
CMP_BAND_BACK = 56
CMP_BAND = CMP_BAND_BACK + T_CMP // CMP_STRIDE
CMP_BAND_DIST0 = CMP_BAND_BACK * CMP_STRIDE - (CMP_LEN - 1)
assert CMP_BAND_DIST0 + CMP_STRIDE >= FAR_DIST and CMP_BAND_BACK <= CMP_PAD
CMP_ROW_CHUNK = 128
SEL_RATIO = SEL_BLOCK // CMP_STRIDE


def _gate_column(head, branch):
    return head * 3 + branch


def _cmp_kernel(gate_ref, qa_ref, kc_ref, vct_ref, band_ref, wagg_ref, ocmp_ref, qaug_ref,
                st_ref, gt_ref):
    tq = qa_ref.shape[2]
    nb = kc_ref.shape[1]
    i = pl.program_id(1)
    g = pl.program_id(0)
    s0 = i * tq
    t = s0 + lax.broadcasted_iota(jnp.int32, (1, tq), 1)
    n_live = (i + 1) * (tq // CMP_STRIDE)
    band_row0 = pl.multiple_of(s0 // CMP_STRIDE - CMP_BAND_BACK + CMP_PAD, SUBLANES)
    gt_ref[...] = gate_ref[0].T
    st_ref[:, :CMP_PAD, :] = jnp.zeros((HEADS_PER_GROUP, CMP_PAD, tq), F32)
    has_block = t >= CMP_LEN - 1

    def body(rows):
        nsr = rows // SEL_RATIO
        kc = kc_ref[0, :rows, :]
        tail = rows - CMP_ROW_CHUNK
        dead = tail + lax.broadcasted_iota(jnp.int32, (CMP_ROW_CHUNK, 1), 0) >= n_live
        imp = jnp.zeros((rows, tq), F32)
        outs = []
        for h in range(HEADS_PER_GROUP):
            st_ref[h, CMP_PAD:CMP_PAD + rows, :] = _dot_nt(kc, qa_ref[0, h])
        for h in range(HEADS_PER_GROUP):
            st_ref[h, pl.ds(band_row0, CMP_BAND), :] += band_ref[h]
        for h in range(HEADS_PER_GROUP):
            last = jnp.where(dead, MASK_BIAS, st_ref[h, CMP_PAD + tail:CMP_PAD + rows, :])
            s = last if tail == 0 else jnp.concatenate(
                [st_ref[h, CMP_PAD:CMP_PAD + tail, :], last], axis=0)
            e = jnp.exp(s - jnp.max(s, axis=0, keepdims=True))
            denom = jnp.maximum(jnp.sum(e, axis=0, keepdims=True), 1.0)
            rinv = jnp.where(has_block, 1.0 / denom, 0.0)
            imp = imp + e * rinv
            gate = gt_ref[pl.ds(_gate_column(g * HEADS_PER_GROUP + h, 0), 1), :]
            outs.append((gate * rinv) * _dot(vct_ref[0, :, :rows], e.astype(BF16)))
        ocmp_ref[0] = jnp.concatenate(outs, axis=0).T.astype(BF16)

        p_sel = _dot_f32_left(wagg_ref[:nsr, :rows], imp)
        j = lax.broadcasted_iota(jnp.int32, (nsr, 1), 0)
        rel = jnp.right_shift(t, _log2(SEL_BLOCK)) - j
        forced = (j == 0) | ((rel >= 0) & (rel < N_LOCAL_SEL))
        score = jnp.where(forced, FORCE_SCORE, jnp.where(j * SEL_BLOCK <= t, p_sel, -1.0))
        jfull = jnp.broadcast_to(j, (nsr, tq)).astype(F32)
        sel = jnp.zeros((nsr, tq), F32)
        for _ in range(SEL_TOPK):
            m = jnp.max(score, axis=0, keepdims=True)
            first = jnp.min(jnp.where(score == m, jfull, float(nsr)), axis=0, keepdims=True)
            hit = jfull == first
            sel = jnp.where(hit, 1.0, sel)
            score = jnp.where(hit, -3e38, score)
        notsel = jnp.where(sel > 0.5, 0.0, MASK_BIAS)
        if nsr < LANES:
            notsel = jnp.concatenate(
                [notsel, jnp.full((LANES - nsr, tq), MASK_BIAS, F32)], axis=0)
        notsel = notsel.T.astype(BF16)
        for h in range(HEADS_PER_GROUP):
            qaug_ref[0, h, :, :LANES] = qa_ref[0, h]
            qaug_ref[0, h, :, LANES:] = notsel

    for rows in range(CMP_ROW_CHUNK, nb + 1, CMP_ROW_CHUNK):
        @pl.when((n_live > rows - CMP_ROW_CHUNK) & (n_live <= rows))
        def _():
            body(rows)


def _cmp_attention(gates, qa, kc, vct, band, wagg):
    b, _, s, _ = qa.shape
    nb = kc.shape[1]
    tq = T_CMP
    assert nb % CMP_ROW_CHUNK == 0
    grid = (KV_GROUPS, s // tq, b)
    return pl.pallas_call(
        _cmp_kernel, grid=grid,
        in_specs=[
            pl.BlockSpec((1, tq, LANES), lambda g, i, bi: (bi, i, 0)),
            pl.BlockSpec((1, HEADS_PER_GROUP, tq, LANES), lambda g, i, bi: (bi, g, i, 0)),
            pl.BlockSpec((1, nb, KV_WIDTH), lambda g, i, bi: (bi, 0, 0)),
            pl.BlockSpec((1, HEAD_DIM, nb), lambda g, i, bi: (bi, g, 0)),
            pl.BlockSpec((HEADS_PER_GROUP, CMP_BAND, tq), lambda g, i, bi: (g, 0, 0)),
            pl.BlockSpec(wagg.shape, lambda g, i, bi: (0, 0)),
        ],
        out_specs=(
            pl.BlockSpec((1, tq, HEADS_PER_GROUP * HEAD_DIM), lambda g, i, bi: (bi, i, g)),
            pl.BlockSpec((1, HEADS_PER_GROUP, tq, 2 * LANES), lambda g, i, bi: (bi, g, i, 0)),
        ),
        out_shape=(jax.ShapeDtypeStruct((b, s, NSA_WIDTH), BF16),
                   jax.ShapeDtypeStruct((b, NSA_HEADS, s, 2 * LANES), BF16)),
        scratch_shapes=[pltpu.VMEM((HEADS_PER_GROUP, CMP_PAD + nb, tq), F32),
                        pltpu.VMEM((LANES, tq), F32)],
        compiler_params=_params("parallel", "parallel", "parallel"), name="cmp_attention",
    )(gates, qa, kc, vct, band, wagg)


def _cmp_tables(rel_bias, nb):
    jj = np.arange(CMP_BAND)[:, None]
    ii = np.arange(T_CMP)[None, :]
    dist = CMP_BAND_DIST0 + ii - CMP_STRIDE * jj
    bidx = _BUCKETS[np.clip(dist, 0, None)]
    band = jnp.transpose(_bias_lookup(rel_bias, bidx) - rel_bias[REL_BUCKETS - 1], (2, 0, 1))
    band = jnp.where(jnp.asarray(dist >= 0)[None], band, MASK_BIAS)
    ns = nb // SEL_RATIO
    assert SEL_TOPK <= ns <= LANES
    wagg = np.zeros((LANES, nb), np.float32)
    for m, w in enumerate(SEL_AGG):
        for jb in range(ns):
            c = SEL_RATIO * jb + m - 1
            if 0 <= c < nb - 1:
                wagg[jb, c] = w
    return band, jnp.asarray(wagg, BF16)


N_NEAR = (FAR_DIST + T_ATT - 1) // T_ATT + 1
assert N_NEAR * T_ATT - (T_ATT - 1) >= FAR_DIST
WIN_TILES = (WINDOW - 1 + T_ATT - 1) // T_ATT
TOEPLITZ_ROWS = 8
assert max(N_NEAR, WIN_TILES + 1) <= TOEPLITZ_ROWS
FAR_KEYS = 4 * T_ATT


def _flash_kernel(gate_ref, q_ref, kt_ref, v_ref, gv_ref, o_ref, m_ref, acc_ref, nb_ref, *,
                  windowed, branch):
    t = T_ATT
    g = pl.program_id(1)
    i = pl.program_id(2)
    n_tab = WIN_TILES + 1 if windowed else N_NEAR

    @pl.when(i == 0)
    def _():
        for h in range(HEADS_PER_GROUP):
            for d in range(n_tab):
                gen = jnp.broadcast_to(gv_ref[0, h, d:d + 1, :], (t, 2 * t))
                nb_ref[h, d] = pltpu.roll(gen, 0, 1, stride=1, stride_axis=0)[:, :t]

    m_ref[...] = jnp.full(m_ref.shape, NEG_INIT, F32)
    acc_ref[...] = jnp.zeros(acc_ref.shape, F32)

    def tile_step(first_tile, n_tiles, dists=()):
        assert not dists or len(dists) == n_tiles
        nk = t * n_tiles
        bias, keep = None, None
        if dists:
            bias = lambda h: jnp.concatenate([nb_ref[h, d] for d in dists], axis=1)
            dist = (dists[0] * t + lax.broadcasted_iota(jnp.int32, (t, nk), 0)
                    - lax.broadcasted_iota(jnp.int32, (t, nk), 1))
            causal = dists[-1] == 0
            banded = windowed and dists[0] * t + t - 1 >= WINDOW
            assert not (causal and banded)
            if causal:
                keep = dist >= 0
            elif banded:
                keep = dist < WINDOW
        update(pl.multiple_of(first_tile * t, t), nk, slice(None), bias, keep)

    def update(start, nk, rows, bias, keep):
        heads = range(HEADS_PER_GROUP)
        kt = kt_ref[0, :, pl.ds(start, nk)]
        v = v_ref[0, 0, pl.ds(start, nk), :]
        s = [_dot(q_ref[0, h, rows, :], kt) for h in heads]
        if bias is not None:
            s = [s[h] + bias(h) for h in heads]
        if keep is not None:
            s = [jnp.where(keep, s[h], MASK_BIAS) for h in heads]
        m_prev = [m_ref[h, rows, :] for h in heads]
        m_new = [jnp.maximum(m_prev[h], jnp.max(s[h], axis=1, keepdims=True)) for h in heads]
        p = [jnp.exp(s[h] - jnp.tile(m_new[h], (1, nk // LANES))).astype(BF16) for h in heads]
        pv = [_dot(p[h], v) for h in heads]
        for h in heads:
            acc_ref[h, rows, :] = jnp.exp(m_prev[h] - m_new[h]) * acc_ref[h, rows, :] + pv[h]
            m_ref[h, rows, :] = m_new[h]

    def window_steps():
        half = t // 2
        assert WINDOW == t
        nk = 3 * half
        dist = (t + lax.broadcasted_iota(jnp.int32, (half, nk), 0)
                - lax.broadcasted_iota(jnp.int32, (half, nk), 1))
        keep = lax.bitcast_convert_type(dist, jnp.uint32) < WINDOW
        bias = lambda h: jnp.concatenate([nb_ref[h, 1, :half, :], nb_ref[h, 0, :half, :half]], axis=1)
        for part in range(2):
            start = pl.multiple_of((i - 1) * t + part * half, half)
            update(start, nk, slice(part * half, (part + 1) * half), bias, keep)

    def near_steps(first):
        @pl.when(i >= first)
        def _():
            d = first
            if d % 2:
                tile_step(i - d, 1, (d,))
                d -= 1
            for d in range(d, 0, -2):
                tile_step(i - d, 2, (d, d - 1))

        @pl.when(i < first)
        def _():
            for d in range(first - 1, 0, -1):
                @pl.when(i >= d)
                def _():
                    tile_step(i - d, 1, (d,))
        tile_step(i, 1, (0,))

    if windowed:
        @pl.when(i == 0)
        def _():
            tile_step(0, 1, (0,))

        @pl.when(i > 0)
        def _():
            window_steps()
    else:
        n_far = jnp.maximum(i - (N_NEAR - 1), 0)
        far_tiles = FAR_KEYS // t

        def far_body(j, carry):
            tile_step(j * far_tiles, far_tiles)
            return carry
        n_far_steps = jnp.right_shift(n_far, _log2(far_tiles))
        lax.fori_loop(0, n_far_steps, far_body, 0)
        done = n_far_steps * far_tiles
        run = far_tiles // 2
        while run >= 1:
            has_run = jnp.bitwise_and(n_far, run) != 0

            @pl.when(has_run)
            def _():
                tile_step(done, run)
            done = done + jnp.where(has_run, run, 0)
            run //= 2
        near_steps(N_NEAR - 1)

    lane = lax.broadcasted_iota(jnp.int32, (t, LANES), 1)
    gates = gate_ref[0]
    outs = []
    for h in range(HEADS_PER_GROUP):
        col = _gate_column(g * HEADS_PER_GROUP + h, branch)
        gate = jnp.sum(jnp.where(lane == col, gates, 0.0), axis=1, keepdims=True)
        acc = acc_ref[h]
        outs.append(acc[:, :HEAD_DIM] / jnp.maximum(acc[:, HEAD_DIM:HEAD_DIM + 1], 1.0) * gate)
    o_ref[0] = jnp.concatenate(outs, axis=1).astype(BF16)


def _flash(gates, q_aug, kt, v, gv, *, windowed, branch):
    b, _, s, kd = q_aug.shape
    assert kt.shape[1] == kd
    t = T_ATT
    grid = (b, KV_GROUPS, s // t)
    return pl.pallas_call(
        functools.partial(_flash_kernel, windowed=windowed, branch=branch), grid=grid,
        in_specs=[
            pl.BlockSpec((1, t, LANES), lambda bi, g, i: (bi, i, 0)),
            pl.BlockSpec((1, HEADS_PER_GROUP, t, kd), lambda bi, g, i: (bi, g, i, 0)),
            pl.BlockSpec((1, kd, s), lambda bi, g, i: (bi, 0, 0)),
            pl.BlockSpec((1, 1, s, LANES), lambda bi, g, i: (bi, g, 0, 0)),
            pl.BlockSpec((1, HEADS_PER_GROUP, TOEPLITZ_ROWS, 2 * t), lambda bi, g, i: (g, 0, 0, 0)),
        ],
        out_specs=pl.BlockSpec((1, t, HEADS_PER_GROUP * HEAD_DIM), lambda bi, g, i: (bi, i, g)),
        out_shape=jax.ShapeDtypeStruct((b, s, NSA_WIDTH), BF16),
        scratch_shapes=[pltpu.VMEM((HEADS_PER_GROUP, t, LANES), F32),
                        pltpu.VMEM((HEADS_PER_GROUP, t, LANES), F32),
                        pltpu.VMEM((HEADS_PER_GROUP, WIN_TILES + 1 if windowed else N_NEAR, t, t),
                                   F32)],
        compiler_params=_params("parallel", "parallel", "arbitrary"),
        name="win_attention" if windowed else "sel_attention",
    )(gates, q_aug, kt, v, gv)


def _toeplitz_generators(rel_bias):
    t = T_ATT
    d = np.arange(TOEPLITZ_ROWS)[:, None]
    y = np.arange(2 * t)[None, :]
    dist = np.where(y < t, d * t - y, d * t + 2 * t - y)
    tab = _bias_lookup(rel_bias, _BUCKETS[np.clip(dist, 0, None)]) - rel_bias[REL_BUCKETS - 1]
    tab = jnp.where(jnp.asarray(dist >= 0)[..., None], tab, 0.0)
    return jnp.transpose(tab, (2, 0, 1)).reshape(KV_GROUPS, HEADS_PER_GROUP, TOEPLITZ_ROWS, 2 * t)


def _pool_tile(x, halo, i, w, scale):
    t = x.shape[0]
    halo = jnp.where(i > 0, halo, 0.0)
    xs = jnp.concatenate([halo, x], axis=0)
    sums = []
    acc = xs
    span = 1
    for win in POOL_WINDOWS:
        while span < win:
            acc = acc + pltpu.roll(acc, span, 0)
            span *= 2
        sums.append(acc[HALO:])
    lane_grp = jnp.right_shift(lax.broadcasted_iota(jnp.int32, (t, POOL_WIDTH), 1), _log2(POOL_CH))
    tok = i * t + lax.broadcasted_iota(jnp.int32, (t, POOL_WIDTH), 0)
    ssum = sums[-1]
    win = jnp.full((t, POOL_WIDTH), POOL_WINDOWS[-1], jnp.int32)
    for gi in range(len(POOL_WINDOWS) - 2, -1, -1):
        ssum = jnp.where(lane_grp == gi, sums[gi], ssum)
        win = jnp.where(lane_grp == gi, POOL_WINDOWS[gi], win)
    cnt = jnp.minimum(tok + 1, win).astype(F32)
    dlt = (ssum / cnt - x).astype(BF16)
    return (_dot(dlt, w) * scale).astype(BF16)


def _sgu_tile(u, v, norm_g, norm_b, ws_cat, bs_exp):
    t = u.shape[0]
    v = _gelu(v)
    mu = jnp.mean(v, axis=-1, keepdims=True)
    var = jnp.mean(jnp.square(v - mu), axis=-1, keepdims=True)
    vn = (v - mu) * lax.rsqrt(var + EPS) * norm_g + norm_b
    lane_grp = jnp.right_shift(lax.broadcasted_iota(jnp.int32, (SGU_CHUNK, SGU_WIDTH), 1),
                               _log2(SGU_CH))
    r = lax.broadcasted_iota(jnp.int32, (SGU_CHUNK, SGU_GROUPS * SGU_CHUNK), 0)
    c = lax.broadcasted_iota(jnp.int32, (SGU_CHUNK, SGU_GROUPS * SGU_CHUNK), 1) & (SGU_CHUNK - 1)
    ws = jnp.where(c <= r, ws_cat, 0.0).astype(BF16)
    outs = []
    for n in range(t // SGU_CHUNK):
        vc = vn[n * SGU_CHUNK:(n + 1) * SGU_CHUNK]
        stacked = jnp.concatenate(
            [jnp.where(lane_grp == gi, vc, 0.0) for gi in range(SGU_GROUPS)], axis=0).astype(BF16)
        mixed = _dot(ws, stacked) + bs_exp
        outs.append((_gelu(u[n * SGU_CHUNK:(n + 1) * SGU_CHUNK]) * mixed).astype(BF16))
    return jnp.concatenate(outs, axis=0)


def _out_proj_kernel(oc_ref, os_ref, ow_ref, pin_ref, halo_ref, u_ref, v_ref, pw_ref, ps_ref,
                     ng_ref, nb_ref, ws_ref, bs_ref, w_ref, g_ref, x_ref, y_ref):
    a = (oc_ref[0].astype(F32) + os_ref[0].astype(F32) + ow_ref[0].astype(F32)).astype(BF16)
    pool = _pool_tile(pin_ref[0], halo_ref[0], pl.program_id(1), pw_ref[...], ps_ref[...])
    sgu = _sgu_tile(u_ref[0], v_ref[0], ng_ref[...], nb_ref[...], ws_ref[...], bs_ref[...])
    mix = (_dot(a, w_ref[:NSA_WIDTH, :])
           + _dot(pool, w_ref[NSA_WIDTH:NSA_WIDTH + POOL_WIDTH, :])
           + _dot(sgu, w_ref[NSA_WIDTH + POOL_WIDTH:, :]))
    y_ref[0] = x_ref[0] + _rms(mix, g_ref[...])


def _out_proj(o_cmp, o_sel, o_win, pin, u, v, pool_w, pool_scale, sgu_g, sgu_b, ws_cat, bs_exp,
              w, g, x, l):
    b, s, d = x.shape
    t = T_OUT
    assert t % SGU_CHUNK == 0 and t % HALO == 0
    row = lambda n: pl.BlockSpec((1, t, n), lambda bi, i: (bi, i, 0))
    halo = pl.BlockSpec((1, HALO, POOL_WIDTH),
                        lambda bi, i: (bi, jnp.maximum(i * (t // HALO) - 1, 0), 0))
    layer = [_layer_spec(a, l) for a in (pool_w, pool_scale, sgu_g, sgu_b, ws_cat, bs_exp, w, g)]
    return pl.pallas_call(
        _out_proj_kernel, grid=(b, s // t),
        in_specs=[row(NSA_WIDTH), row(NSA_WIDTH), row(NSA_WIDTH), row(POOL_WIDTH), halo,
                  row(SGU_WIDTH), row(SGU_WIDTH)] + layer + [row(d)],
        out_specs=row(d), out_shape=jax.ShapeDtypeStruct((b, s, d), F32),
        compiler_params=_params("parallel", "parallel"), name="out_proj",
    )(o_cmp, o_sel, o_win, pin, pin, u, v, pool_w, pool_scale, sgu_g, sgu_b, ws_cat, bs_exp, w, g, x)


def _mem_kv_kernel(mem_ref, g_ref, wk_ref, wv_ref, kt_ref, v_ref):
    m = _rms(mem_ref[0], g_ref[...]).astype(BF16)
    kt_ref[0] = _dot(m, wk_ref[...]).T.astype(BF16)
    v_ref[0] = _dot(m, wv_ref[...]).astype(BF16)


def _mem_kv(mem, g, wk, wv, l):
    b, m, d = mem.shape
    blk = pl.BlockSpec((1, m, d), lambda bi: (bi, 0, 0))
    blk_t = pl.BlockSpec((1, d, m), lambda bi: (bi, 0, 0))
    return pl.pallas_call(
        _mem_kv_kernel, grid=(b,),
        in_specs=[blk] + [_layer_spec(a, l) for a in (g, wk, wv)],
        out_specs=(blk_t, blk),
        out_shape=(jax.ShapeDtypeStruct((b, d, m), BF16), jax.ShapeDtypeStruct((b, m, d), BF16)),
        compiler_params=_params("parallel"), name="mem_kv",
    )(mem, g, wk, wv)


def _mem_attn_kernel(x_ref, gpre_ref, wq_ref, kt_ref, v_ref, wo_ref, gpost_ref, y_ref):
    x = x_ref[0]
    h = _rms(x, gpre_ref[...]).astype(BF16)
    q = (_dot(h, wq_ref[...]) * (MEM_HEAD_DIM ** -0.5)).astype(BF16)
    heads = range(MEM_HEADS)
    sl = [slice(hd * MEM_HEAD_DIM, (hd + 1) * MEM_HEAD_DIM) for hd in heads]
    s = [_dot(q[:, sl[hd]], kt_ref[0, sl[hd], :]) for hd in heads]
    e = [jnp.exp(s[hd] - jnp.max(s[hd], axis=-1, keepdims=True)) for hd in heads]
    p = [(e[hd] * (1.0 / jnp.sum(e[hd], axis=-1, keepdims=True))).astype(BF16) for hd in heads]
    o = jnp.concatenate([_dot(p[hd], v_ref[0, :, sl[hd]]).astype(BF16) for hd in heads], axis=1)
    y_ref[0] = x + _rms(_dot(o, wo_ref[...]), gpost_ref[...])


def _mem_attn(x, gpre, wq, kt, v, wo, gpost, l):
    b, s, d = x.shape
    m = v.shape[1]
    t = T_MEM
    row = pl.BlockSpec((1, t, d), lambda bi, i: (bi, i, 0))
    return pl.pallas_call(
        _mem_attn_kernel, grid=(b, s // t),
        in_specs=[row, _layer_spec(gpre, l), _layer_spec(wq, l),
                  pl.BlockSpec((1, d, m), lambda bi, i: (bi, 0, 0)),
                  pl.BlockSpec((1, m, d), lambda bi, i: (bi, 0, 0)),
                  _layer_spec(wo, l), _layer_spec(gpost, l)],
        out_specs=row, out_shape=jax.ShapeDtypeStruct((b, s, d), F32),
        compiler_params=_params("parallel", "parallel"), name="mem_attention",
    )(x, gpre, wq, kt, v, wo, gpost)


def _ffn_kernel(x_ref, halo_ref, gpre_ref, wup_ref, cw_ref, cb_ref, wd_ref, gpost_ref, y_ref,
                act_ref):
    i = pl.program_id(1)
    f = wd_ref.shape[0]
    halo = jnp.where(i > 0, halo_ref[0], 0.0)
    h = _rms(jnp.concatenate([halo, x_ref[0]], axis=0), gpre_ref[...]).astype(BF16)

    def conv(off):
        a = _dot(h, wup_ref[:, off:off + F_CHUNK])
        cw = cw_ref[:, off:off + F_CHUNK]
        return (cw[0:1] * pltpu.roll(a, 2, 0) + cw[1:2] * pltpu.roll(a, 1, 0) + cw[2:3] * a
                + cb_ref[:, off:off + F_CHUNK])

    for off in range(0, f, F_CHUNK):
        act = _gelu(conv(off)) * conv(f + off)
        act_ref[:, off:off + F_CHUNK] = act[HALO:].astype(BF16)
    y_ref[0] = x_ref[0] + _rms(_dot(act_ref[...], wd_ref[...]), gpost_ref[...])


def _ffn(x, gpre, w_up, conv_w, conv_b, w_down, gpost, l):
    b, s, d = x.shape
    f = w_down.shape[1]
    t = T_FFN
    assert f % F_CHUNK == 0
    resident = lambda a: _layer_spec(a, l, pipeline_mode=pl.Buffered(1))
    return pl.pallas_call(
        _ffn_kernel, grid=(b, s // t),
        in_specs=[
            pl.BlockSpec((1, t, d), lambda bi, i: (bi, i, 0)),
            pl.BlockSpec((1, HALO, d), lambda bi, i: (bi, jnp.maximum(i * (t // HALO) - 1, 0), 0)),
            resident(gpre), resident(w_up), resident(conv_w), resident(conv_b), resident(w_down),
            resident(gpost),
        ],
        out_specs=pl.BlockSpec((1, t, d), lambda bi, i: (bi, i, 0)),
        out_shape=jax.ShapeDtypeStruct((b, s, d), F32),
        scratch_shapes=[pltpu.VMEM((t, f), BF16)],
        compiler_params=_params("parallel", "parallel"), name="conv_ffn",
    )(x, x, gpre, w_up, conv_w, conv_b, w_down, gpost)


def kernel(x, mem, rel_bias, mix_norm_pre, mix_norm_post, w_in, cmp_pos, cmp_w1, cmp_b1, cmp_w2,
           cmp_b2, pool_w, pool_scale, sgu_norm_g, sgu_norm_b, sgu_w, sgu_b, w_out, mem_norm_pre,
           mem_norm_kv, mem_norm_post, w_mq, w_mk, w_mv, w_mo, ffn_norm_pre, ffn_norm_post, w_up,
           conv_w, conv_b, w_down):
    depth = w_in.shape[0]
    s = x.shape[1]
    assert s % max(T_IN, T_CMP, T_ATT, T_OUT, T_MEM, T_FFN) == 0
    nb = s // CMP_STRIDE

    w_in_p = _in_proj_weights(w_in)
    band, wagg = _cmp_tables(rel_bias, nb)
    toep = _toeplitz_generators(rel_bias)
    row = lambda a: a[:, None, :]
    bf = lambda a: a.astype(BF16)
    w_out_b, w_mq_b, w_mk_b, w_mv_b, w_mo_b = bf(w_out), bf(w_mq), bf(w_mk), bf(w_mv), bf(w_mo)
    w_up_b, w_down_b = bf(w_up), bf(w_down)
    g_mix_pre, g_mix_post = row(mix_norm_pre), row(mix_norm_post)
    g_mem_pre, g_mem_kv, g_mem_post = row(mem_norm_pre), row(mem_norm_kv), row(mem_norm_post)
    g_ffn_pre, g_ffn_post, conv_b_r = row(ffn_norm_pre), row(ffn_norm_post), row(conv_b)
    cw = jax.vmap(_compress_weights)(cmp_pos, cmp_w1, cmp_b1, cmp_w2, cmp_b2)
    n_pool = len(POOL_WINDOWS)
    pool_blk = bf(jnp.einsum("lgcd,gh->lgchd", pool_w, jnp.eye(n_pool, dtype=F32)).reshape(
        depth, POOL_WIDTH, POOL_WIDTH))
    pool_scale_r = row(pool_scale)
    sgu_g, sgu_bias = row(sgu_norm_g), row(sgu_norm_b)
    ws_cat = jnp.concatenate([sgu_w[:, gi] for gi in range(SGU_GROUPS)], axis=2)
    bs_exp = jnp.repeat(jnp.swapaxes(sgu_b, 1, 2), SGU_CH, axis=2)

    for l in range(depth):
        qa, ksel_t, kwin_t, kcmp, vcmp, vsel, vwin, gates, pin, u, v = _in_proj(
            x, g_mix_pre, w_in_p, l)
        kc, vct = _compress(kcmp, vcmp, cw, l)
        o_cmp, q_aug = _cmp_attention(gates, qa, kc, vct, band, wagg)
        o_sel = _flash(gates, q_aug, ksel_t, vsel, toep, windowed=False, branch=1)
        o_win = _flash(gates, qa, kwin_t, vwin, toep, windowed=True, branch=2)
        x = _out_proj(o_cmp, o_sel, o_win, pin, u, v, pool_blk, pool_scale_r, sgu_g, sgu_bias,
                      ws_cat, bs_exp, w_out_b, g_mix_post, x, l)
        km, vm = _mem_kv(mem, g_mem_kv, w_mk_b, w_mv_b, l)
        x = _mem_attn(x, g_mem_pre, w_mq_b, km, vm, w_mo_b, g_mem_post, l)
        x = _ffn(x, g_ffn_pre, w_up_b, conv_w, conv_b_r, w_down_b, g_ffn_post, l)
    return x
```

```python
import functools
import math

import numpy as np
import jax
import jax.numpy as jnp
from jax import lax
from jax.experimental import pallas as pl
from jax.experimental.pallas import tpu as pltpu

F32 = jnp.float32
BF16 = jnp.bfloat16

NSA_WIDTH = 512
POOL_WIDTH = 256
SGU_WIDTH = 256
NSA_HEADS = 8
KV_GROUPS = 2
HEADS_PER_GROUP = 4
HEAD_DIM = 64
KV_WIDTH = 128
GATE_WIDTH = 24
CMP_STRIDE = 16
CMP_LEN = 32
CMP_HIDDEN = 128
SEL_BLOCK = 64
SEL_TOPK = 16
N_LOCAL_SEL = 2
SEL_AGG = (1.0, 2.0, 2.0, 2.0, 1.0)
FORCE_SCORE = 1e9
WINDOW = 512
REL_BUCKETS = 32
REL_MAX_DIST = 1024
POOL_WINDOWS = (2, 4, 8, 16)
POOL_CH = 64
SGU_GROUPS = 4
SGU_CH = 64
SGU_CHUNK = 128
MEM_HEADS = 4
MEM_HEAD_DIM = 256
EPS = 1e-6

LANES = 128
SUBLANES = 8
VMEM_LIMIT = 56 * 1024 * 1024

T_IN = 512
T_CMP = 512
T_ATT = 512
T_OUT = 512
T_MEM = 512
T_FFN = 1024
F_CHUNK = 256
HALO = 16

MASK_BIAS = -float(2 ** 30)
NEG_INIT = -1e30

_Q_OFF = 0
_KCMP_OFF = NSA_WIDTH
_VCMP_OFF = _KCMP_OFF + KV_WIDTH
_KSEL_OFF = _VCMP_OFF + KV_WIDTH
_VSEL_OFF = _KSEL_OFF + KV_WIDTH
_KWIN_OFF = _VSEL_OFF + KV_WIDTH
_VWIN_OFF = _KWIN_OFF + KV_WIDTH
_GATE_OFF = _VWIN_OFF + KV_WIDTH
_POOL_OFF = 1536
_U_OFF = _POOL_OFF + POOL_WIDTH
_V_OFF = _U_OFF + SGU_WIDTH
IN_COLS = _V_OFF + SGU_WIDTH
MXU_COLS = 256


def _bucket_table(n_max):
    n = np.arange(n_max)
    max_exact = REL_BUCKETS // 2
    nf = np.maximum(n, 1).astype(np.float64)
    large = max_exact + (np.log(nf / max_exact) / math.log(REL_MAX_DIST / max_exact)
                         * (REL_BUCKETS - max_exact)).astype(np.int32)
    large = np.minimum(large, REL_BUCKETS - 1)
    return np.where(n < max_exact, n, large).astype(np.int32)


_BUCKETS = _bucket_table(16384)
FAR_DIST = int(np.argmax(_BUCKETS == REL_BUCKETS - 1))


def _bias_lookup(rel_bias, buckets):
    onehot = jnp.asarray(np.eye(REL_BUCKETS, dtype=np.float32)[buckets])
    return jnp.einsum("...k,kh->...h", onehot, rel_bias, precision=lax.Precision.HIGHEST)


def _in_proj_weights(w_in):
    depth, d, _ = w_in.shape
    split = _GATE_OFF + GATE_WIDTH
    assert _POOL_OFF >= split and _POOL_OFF % MXU_COLS == 0
    return jnp.concatenate(
        [w_in[:, :, :split].astype(BF16), jnp.zeros((depth, d, _POOL_OFF - split), BF16),
         w_in[:, :, split:].astype(BF16)], axis=2)


def _log2(n):
    assert n & (n - 1) == 0
    return n.bit_length() - 1


def _params(*sem):
    return pltpu.CompilerParams(dimension_semantics=sem, vmem_limit_bytes=VMEM_LIMIT)


def _layer_spec(a, l, **kw):
    return pl.BlockSpec((None,) + a.shape[1:], lambda *_: (l,) + (0,) * (a.ndim - 1), **kw)


def _rms(x, g):
    return x * lax.rsqrt(jnp.mean(x * x, axis=-1, keepdims=True) + EPS) * g


def _gelu(x):
    c = math.sqrt(2.0 / math.pi)
    return x * (0.5 * (1.0 + jnp.tanh(c * (x + 0.044715 * (x * x * x)))))


def _dot(a, b):
    return jnp.dot(a, b, preferred_element_type=F32)


def _dot_nt(a, b):
    return lax.dot_general(a, b, (((1,), (1,)), ((), ())), preferred_element_type=F32)


def _split3(x):
    hi = x.astype(BF16)
    r1 = x - hi.astype(F32)
    mid = r1.astype(BF16)
    lo = (r1 - mid.astype(F32)).astype(BF16)
    return hi, mid, lo


def _dot_f32_left(w_bf16, x):
    hi, mid, lo = _split3(x)
    return _dot(w_bf16, hi) + _dot(w_bf16, mid) + _dot(w_bf16, lo)


def _in_proj_kernel(x_ref, g_ref, w_ref, qa_ref, ksel_ref, kwin_ref, kcmp_ref, vcmp_ref,
                    vsel_ref, vwin_ref, gate_ref, pool_ref, u_ref, v_ref):
    t = x_ref.shape[1]
    h = _rms(x_ref[0], g_ref[...]).astype(BF16)

    def proj(off):
        assert off % MXU_COLS == 0
        return _dot(h, w_ref[:, off:off + MXU_COLS])

    zeros = jnp.zeros((t, HEAD_DIM), F32)

    def group_slab(piece, g):
        return jnp.concatenate([piece, zeros] if g == 0 else [zeros, piece], axis=1)

    scale = HEAD_DIM ** -0.5
    for g in range(KV_GROUPS):
        qg = proj(_Q_OFF + g * MXU_COLS) * scale
        for hg in range(HEADS_PER_GROUP):
            qa_ref[0, g * HEADS_PER_GROUP + hg] = group_slab(
                qg[:, hg * HEAD_DIM:(hg + 1) * HEAD_DIM], g).astype(BF16)

    lane = lax.broadcasted_iota(jnp.int32, (t, LANES), 1)
    kv_cmp = proj(_KCMP_OFF)
    kcmp_ref[0] = kv_cmp[:, :KV_WIDTH].astype(BF16)
    vcmp_ref[0] = kv_cmp[:, KV_WIDTH:].astype(BF16)
    tok = pl.program_id(1) * t + lax.broadcasted_iota(jnp.int32, (LANES, t), 1)
    blk_id = lax.broadcasted_iota(jnp.int32, (LANES, t), 0)
    ksel_ref[0, LANES:, :] = jnp.where(
        jnp.right_shift(tok, _log2(SEL_BLOCK)) == blk_id, 1.0, 0.0).astype(BF16)

    def keys_values(off, v_out):
        kv = proj(off)
        for g in range(KV_GROUPS):
            vg = group_slab(kv[:, KV_WIDTH + g * HEAD_DIM:KV_WIDTH + (g + 1) * HEAD_DIM], 0)
            v_out[0, g] = jnp.where(lane == HEAD_DIM, 1.0, vg).astype(BF16)
        return kv[:, :KV_WIDTH].T.astype(BF16)

    ksel_ref[0, :LANES, :] = keys_values(_KSEL_OFF, vsel_ref)
    kwin_ref[0] = keys_values(_KWIN_OFF, vwin_ref)
    gate_ref[0] = jax.nn.sigmoid(proj(_GATE_OFF)[:, :LANES])
    pool_ref[0] = proj(_POOL_OFF)
    u_ref[0] = proj(_U_OFF)
    v_ref[0] = proj(_V_OFF)


def _in_proj(x, g, w, l):
    b, s, d = x.shape
    t = T_IN
    grid = (b, s // t)
    row3 = lambda n: pl.BlockSpec((1, t, n), lambda bi, i: (bi, i, 0))
    out_shape = (
        jax.ShapeDtypeStruct((b, NSA_HEADS, s, LANES), BF16),
        jax.ShapeDtypeStruct((b, 2 * LANES, s), BF16),
        jax.ShapeDtypeStruct((b, LANES, s), BF16),
        jax.ShapeDtypeStruct((b, s, LANES), BF16),
        jax.ShapeDtypeStruct((b, s, LANES), BF16),
        jax.ShapeDtypeStruct((b, KV_GROUPS, s, LANES), BF16),
        jax.ShapeDtypeStruct((b, KV_GROUPS, s, LANES), BF16),
        jax.ShapeDtypeStruct((b, s, LANES), F32),
        jax.ShapeDtypeStruct((b, s, POOL_WIDTH), F32),
        jax.ShapeDtypeStruct((b, s, SGU_WIDTH), F32),
        jax.ShapeDtypeStruct((b, s, SGU_WIDTH), F32),
    )
    out_specs = (
        pl.BlockSpec((1, NSA_HEADS, t, LANES), lambda bi, i: (bi, 0, i, 0)),
        pl.BlockSpec((1, 2 * LANES, t), lambda bi, i: (bi, 0, i)),
        pl.BlockSpec((1, LANES, t), lambda bi, i: (bi, 0, i)),
        row3(LANES), row3(LANES),
        pl.BlockSpec((1, KV_GROUPS, t, LANES), lambda bi, i: (bi, 0, i, 0)),
        pl.BlockSpec((1, KV_GROUPS, t, LANES), lambda bi, i: (bi, 0, i, 0)),
        row3(LANES), row3(POOL_WIDTH), row3(SGU_WIDTH), row3(SGU_WIDTH),
    )
    return pl.pallas_call(
        _in_proj_kernel, grid=grid,
        in_specs=[row3(d), _layer_spec(g, l), _layer_spec(w, l)],
        out_specs=out_specs, out_shape=out_shape,
        compiler_params=_params("parallel", "parallel"), name="in_proj",
    )(x, g, w)


def _compress_kernel(rk_ref, rv_ref, w1a_ref, w1b_ref, pos_ref, b1_ref, w2k_ref, b2k_ref,
                     w2v_ref, b2v_ref, kc_ref, vct_ref):
    nb = rk_ref.shape[1]
    keep = lax.broadcasted_iota(jnp.int32, (nb, LANES), 0) < nb - 1

    def hidden(r, which):
        a = _dot(r, w1a_ref[which])
        bm = _dot(r, w1b_ref[which])
        pre = a + pltpu.roll(bm, nb - 1, 0)
        posb = (_dot(pos_ref[which, 0], w1a_ref[which])[0:1]
                + _dot(pos_ref[which, 1], w1b_ref[which])[0:1])
        return _gelu(pre + posb + b1_ref[which]).astype(BF16)

    kc = _dot(hidden(rk_ref[0], 0), w2k_ref[...]) + b2k_ref[...]
    kc_ref[0] = jnp.where(keep, kc, 0.0).astype(BF16)
    vc = _dot(hidden(rv_ref[0], 1), w2v_ref[...]) + b2v_ref[...]
    vct_ref[0] = jnp.where(keep, vc, 0.0).T.astype(BF16)


def _compress(kcmp, vcmp, cw, l):
    b, s, _ = kcmp.shape
    nb = s // CMP_STRIDE
    rk = kcmp.reshape(b, nb, CMP_STRIDE * KV_WIDTH)
    rv = vcmp.reshape(b, nb, CMP_STRIDE * KV_WIDTH)
    ws = (cw["w1a"], cw["w1b"], cw["pos"], cw["b1"], cw["w2k"], cw["b2k"], cw["w2v"], cw["b2v"])
    return pl.pallas_call(
        _compress_kernel, grid=(b,),
        in_specs=[pl.BlockSpec((1, nb, rk.shape[2]), lambda bi: (bi, 0, 0)),
                  pl.BlockSpec((1, nb, rv.shape[2]), lambda bi: (bi, 0, 0))]
                 + [_layer_spec(a, l) for a in ws],
        out_specs=(pl.BlockSpec((1, nb, KV_WIDTH), lambda bi: (bi, 0, 0)),
                   pl.BlockSpec((1, KV_WIDTH, nb), lambda bi: (bi, 0, 0))),
        out_shape=(jax.ShapeDtypeStruct((b, nb, KV_WIDTH), BF16),
                   jax.ShapeDtypeStruct((b, KV_WIDTH, nb), BF16)),
        compiler_params=_params("parallel"), name="compress",
    )(rk, rv, *ws)


def _compress_weights(pos, w1, b1, w2, b2):
    half = CMP_LEN // 2

    def group_diag(w):
        z = jnp.zeros_like(w)
        return jnp.stack([jnp.concatenate([w if p == g else z for g in range(KV_GROUPS)], axis=-1)
                          for p in range(KV_GROUPS)], axis=-3)

    def w1_blocks(w):
        return group_diag(w).reshape(half * KV_WIDTH, KV_GROUPS * CMP_HIDDEN)

    def pos_rows(p):
        r = jnp.broadcast_to(p[:, None, :], (half, KV_GROUPS, HEAD_DIM)).reshape(1, half * KV_WIDTH)
        return jnp.concatenate([r, jnp.zeros((SUBLANES - 1, half * KV_WIDTH), F32)], axis=0)

    w1a = jnp.stack([w1_blocks(w1[i, :half]) for i in range(2)]).astype(BF16)
    w1b = jnp.stack([w1_blocks(w1[i, half:]) for i in range(2)]).astype(BF16)
    posr = jnp.stack([jnp.stack([pos_rows(pos[i, :half]), pos_rows(pos[i, half:])])
                      for i in range(2)]).astype(BF16)
    b1t = jnp.tile(b1, (1, KV_GROUPS))[:, None, :]
    w2blk = lambda w: group_diag(w).reshape(KV_GROUPS * CMP_HIDDEN, KV_WIDTH)
    return dict(w1a=w1a, w1b=w1b, pos=posr, b1=b1t,
                w2k=w2blk(w2[0]).astype(BF16), b2k=jnp.tile(b2[0], KV_GROUPS)[None, :],
                w2v=w2blk(w2[1]).astype(BF16), b2v=jnp.tile(b2[1], KV_GROUPS)[None, :])


CMP_PAD = 64
CMP_BAND_BACK = 56
CMP_BAND = CMP_BAND_BACK + T_CMP // CMP_STRIDE
CMP_BAND_DIST0 = CMP_BAND_BACK * CMP_STRIDE - (CMP_LEN - 1)
assert CMP_BAND_DIST0 + CMP_STRIDE >= FAR_DIST and CMP_BAND_BACK <= CMP_PAD
CMP_ROW_CHUNK = 128
SEL_RATIO = SEL_BLOCK // CMP_STRIDE


def _gate_column(head, branch):
    return head * 3 + branch


def _cmp_kernel(gate_ref, qa_ref, kc_ref, vct_ref, band_ref, wagg_ref, ocmp_ref, qaug_ref,
                st_ref, gt_ref, sel_ref):
    tq = qa_ref.shape[2]
    nb = kc_ref.shape[1]
    i = pl.program_id(1)
    g = pl.program_id(0)
    s0 = i * tq
    t = s0 + lax.broadcasted_iota(jnp.int32, (1, tq), 1)
    n_live = (i + 1) * (tq // CMP_STRIDE)
    band_row0 = pl.multiple_of(s0 // CMP_STRIDE - CMP_BAND_BACK + CMP_PAD, SUBLANES)
    gt_ref[...] = gate_ref[0].T
    st_ref[:, :CMP_PAD, :] = jnp.zeros((HEADS_PER_GROUP, CMP_PAD, tq), F32)
    has_block = t >= CMP_LEN - 1

    def body(rows):
        nsr = rows // SEL_RATIO
        kc = kc_ref[0, :rows, :]
        tail = rows - CMP_ROW_CHUNK
        dead = tail + lax.broadcasted_iota(jnp.int32, (CMP_ROW_CHUNK, 1), 0) >= n_live
        imp = jnp.zeros((rows, tq), F32)
        outs = []
        for h in range(HEADS_PER_GROUP):
            st_ref[h, CMP_PAD:CMP_PAD + rows, :] = _dot_nt(kc, qa_ref[0, h])
        for h in range(HEADS_PER_GROUP):
            st_ref[h, pl.ds(band_row0, CMP_BAND), :] += band_ref[h]
        for h in range(HEADS_PER_GROUP):
            last = jnp.where(dead, MASK_BIAS, st_ref[h, CMP_PAD + tail:CMP_PAD + rows, :])
            s = last if tail == 0 else jnp.concatenate(
                [st_ref[h, CMP_PAD:CMP_PAD + tail, :], last], axis=0)
            e = jnp.exp(s - jnp.max(s, axis=0, keepdims=True))
            denom = jnp.maximum(jnp.sum(e, axis=0, keepdims=True), 1.0)
            rinv = jnp.where(has_block, 1.0 / denom, 0.0)
            imp = imp + e * rinv
            gate = gt_ref[pl.ds(_gate_column(g * HEADS_PER_GROUP + h, 0), 1), :]
            outs.append((gate * rinv) * _dot(vct_ref[0, :, :rows], e.astype(BF16)))
        ocmp_ref[0] = jnp.concatenate(outs, axis=0).T.astype(BF16)

        p_sel = _dot_f32_left(wagg_ref[:nsr, :rows], imp)
        j = lax.broadcasted_iota(jnp.int32, (nsr, 1), 0)
        rel = jnp.right_shift(t, _log2(SEL_BLOCK)) - j
        forced = (j == 0) | ((rel >= 0) & (rel < N_LOCAL_SEL))
        jfull = jnp.broadcast_to(j, (nsr, tq)).astype(F32)
        score = jnp.where(forced, FORCE_SCORE - 1e3 * jfull,
                          jnp.where(j * SEL_BLOCK <= t, p_sel, -1.0))
        taken_mark = -3e38

        def top_k(score, break_ties):
            for _ in range(SEL_TOPK):
                hit = score == jnp.max(score, axis=0, keepdims=True)
                if break_ties:
                    hit = jfull == jnp.min(jnp.where(hit, jfull, float(nsr)), axis=0, keepdims=True)
                score = jnp.where(hit, taken_mark, score)
            return score

        sel_ref[:nsr, :] = top_k(score, False)
        crowded = t >= (SEL_TOPK - 1) * SEL_BLOCK
        taken = jnp.sum(jnp.where(sel_ref[:nsr, :] == taken_mark, 1.0, 0.0), axis=0, keepdims=True)
        tied = jnp.max(jnp.where(crowded & (taken != SEL_TOPK), 1.0, 0.0)) > 0.0

        @pl.when(tied)
        def _():
            sel_ref[:nsr, :] = top_k(score, True)
        notsel = jnp.where(sel_ref[:nsr, :] == taken_mark, 0.0, MASK_BIAS)
        if nsr < LANES:
            notsel = jnp.concatenate(
                [notsel, jnp.full((LANES - nsr, tq), MASK_BIAS, F32)], axis=0)
        notsel = notsel.T.astype(BF16)
        for h in range(HEADS_PER_GROUP):
            qaug_ref[0, h, :, :LANES] = qa_ref[0, h]
            qaug_ref[0, h, :, LANES:] = notsel

    for rows in range(CMP_ROW_CHUNK, nb + 1, CMP_ROW_CHUNK):
        @pl.when((n_live > rows - CMP_ROW_CHUNK) & (n_live <= rows))
        def _():
            body(rows)


def _cmp_attention(gates, qa, kc, vct, band, wagg):
    b, _, s, _ = qa.shape
    nb = kc.shape[1]
    tq = T_CMP
    assert nb % CMP_ROW_CHUNK == 0
    grid = (KV_GROUPS, s // tq, b)
    return pl.pallas_call(
        _cmp_kernel, grid=grid,
        in_specs=[
            pl.BlockSpec((1, tq, LANES), lambda g, i, bi: (bi, i, 0)),
            pl.BlockSpec((1, HEADS_PER_GROUP, tq, LANES), lambda g, i, bi: (bi, g, i, 0)),
            pl.BlockSpec((1, nb, KV_WIDTH), lambda g, i, bi: (bi, 0, 0)),
            pl.BlockSpec((1, HEAD_DIM, nb), lambda g, i, bi: (bi, g, 0)),
            pl.BlockSpec((HEADS_PER_GROUP, CMP_BAND, tq), lambda g, i, bi: (g, 0, 0)),
            pl.BlockSpec(wagg.shape, lambda g, i, bi: (0, 0)),
        ],
        out_specs=(
            pl.BlockSpec((1, tq, HEADS_PER_GROUP * HEAD_DIM), lambda g, i, bi: (bi, i, g)),
            pl.BlockSpec((1, HEADS_PER_GROUP, tq, 2 * LANES), lambda g, i, bi: (bi, g, i, 0)),
        ),
        out_shape=(jax.ShapeDtypeStruct((b, s, NSA_WIDTH), BF16),
                   jax.ShapeDtypeStruct((b, NSA_HEADS, s, 2 * LANES), BF16)),
        scratch_shapes=[pltpu.VMEM((HEADS_PER_GROUP, CMP_PAD + nb, tq), F32),
                        pltpu.VMEM((LANES, tq), F32), pltpu.VMEM((LANES, tq), F32)],
        compiler_params=_params("parallel", "parallel", "parallel"), name="cmp_attention",
    )(gates, qa, kc, vct, band, wagg)


def _cmp_tables(rel_bias, nb):
    jj = np.arange(CMP_BAND)[:, None]
    ii = np.arange(T_CMP)[None, :]
    dist = CMP_BAND_DIST0 + ii - CMP_STRIDE * jj
    bidx = _BUCKETS[np.clip(dist, 0, None)]
    band = jnp.transpose(_bias_lookup(rel_bias, bidx) - rel_bias[REL_BUCKETS - 1], (2, 0, 1))
    band = jnp.where(jnp.asarray(dist >= 0)[None], band, MASK_BIAS)
    ns = nb // SEL_RATIO
    assert SEL_TOPK <= ns <= LANES
    wagg = np.zeros((LANES, nb), np.float32)
    for m, w in enumerate(SEL_AGG):
        for jb in range(ns):
            c = SEL_RATIO * jb + m - 1
            if 0 <= c < nb - 1:
                wagg[jb, c] = w
    return band, jnp.asarray(wagg, BF16)


N_NEAR = (FAR_DIST + T_ATT - 1) // T_ATT + 1
assert N_NEAR * T_ATT - (T_ATT - 1) >= FAR_DIST
WIN_TILES = (WINDOW - 1 + T_ATT - 1) // T_ATT
TOEPLITZ_ROWS = 8
assert max(N_NEAR, WIN_TILES + 1) <= TOEPLITZ_ROWS
FAR_KEYS = 4 * T_ATT


def _flash_kernel(gate_ref, q_ref, kt_ref, v_ref, gv_ref, o_ref, m_ref, acc_ref, nb_ref, *,
                  windowed, branch):
    t = T_ATT
    g = pl.program_id(1)
    i = pl.program_id(2)
    n_tab = WIN_TILES + 1 if windowed else N_NEAR

    @pl.when(i == 0)
    def _():
        for h in range(HEADS_PER_GROUP):
            for d in range(n_tab):
                gen = jnp.broadcast_to(gv_ref[0, h, d:d + 1, :], (t, 2 * t))
                nb_ref[h, d] = pltpu.roll(gen, 0, 1, stride=1, stride_axis=0)[:, :t]

    m_ref[...] = jnp.full(m_ref.shape, NEG_INIT, F32)
    acc_ref[...] = jnp.zeros(acc_ref.shape, F32)

    def tile_step(first_tile, n_tiles, dists=()):
        heads = range(HEADS_PER_GROUP)
        assert not dists or len(dists) == n_tiles
        nk = t * n_tiles
        start = pl.multiple_of(first_tile * t, t)
        kt = kt_ref[0, :, pl.ds(start, nk)]
        v = v_ref[0, 0, pl.ds(start, nk), :]
        s = [_dot(q_ref[0, h], kt) for h in heads]
        if dists:
            s = [s[h] + jnp.concatenate([nb_ref[h, d] for d in dists], axis=1) for h in heads]
            dist = (dists[0] * t + lax.broadcasted_iota(jnp.int32, (t, nk), 0)
                    - lax.broadcasted_iota(jnp.int32, (t, nk), 1))
            causal = dists[-1] == 0
            banded = windowed and dists[0] * t + t - 1 >= WINDOW
            assert not (causal and banded)
            keep = None
            if causal:
                keep = dist >= 0
            elif banded:
                keep = dist < WINDOW
            if keep is not None:
                s = [jnp.where(keep, s[h], MASK_BIAS) for h in heads]
        m_prev = [m_ref[h] for h in heads]
        m_new = [jnp.maximum(m_prev[h], jnp.max(s[h], axis=1, keepdims=True)) for h in heads]
        p = [jnp.exp(s[h] - jnp.tile(m_new[h], (1, nk // LANES))).astype(BF16) for h in heads]
        pv = [_dot(p[h], v) for h in heads]
        for h in heads:
            acc_ref[h] = jnp.exp(m_prev[h] - m_new[h]) * acc_ref[h] + pv[h]
            m_ref[h] = m_new[h]

    def near_steps(first):
        @pl.when(i >= first)
        def _():
            d = first
            if d % 2:
                tile_step(i - d, 1, (d,))
                d -= 1
            for d in range(d, 0, -2):
                tile_step(i - d, 2, (d, d - 1))

        @pl.when(i < first)
        def _():
            for d in range(first - 1, 0, -1):
                @pl.when(i >= d)
                def _():
                    tile_step(i - d, 1, (d,))
        tile_step(i, 1, (0,))

    if windowed:
        near_steps(WIN_TILES)
    else:
        n_far = jnp.maximum(i - (N_NEAR - 1), 0)
        far_tiles = FAR_KEYS // t

        def far_body(j, carry):
            tile_step(j * far_tiles, far_tiles)
            return carry
        n_far_steps = jnp.right_shift(n_far, _log2(far_tiles))
        lax.fori_loop(0, n_far_steps, far_body, 0)
        done = n_far_steps * far_tiles
        run = far_tiles // 2
        while run >= 1:
            has_run = jnp.bitwise_and(n_far, run) != 0

            @pl.when(has_run)
            def _():
                tile_step(done, run)
            done = done + jnp.where(has_run, run, 0)
            run //= 2
        near_steps(N_NEAR - 1)

    lane = lax.broadcasted_iota(jnp.int32, (t, LANES), 1)
    gates = gate_ref[0]
    outs = []
    for h in range(HEADS_PER_GROUP):
        col = _gate_column(g * HEADS_PER_GROUP + h, branch)
        gate = jnp.sum(jnp.where(lane == col, gates, 0.0), axis=1, keepdims=True)
        acc = acc_ref[h]
        outs.append(acc[:, :HEAD_DIM] / jnp.maximum(acc[:, HEAD_DIM:HEAD_DIM + 1], 1.0) * gate)
    o_ref[0] = jnp.concatenate(outs, axis=1).astype(BF16)


def _flash(gates, q_aug, kt, v, gv, *, windowed, branch):
    b, _, s, kd = q_aug.shape
    assert kt.shape[1] == kd
    t = T_ATT
    grid = (b, KV_GROUPS, s // t)
    return pl.pallas_call(
        functools.partial(_flash_kernel, windowed=windowed, branch=branch), grid=grid,
        in_specs=[
            pl.BlockSpec((1, t, LANES), lambda bi, g, i: (bi, i, 0)),
            pl.BlockSpec((1, HEADS_PER_GROUP, t, kd), lambda bi, g, i: (bi, g, i, 0)),
            pl.BlockSpec((1, kd, s), lambda bi, g, i: (bi, 0, 0)),
            pl.BlockSpec((1, 1, s, LANES), lambda bi, g, i: (bi, g, 0, 0)),
            pl.BlockSpec((1, HEADS_PER_GROUP, TOEPLITZ_ROWS, 2 * t), lambda bi, g, i: (g, 0, 0, 0)),
        ],
        out_specs=pl.BlockSpec((1, t, HEADS_PER_GROUP * HEAD_DIM), lambda bi, g, i: (bi, i, g)),
        out_shape=jax.ShapeDtypeStruct((b, s, NSA_WIDTH), BF16),
        scratch_shapes=[pltpu.VMEM((HEADS_PER_GROUP, t, LANES), F32),
                        pltpu.VMEM((HEADS_PER_GROUP, t, LANES), F32),
                        pltpu.VMEM((HEADS_PER_GROUP, WIN_TILES + 1 if windowed else N_NEAR, t, t),
                                   F32)],
        compiler_params=_params("parallel", "parallel", "arbitrary"),
        name="win_attention" if windowed else "sel_attention",
    )(gates, q_aug, kt, v, gv)


def _toeplitz_generators(rel_bias):
    t = T_ATT
    d = np.arange(TOEPLITZ_ROWS)[:, None]
    y = np.arange(2 * t)[None, :]
    dist = np.where(y < t, d * t - y, d * t + 2 * t - y)
    tab = _bias_lookup(rel_bias, _BUCKETS[np.clip(dist, 0, None)]) - rel_bias[REL_BUCKETS - 1]
    tab = jnp.where(jnp.asarray(dist >= 0)[..., None], tab, 0.0)
    return jnp.transpose(tab, (2, 0, 1)).reshape(KV_GROUPS, HEADS_PER_GROUP, TOEPLITZ_ROWS, 2 * t)


def _pool_tile(x, halo, i, w, scale):
    t = x.shape[0]
    halo = jnp.where(i > 0, halo, 0.0)
    xs = jnp.concatenate([halo, x], axis=0)
    sums = []
    acc = xs
    span = 1
    for win in POOL_WINDOWS:
        while span < win:
            acc = acc + pltpu.roll(acc, span, 0)
            span *= 2
        sums.append(acc[HALO:])
    lane_grp = jnp.right_shift(lax.broadcasted_iota(jnp.int32, (t, POOL_WIDTH), 1), _log2(POOL_CH))
    tok = i * t + lax.broadcasted_iota(jnp.int32, (t, POOL_WIDTH), 0)
    ssum = sums[-1]
    win = jnp.full((t, POOL_WIDTH), POOL_WINDOWS[-1], jnp.int32)
    for gi in range(len(POOL_WINDOWS) - 2, -1, -1):
        ssum = jnp.where(lane_grp == gi, sums[gi], ssum)
        win = jnp.where(lane_grp == gi, POOL_WINDOWS[gi], win)
    cnt = jnp.minimum(tok + 1, win).astype(F32)
    dlt = (ssum / cnt - x).astype(BF16)
    return (_dot(dlt, w) * scale).astype(BF16)


def _sgu_tile(u, v, norm_g, norm_b, ws_cat, bs_exp):
    t = u.shape[0]
    v = _gelu(v)
    mu = jnp.mean(v, axis=-1, keepdims=True)
    var = jnp.mean(jnp.square(v - mu), axis=-1, keepdims=True)
    vn = (v - mu) * lax.rsqrt(var + EPS) * norm_g + norm_b
    lane_grp = jnp.right_shift(lax.broadcasted_iota(jnp.int32, (SGU_CHUNK, SGU_WIDTH), 1),
                               _log2(SGU_CH))
    r = lax.broadcasted_iota(jnp.int32, (SGU_CHUNK, SGU_GROUPS * SGU_CHUNK), 0)
    c = lax.broadcasted_iota(jnp.int32, (SGU_CHUNK, SGU_GROUPS * SGU_CHUNK), 1) & (SGU_CHUNK - 1)
    ws = jnp.where(c <= r, ws_cat, 0.0).astype(BF16)
    outs = []
    for n in range(t // SGU_CHUNK):
        vc = vn[n * SGU_CHUNK:(n + 1) * SGU_CHUNK]
        stacked = jnp.concatenate(
            [jnp.where(lane_grp == gi, vc, 0.0) for gi in range(SGU_GROUPS)], axis=0).astype(BF16)
        mixed = _dot(ws, stacked) + bs_exp
        outs.append((_gelu(u[n * SGU_CHUNK:(n + 1) * SGU_CHUNK]) * mixed).astype(BF16))
    return jnp.concatenate(outs, axis=0)


def _out_proj_kernel(oc_ref, os_ref, ow_ref, pin_ref, halo_ref, u_ref, v_ref, pw_ref, ps_ref,
                     ng_ref, nb_ref, ws_ref, bs_ref, w_ref, g_ref, x_ref, y_ref):
    a = (oc_ref[0].astype(F32) + os_ref[0].astype(F32) + ow_ref[0].astype(F32)).astype(BF16)
    pool = _pool_tile(pin_ref[0], halo_ref[0], pl.program_id(1), pw_ref[...], ps_ref[...])
    sgu = _sgu_tile(u_ref[0], v_ref[0], ng_ref[...], nb_ref[...], ws_ref[...], bs_ref[...])
    mix = (_dot(a, w_ref[:NSA_WIDTH, :])
           + _dot(pool, w_ref[NSA_WIDTH:NSA_WIDTH + POOL_WIDTH, :])
           + _dot(sgu, w_ref[NSA_WIDTH + POOL_WIDTH:, :]))
    y_ref[0] = x_ref[0] + _rms(mix, g_ref[...])


def _out_proj(o_cmp, o_sel, o_win, pin, u, v, pool_w, pool_scale, sgu_g, sgu_b, ws_cat, bs_exp,
              w, g, x, l):
    b, s, d = x.shape
    t = T_OUT
    assert t % SGU_CHUNK == 0 and t % HALO == 0
    row = lambda n: pl.BlockSpec((1, t, n), lambda bi, i: (bi, i, 0))
    halo = pl.BlockSpec((1, HALO, POOL_WIDTH),
                        lambda bi, i: (bi, jnp.maximum(i * (t // HALO) - 1, 0), 0))
    layer = [_layer_spec(a, l) for a in (pool_w, pool_scale, sgu_g, sgu_b, ws_cat, bs_exp, w, g)]
    return pl.pallas_call(
        _out_proj_kernel, grid=(b, s // t),
        in_specs=[row(NSA_WIDTH), row(NSA_WIDTH), row(NSA_WIDTH), row(POOL_WIDTH), halo,
                  row(SGU_WIDTH), row(SGU_WIDTH)] + layer + [row(d)],
        out_specs=row(d), out_shape=jax.ShapeDtypeStruct((b, s, d), F32),
        compiler_params=_params("parallel", "parallel"), name="out_proj",
    )(o_cmp, o_sel, o_win, pin, pin, u, v, pool_w, pool_scale, sgu_g, sgu_b, ws_cat, bs_exp, w, g, x)


def _mem_kv_kernel(mem_ref, g_ref, wk_ref, wv_ref, kt_ref, v_ref):
    m = _rms(mem_ref[0], g_ref[...]).astype(BF16)
    kt_ref[0] = _dot(m, wk_ref[...]).T.astype(BF16)
    v_ref[0] = _dot(m, wv_ref[...]).astype(BF16)


def _mem_kv(mem, g, wk, wv, l):
    b, m, d = mem.shape
    blk = pl.BlockSpec((1, m, d), lambda bi: (bi, 0, 0))
    blk_t = pl.BlockSpec((1, d, m), lambda bi: (bi, 0, 0))
    return pl.pallas_call(
        _mem_kv_kernel, grid=(b,),
        in_specs=[blk] + [_layer_spec(a, l) for a in (g, wk, wv)],
        out_specs=(blk_t, blk),
        out_shape=(jax.ShapeDtypeStruct((b, d, m), BF16), jax.ShapeDtypeStruct((b, m, d), BF16)),
        compiler_params=_params("parallel"), name="mem_kv",
    )(mem, g, wk, wv)


def _mem_attn_kernel(x_ref, gpre_ref, wq_ref, kt_ref, v_ref, wo_ref, gpost_ref, y_ref):
    x = x_ref[0]
    h = _rms(x, gpre_ref[...]).astype(BF16)
    q = (_dot(h, wq_ref[...]) * (MEM_HEAD_DIM ** -0.5)).astype(BF16)
    heads = range(MEM_HEADS)
    sl = [slice(hd * MEM_HEAD_DIM, (hd + 1) * MEM_HEAD_DIM) for hd in heads]
    s = [_dot(q[:, sl[hd]], kt_ref[0, sl[hd], :]) for hd in heads]
    e = [jnp.exp(s[hd] - jnp.max(s[hd], axis=-1, keepdims=True)) for hd in heads]
    p = [(e[hd] * (1.0 / jnp.sum(e[hd], axis=-1, keepdims=True))).astype(BF16) for hd in heads]
    o = jnp.concatenate([_dot(p[hd], v_ref[0, :, sl[hd]]).astype(BF16) for hd in heads], axis=1)
    y_ref[0] = x + _rms(_dot(o, wo_ref[...]), gpost_ref[...])


def _mem_attn(x, gpre, wq, kt, v, wo, gpost, l):
    b, s, d = x.shape
    m = v.shape[1]
    t = T_MEM
    row = pl.BlockSpec((1, t, d), lambda bi, i: (bi, i, 0))
    return pl.pallas_call(
        _mem_attn_kernel, grid=(b, s // t),
        in_specs=[row, _layer_spec(gpre, l), _layer_spec(wq, l),
                  pl.BlockSpec((1, d, m), lambda bi, i: (bi, 0, 0)),
                  pl.BlockSpec((1, m, d), lambda bi, i: (bi, 0, 0)),
                  _layer_spec(wo, l), _layer_spec(gpost, l)],
        out_specs=row, out_shape=jax.ShapeDtypeStruct((b, s, d), F32),
        compiler_params=_params("parallel", "parallel"), name="mem_attention",
    )(x, gpre, wq, kt, v, wo, gpost)


def _ffn_kernel(x_ref, halo_ref, gpre_ref, wup_ref, cw_ref, cb_ref, wd_ref, gpost_ref, y_ref,
                act_ref):
    i = pl.program_id(1)
    f = wd_ref.shape[0]
    halo = jnp.where(i > 0, halo_ref[0], 0.0)
    h = _rms(jnp.concatenate([halo, x_ref[0]], axis=0), gpre_ref[...]).astype(BF16)

    def conv(off):
        a = _dot(h, wup_ref[:, off:off + F_CHUNK])
        cw = cw_ref[:, off:off + F_CHUNK]
        return (cw[0:1] * pltpu.roll(a, 2, 0) + cw[1:2] * pltpu.roll(a, 1, 0) + cw[2:3] * a
                + cb_ref[:, off:off + F_CHUNK])

    for off in range(0, f, F_CHUNK):
        act = _gelu(conv(off)) * conv(f + off)
        act_ref[:, off:off + F_CHUNK] = act[HALO:].astype(BF16)
    y_ref[0] = x_ref[0] + _rms(_dot(act_ref[...], wd_ref[...]), gpost_ref[...])


def _ffn(x, gpre, w_up, conv_w, conv_b, w_down, gpost, l):
    b, s, d = x.shape
    f = w_down.shape[1]
    t = T_FFN
    assert f % F_CHUNK == 0
    resident = lambda a: _layer_spec(a, l, pipeline_mode=pl.Buffered(1))
    return pl.pallas_call(
        _ffn_kernel, grid=(b, s // t),
        in_specs=[
            pl.BlockSpec((1, t, d), lambda bi, i: (bi, i, 0)),
            pl.BlockSpec((1, HALO, d), lambda bi, i: (bi, jnp.maximum(i * (t // HALO) - 1, 0), 0)),
            resident(gpre), resident(w_up), resident(conv_w), resident(conv_b), resident(w_down),
            resident(gpost),
        ],
        out_specs=pl.BlockSpec((1, t, d), lambda bi, i: (bi, i, 0)),
        out_shape=jax.ShapeDtypeStruct((b, s, d), F32),
        scratch_shapes=[pltpu.VMEM((t, f), BF16)],
        compiler_params=_params("parallel", "parallel"), name="conv_ffn",
    )(x, x, gpre, w_up, conv_w, conv_b, w_down, gpost)


def kernel(x, mem, rel_bias, mix_norm_pre, mix_norm_post, w_in, cmp_pos, cmp_w1, cmp_b1, cmp_w2,
           cmp_b2, pool_w, pool_scale, sgu_norm_g, sgu_norm_b, sgu_w, sgu_b, w_out, mem_norm_pre,
           mem_norm_kv, mem_norm_post, w_mq, w_mk, w_mv, w_mo, ffn_norm_pre, ffn_norm_post, w_up,
           conv_w, conv_b, w_down):
    depth = w_in.shape[0]
    s = x.shape[1]
    assert s % max(T_IN, T_CMP, T_ATT, T_OUT, T_MEM, T_FFN) == 0
    nb = s // CMP_STRIDE

    w_in_p = _in_proj_weights(w_in)
    band, wagg = _cmp_tables(rel_bias, nb)
    toep = _toeplitz_generators(rel_bias)
    row = lambda a: a[:, None, :]
    bf = lambda a: a.astype(BF16)
    w_out_b, w_mq_b, w_mk_b, w_mv_b, w_mo_b = bf(w_out), bf(w_mq), bf(w_mk), bf(w_mv), bf(w_mo)
    w_up_b, w_down_b = bf(w_up), bf(w_down)
    g_mix_pre, g_mix_post = row(mix_norm_pre), row(mix_norm_post)
    g_mem_pre, g_mem_kv, g_mem_post = row(mem_norm_pre), row(mem_norm_kv), row(mem_norm_post)
    g_ffn_pre, g_ffn_post, conv_b_r = row(ffn_norm_pre), row(ffn_norm_post), row(conv_b)
    cw = jax.vmap(_compress_weights)(cmp_pos, cmp_w1, cmp_b1, cmp_w2, cmp_b2)
    n_pool = len(POOL_WINDOWS)
    pool_blk = bf(jnp.einsum("lgcd,gh->lgchd", pool_w, jnp.eye(n_pool, dtype=F32)).reshape(
        depth, POOL_WIDTH, POOL_WIDTH))
    pool_scale_r = row(pool_scale)
    sgu_g, sgu_bias = row(sgu_norm_g), row(sgu_norm_b)
    ws_cat = jnp.concatenate([sgu_w[:, gi] for gi in range(SGU_GROUPS)], axis=2)
    bs_exp = jnp.repeat(jnp.swapaxes(sgu_b, 1, 2), SGU_CH, axis=2)

    for l in range(depth):
        qa, ksel_t, kwin_t, kcmp, vcmp, vsel, vwin, gates, pin, u, v = _in_proj(
            x, g_mix_pre, w_in_p, l)
        kc, vct = _compress(kcmp, vcmp, cw, l)
        o_cmp, q_aug = _cmp_attention(gates, qa, kc, vct, band, wagg)
        o_sel = _flash(gates, q_aug, ksel_t, vsel, toep, windowed=False, branch=1)
        o_win = _flash(gates, qa, kwin_t, vwin, toep, windowed=True, branch=2)
        x = _out_proj(o_cmp, o_sel, o_win, pin, u, v, pool_blk, pool_scale_r, sgu_g, sgu_bias,
                      ws_cat, bs_exp, w_out_b, g_mix_post, x, l)
        km, vm = _mem_kv(mem, g_mem_kv, w_mk_b, w_mv_b, l)
        x = _mem_attn(x, g_mem_pre, w_mq_b, km, vm, w_mo_b, g_mem_post, l)
        x = _ffn(x, g_ffn_pre, w_up_b, conv_w, conv_b_r, w_down_b, g_ffn_post, l)
    return x
```

```python
import functools
import math

import numpy as np
import jax
import jax.numpy as jnp
from jax import lax
from jax.experimental import pallas as pl
from jax.experimental.pallas import tpu as pltpu

F32 = jnp.float32
BF16 = jnp.bfloat16

NSA_WIDTH = 512
POOL_WIDTH = 256
SGU_WIDTH = 256
NSA_HEADS = 8
KV_GROUPS = 2
HEADS_PER_GROUP = 4
HEAD_DIM = 64
KV_WIDTH = 128
GATE_WIDTH = 24
CMP_STRIDE = 16
CMP_LEN = 32
CMP_HIDDEN = 128
SEL_BLOCK = 64
SEL_TOPK = 16
N_LOCAL_SEL = 2
SEL_AGG = (1.0, 2.0, 2.0, 2.0, 1.0)
FORCE_SCORE = 1e9
WINDOW = 512
REL_BUCKETS = 32
REL_MAX_DIST = 1024
POOL_WINDOWS = (2, 4, 8, 16)
POOL_CH = 64
SGU_GROUPS = 4
SGU_CH = 64
SGU_CHUNK = 128
MEM_HEADS = 4
MEM_HEAD_DIM = 256
EPS = 1e-6

LANES = 128
SUBLANES = 8
VMEM_LIMIT = 56 * 1024 * 1024

T_IN = 512
T_CMP = 512
T_ATT = 512
T_OUT = 512
T_MEM = 512
T_FFN = 1024
F_CHUNK = 256
HALO = 16

MASK_BIAS = -float(2 ** 30)
NEG_INIT = -1e30

_Q_OFF = 0
_KCMP_OFF = NSA_WIDTH
_VCMP_OFF = _KCMP_OFF + KV_WIDTH
_KSEL_OFF = _VCMP_OFF + KV_WIDTH
_VSEL_OFF = _KSEL_OFF + KV_WIDTH
_KWIN_OFF = _VSEL_OFF + KV_WIDTH
_VWIN_OFF = _KWIN_OFF + KV_WIDTH
_GATE_OFF = _VWIN_OFF + KV_WIDTH
_POOL_OFF = 1536
_U_OFF = _POOL_OFF + POOL_WIDTH
_V_OFF = _U_OFF + SGU_WIDTH
IN_COLS = _V_OFF + SGU_WIDTH
MXU_COLS = 256


def _bucket_table(n_max):
    n = np.arange(n_max)
    max_exact = REL_BUCKETS // 2
    nf = np.maximum(n, 1).astype(np.float64)
    large = max_exact + (np.log(nf / max_exact) / math.log(REL_MAX_DIST / max_exact)
                         * (REL_BUCKETS - max_exact)).astype(np.int32)
    large = np.minimum(large, REL_BUCKETS - 1)
    return np.where(n < max_exact, n, large).astype(np.int32)


_BUCKETS = _bucket_table(16384)
FAR_DIST = int(np.argmax(_BUCKETS == REL_BUCKETS - 1))


def _bias_lookup(rel_bias, buckets):
    onehot = jnp.asarray(np.eye(REL_BUCKETS, dtype=np.float32)[buckets])
    return jnp.einsum("...k,kh->...h", onehot, rel_bias, precision=lax.Precision.HIGHEST)


def _in_proj_weights(w_in):
    depth, d, _ = w_in.shape
    split = _GATE_OFF + GATE_WIDTH
    assert _POOL_OFF >= split and _POOL_OFF % MXU_COLS == 0
    return jnp.concatenate(
        [w_in[:, :, :split].astype(BF16), jnp.zeros((depth, d, _POOL_OFF - split), BF16),
         w_in[:, :, split:].astype(BF16)], axis=2)


def _log2(n):
    assert n & (n - 1) == 0
    return n.bit_length() - 1


def _params(*sem):
    return pltpu.CompilerParams(dimension_semantics=sem, vmem_limit_bytes=VMEM_LIMIT)


def _layer_spec(a, l, **kw):
    return pl.BlockSpec((None,) + a.shape[1:], lambda *_: (l,) + (0,) * (a.ndim - 1), **kw)


def _rms(x, g):
    return x * lax.rsqrt(jnp.mean(x * x, axis=-1, keepdims=True) + EPS) * g


def _gelu(x):
    c = math.sqrt(2.0 / math.pi)
    return x * (0.5 * (1.0 + jnp.tanh(c * (x + 0.044715 * (x * x * x)))))


def _dot(a, b):
    return jnp.dot(a, b, preferred_element_type=F32)


def _dot_nt(a, b):
    return lax.dot_general(a, b, (((1,), (1,)), ((), ())), preferred_element_type=F32)


def _split3(x):
    hi = x.astype(BF16)
    r1 = x - hi.astype(F32)
    mid = r1.astype(BF16)
    lo = (r1 - mid.astype(F32)).astype(BF16)
    return hi, mid, lo


def _dot_f32_left(w_bf16, x):
    hi, mid, lo = _split3(x)
    return _dot(w_bf16, hi) + _dot(w_bf16, mid) + _dot(w_bf16, lo)


def _in_proj_kernel(x_ref, g_ref, w_ref, qa_ref, ksel_ref, kwin_ref, kcmp_ref, vcmp_ref,
                    vsel_ref, vwin_ref, gate_ref, pool_ref, u_ref, v_ref):
    t = x_ref.shape[1]
    h = _rms(x_ref[0], g_ref[...]).astype(BF16)

    def proj(off):
        assert off % MXU_COLS == 0
        return _dot(h, w_ref[:, off:off + MXU_COLS])

    zeros = jnp.zeros((t, HEAD_DIM), F32)

    def group_slab(piece, g):
        return jnp.concatenate([piece, zeros] if g == 0 else [zeros, piece], axis=1)

    scale = HEAD_DIM ** -0.5
    for g in range(KV_GROUPS):
        qg = proj(_Q_OFF + g * MXU_COLS) * scale
        for hg in range(HEADS_PER_GROUP):
            qa_ref[0, g * HEADS_PER_GROUP + hg] = group_slab(
                qg[:, hg * HEAD_DIM:(hg + 1) * HEAD_DIM], g).astype(BF16)

    lane = lax.broadcasted_iota(jnp.int32, (t, LANES), 1)
    kv_cmp = proj(_KCMP_OFF)
    kcmp_ref[0] = kv_cmp[:, :KV_WIDTH].astype(BF16)
    vcmp_ref[0] = kv_cmp[:, KV_WIDTH:].astype(BF16)
    tok = pl.program_id(1) * t + lax.broadcasted_iota(jnp.int32, (LANES, t), 1)
    blk_id = lax.broadcasted_iota(jnp.int32, (LANES, t), 0)
    ksel_ref[0, LANES:, :] = jnp.where(
        jnp.right_shift(tok, _log2(SEL_BLOCK)) == blk_id, 1.0, 0.0).astype(BF16)

    def keys_values(off, v_out):
        kv = proj(off)
        for g in range(KV_GROUPS):
            vg = group_slab(kv[:, KV_WIDTH + g * HEAD_DIM:KV_WIDTH + (g + 1) * HEAD_DIM], 0)
            v_out[0, g] = jnp.where(lane == HEAD_DIM, 1.0, vg).astype(BF16)
        return kv[:, :KV_WIDTH].T.astype(BF16)

    ksel_ref[0, :LANES, :] = keys_values(_KSEL_OFF, vsel_ref)
    kwin_ref[0] = keys_values(_KWIN_OFF, vwin_ref)
    gate_ref[0] = jax.nn.sigmoid(proj(_GATE_OFF)[:, :LANES])
    pool_ref[0] = proj(_POOL_OFF)
    u_ref[0] = proj(_U_OFF)
    v_ref[0] = proj(_V_OFF)


def _in_proj(x, g, w, l):
    b, s, d = x.shape
    t = T_IN
    grid = (b, s // t)
    row3 = lambda n: pl.BlockSpec((1, t, n), lambda bi, i: (bi, i, 0))
    out_shape = (
        jax.ShapeDtypeStruct((b, NSA_HEADS, s, LANES), BF16),
        jax.ShapeDtypeStruct((b, 2 * LANES, s), BF16),
        jax.ShapeDtypeStruct((b, LANES, s), BF16),
        jax.ShapeDtypeStruct((b, s, LANES), BF16),
        jax.ShapeDtypeStruct((b, s, LANES), BF16),
        jax.ShapeDtypeStruct((b, KV_GROUPS, s, LANES), BF16),
        jax.ShapeDtypeStruct((b, KV_GROUPS, s, LANES), BF16),
        jax.ShapeDtypeStruct((b, s, LANES), F32),
        jax.ShapeDtypeStruct((b, s, POOL_WIDTH), F32),
        jax.ShapeDtypeStruct((b, s, SGU_WIDTH), F32),
        jax.ShapeDtypeStruct((b, s, SGU_WIDTH), F32),
    )
    out_specs = (
        pl.BlockSpec((1, NSA_HEADS, t, LANES), lambda bi, i: (bi, 0, i, 0)),
        pl.BlockSpec((1, 2 * LANES, t), lambda bi, i: (bi, 0, i)),
        pl.BlockSpec((1, LANES, t), lambda bi, i: (bi, 0, i)),
        row3(LANES), row3(LANES),
        pl.BlockSpec((1, KV_GROUPS, t, LANES), lambda bi, i: (bi, 0, i, 0)),
        pl.BlockSpec((1, KV_GROUPS, t, LANES), lambda bi, i: (bi, 0, i, 0)),
        row3(LANES), row3(POOL_WIDTH), row3(SGU_WIDTH), row3(SGU_WIDTH),
    )
    return pl.pallas_call(
        _in_proj_kernel, grid=grid,
        in_specs=[row3(d), _layer_spec(g, l), _layer_spec(w, l)],
        out_specs=out_specs, out_shape=out_shape,
        compiler_params=_params("parallel", "parallel"), name="in_proj",
    )(x, g, w)


def _compress_kernel(rk_ref, rv_ref, w1a_ref, w1b_ref, pos_ref, b1_ref, w2k_ref, b2k_ref,
                     w2v_ref, b2v_ref, kc_ref, vct_ref):
    nb = rk_ref.shape[1]
    keep = lax.broadcasted_iota(jnp.int32, (nb, LANES), 0) < nb - 1

    def hidden(r, which):
        a = _dot(r, w1a_ref[which])
        bm = _dot(r, w1b_ref[which])
        pre = a + pltpu.roll(bm, nb - 1, 0)
        posb = (_dot(pos_ref[which, 0], w1a_ref[which])[0:1]
                + _dot(pos_ref[which, 1], w1b_ref[which])[0:1])
        return _gelu(pre + posb + b1_ref[which]).astype(BF16)

    kc = _dot(hidden(rk_ref[0], 0), w2k_ref[...]) + b2k_ref[...]
    kc_ref[0] = jnp.where(keep, kc, 0.0).astype(BF16)
    vc = _dot(hidden(rv_ref[0], 1), w2v_ref[...]) + b2v_ref[...]
    vct_ref[0] = jnp.where(keep, vc, 0.0).T.astype(BF16)


def _compress(kcmp, vcmp, cw, l):
    b, s, _ = kcmp.shape
    nb = s // CMP_STRIDE
    rk = kcmp.reshape(b, nb, CMP_STRIDE * KV_WIDTH)
    rv = vcmp.reshape(b, nb, CMP_STRIDE * KV_WIDTH)
    ws = (cw["w1a"], cw["w1b"], cw["pos"], cw["b1"], cw["w2k"], cw["b2k"], cw["w2v"], cw["b2v"])
    return pl.pallas_call(
        _compress_kernel, grid=(b,),
        in_specs=[pl.BlockSpec((1, nb, rk.shape[2]), lambda bi: (bi, 0, 0)),
                  pl.BlockSpec((1, nb, rv.shape[2]), lambda bi: (bi, 0, 0))]
                 + [_layer_spec(a, l) for a in ws],
        out_specs=(pl.BlockSpec((1, nb, KV_WIDTH), lambda bi: (bi, 0, 0)),
                   pl.BlockSpec((1, KV_WIDTH, nb), lambda bi: (bi, 0, 0))),
        out_shape=(jax.ShapeDtypeStruct((b, nb, KV_WIDTH), BF16),
                   jax.ShapeDtypeStruct((b, KV_WIDTH, nb), BF16)),
        compiler_params=_params("parallel"), name="compress",
    )(rk, rv, *ws)


def _compress_weights(pos, w1, b1, w2, b2):
    half = CMP_LEN // 2

    def group_diag(w):
        z = jnp.zeros_like(w)
        return jnp.stack([jnp.concatenate([w if p == g else z for g in range(KV_GROUPS)], axis=-1)
                          for p in range(KV_GROUPS)], axis=-3)

    def w1_blocks(w):
        return group_diag(w).reshape(half * KV_WIDTH, KV_GROUPS * CMP_HIDDEN)

    def pos_rows(p):
        r = jnp.broadcast_to(p[:, None, :], (half, KV_GROUPS, HEAD_DIM)).reshape(1, half * KV_WIDTH)
        return jnp.concatenate([r, jnp.zeros((SUBLANES - 1, half * KV_WIDTH), F32)], axis=0)

    w1a = jnp.stack([w1_blocks(w1[i, :half]) for i in range(2)]).astype(BF16)
    w1b = jnp.stack([w1_blocks(w1[i, half:]) for i in range(2)]).astype(BF16)
    posr = jnp.stack([jnp.stack([pos_rows(pos[i, :half]), pos_rows(pos[i, half:])])
                      for i in range(2)]).astype(BF16)
    b1t = jnp.tile(b1, (1, KV_GROUPS))[:, None, :]
    w2blk = lambda w: group_diag(w).reshape(KV_GROUPS * CMP_HIDDEN, KV_WIDTH)
    return dict(w1a=w1a, w1b=w1b, pos=posr, b1=b1t,
                w2k=w2blk(w2[0]).astype(BF16), b2k=jnp.tile(b2[0], KV_GROUPS)[None, :],
                w2v=w2blk(w2[1]).astype(BF16), b2v=jnp.tile(b2[1], KV_GROUPS)[None, :])


CMP_PAD = 64
CMP_BAND_BACK = 56
CMP_BAND = CMP_BAND_BACK + T_CMP // CMP_STRIDE
CMP_BAND_DIST0 = CMP_BAND_BACK * CMP_STRIDE - (CMP_LEN - 1)
assert CMP_BAND_DIST0 + CMP_STRIDE >= FAR_DIST and CMP_BAND_BACK <= CMP_PAD
CMP_ROW_CHUNK = 64
SEL_RATIO = SEL_BLOCK // CMP_STRIDE


def _gate_column(head, branch):
    return head * 3 + branch


def _cmp_kernel(gate_ref, qa_ref, kc_ref, vct_ref, band_ref, wagg_ref, ocmp_ref, qaug_ref,
                st_ref, gt_ref, sel_ref):
    tq = qa_ref.shape[2]
    nb = kc_ref.shape[1]
    i = pl.program_id(1)
    g = pl.program_id(0)
    s0 = i * tq
    t = s0 + lax.broadcasted_iota(jnp.int32, (1, tq), 1)
    n_live = (i + 1) * (tq // CMP_STRIDE)
    band_row0 = pl.multiple_of(s0 // CMP_STRIDE - CMP_BAND_BACK + CMP_PAD, SUBLANES)
    gt_ref[...] = gate_ref[0].T
    st_ref[:, :CMP_PAD, :] = jnp.zeros((HEADS_PER_GROUP, CMP_PAD, tq), F32)
    has_block = t >= CMP_LEN - 1

    def body(rows):
        nsr = rows // SEL_RATIO
        kc = kc_ref[0, :rows, :]
        tail = rows - CMP_ROW_CHUNK
        dead = tail + lax.broadcasted_iota(jnp.int32, (CMP_ROW_CHUNK, 1), 0) >= n_live
        imp = jnp.zeros((rows, tq), F32)
        outs = []
        for h in range(HEADS_PER_GROUP):
            st_ref[h, CMP_PAD:CMP_PAD + rows, :] = _dot_nt(kc, qa_ref[0, h])
        for h in range(HEADS_PER_GROUP):
            st_ref[h, pl.ds(band_row0, CMP_BAND), :] += band_ref[h]
        for h in range(HEADS_PER_GROUP):
            last = jnp.where(dead, MASK_BIAS, st_ref[h, CMP_PAD + tail:CMP_PAD + rows, :])
            s = last if tail == 0 else jnp.concatenate(
                [st_ref[h, CMP_PAD:CMP_PAD + tail, :], last], axis=0)
            e = jnp.exp(s - jnp.max(s, axis=0, keepdims=True))
            denom = jnp.maximum(jnp.sum(e, axis=0, keepdims=True), 1.0)
            rinv = jnp.where(has_block, 1.0 / denom, 0.0)
            imp = imp + e * rinv
            gate = gt_ref[pl.ds(_gate_column(g * HEADS_PER_GROUP + h, 0), 1), :]
            outs.append((gate * rinv) * _dot(vct_ref[0, :, :rows], e.astype(BF16)))
        ocmp_ref[0] = jnp.concatenate(outs, axis=0).T.astype(BF16)

        p_sel = _dot_f32_left(wagg_ref[:nsr, :rows], imp)
        j = lax.broadcasted_iota(jnp.int32, (nsr, 1), 0)
        rel = jnp.right_shift(t, _log2(SEL_BLOCK)) - j
        forced = (j == 0) | ((rel >= 0) & (rel < N_LOCAL_SEL))
        jfull = jnp.broadcast_to(j, (nsr, tq)).astype(F32)
        score = jnp.where(forced, FORCE_SCORE - 1e3 * jfull,
                          jnp.where(j * SEL_BLOCK <= t, p_sel, -1.0))
        taken_mark = -3e38

        def top_k(score, break_ties):
            for _ in range(SEL_TOPK):
                hit = score == jnp.max(score, axis=0, keepdims=True)
                if break_ties:
                    hit = jfull == jnp.min(jnp.where(hit, jfull, float(nsr)), axis=0, keepdims=True)
                score = jnp.where(hit, taken_mark, score)
            return score

        sel_ref[:nsr, :] = top_k(score, False)
        crowded = t >= (SEL_TOPK - 1) * SEL_BLOCK
        taken = jnp.sum(jnp.where(sel_ref[:nsr, :] == taken_mark, 1.0, 0.0), axis=0, keepdims=True)
        tied = jnp.max(jnp.where(crowded & (taken != SEL_TOPK), 1.0, 0.0)) > 0.0

        @pl.when(tied)
        def _():
            sel_ref[:nsr, :] = top_k(score, True)
        notsel = jnp.where(sel_ref[:nsr, :] == taken_mark, 0.0, MASK_BIAS)
        if nsr < LANES:
            notsel = jnp.concatenate(
                [notsel, jnp.full((LANES - nsr, tq), MASK_BIAS, F32)], axis=0)
        notsel = notsel.T.astype(BF16)
        for h in range(HEADS_PER_GROUP):
            qaug_ref[0, h, :, :LANES] = qa_ref[0, h]
            qaug_ref[0, h, :, LANES:] = notsel

    for rows in range(CMP_ROW_CHUNK, nb + 1, CMP_ROW_CHUNK):
        @pl.when((n_live > rows - CMP_ROW_CHUNK) & (n_live <= rows))
        def _():
            body(rows)


def _cmp_attention(gates, qa, kc, vct, band, wagg):
    b, _, s, _ = qa.shape
    nb = kc.shape[1]
    tq = T_CMP
    assert nb % CMP_ROW_CHUNK == 0
    grid = (KV_GROUPS, s // tq, b)
    return pl.pallas_call(
        _cmp_kernel, grid=grid,
        in_specs=[
            pl.BlockSpec((1, tq, LANES), lambda g, i, bi: (bi, i, 0)),
            pl.BlockSpec((1, HEADS_PER_GROUP, tq, LANES), lambda g, i, bi: (bi, g, i, 0)),
            pl.BlockSpec((1, nb, KV_WIDTH), lambda g, i, bi: (bi, 0, 0)),
            pl.BlockSpec((1, HEAD_DIM, nb), lambda g, i, bi: (bi, g, 0)),
            pl.BlockSpec((HEADS_PER_GROUP, CMP_BAND, tq), lambda g, i, bi: (g, 0, 0)),
            pl.BlockSpec(wagg.shape, lambda g, i, bi: (0, 0)),
        ],
        out_specs=(
            pl.BlockSpec((1, tq, HEADS_PER_GROUP * HEAD_DIM), lambda g, i, bi: (bi, i, g)),
            pl.BlockSpec((1, HEADS_PER_GROUP, tq, 2 * LANES), lambda g, i, bi: (bi, g, i, 0)),
        ),
        out_shape=(jax.ShapeDtypeStruct((b, s, NSA_WIDTH), BF16),
                   jax.ShapeDtypeStruct((b, NSA_HEADS, s, 2 * LANES), BF16)),
        scratch_shapes=[pltpu.VMEM((HEADS_PER_GROUP, CMP_PAD + nb, tq), F32),
                        pltpu.VMEM((LANES, tq), F32), pltpu.VMEM((LANES, tq), F32)],
        compiler_params=_params("parallel", "parallel", "parallel"), name="cmp_attention",
    )(gates, qa, kc, vct, band, wagg)


def _cmp_tables(rel_bias, nb):
    jj = np.arange(CMP_BAND)[:, None]
    ii = np.arange(T_CMP)[None, :]
    dist = CMP_BAND_DIST0 + ii - CMP_STRIDE * jj
    bidx = _BUCKETS[np.clip(dist, 0, None)]
    band = jnp.transpose(_bias_lookup(rel_bias, bidx) - rel_bias[REL_BUCKETS - 1], (2, 0, 1))
    band = jnp.where(jnp.asarray(dist >= 0)[None], band, MASK_BIAS)
    ns = nb // SEL_RATIO
    assert SEL_TOPK <= ns <= LANES
    wagg = np.zeros((LANES, nb), np.float32)
    for m, w in enumerate(SEL_AGG):
        for jb in range(ns):
            c = SEL_RATIO * jb + m - 1
            if 0 <= c < nb - 1:
                wagg[jb, c] = w
    return band, jnp.asarray(wagg, BF16)


N_NEAR = (FAR_DIST + T_ATT - 1) // T_ATT + 1
assert N_NEAR * T_ATT - (T_ATT - 1) >= FAR_DIST
WIN_TILES = (WINDOW - 1 + T_ATT - 1) // T_ATT
TOEPLITZ_ROWS = 8
assert max(N_NEAR, WIN_TILES + 1) <= TOEPLITZ_ROWS
FAR_KEYS = 4 * T_ATT


def _flash_kernel(gate_ref, q_ref, kt_ref, v_ref, gv_ref, o_ref, m_ref, acc_ref, nb_ref, *,
                  windowed, branch):
    t = T_ATT
    g = pl.program_id(1)
    i = pl.program_id(2)
    n_tab = WIN_TILES + 1 if windowed else N_NEAR

    @pl.when(i == 0)
    def _():
        for h in range(HEADS_PER_GROUP):
            for d in range(n_tab):
                gen = jnp.broadcast_to(gv_ref[0, h, d:d + 1, :], (t, 2 * t))
                nb_ref[h, d] = pltpu.roll(gen, 0, 1, stride=1, stride_axis=0)[:, :t]

    m_ref[...] = jnp.full(m_ref.shape, NEG_INIT, F32)
    acc_ref[...] = jnp.zeros(acc_ref.shape, F32)

    def tile_step(first_tile, n_tiles, dists=()):
        heads = range(HEADS_PER_GROUP)
        assert not dists or len(dists) == n_tiles
        nk = t * n_tiles
        start = pl.multiple_of(first_tile * t, t)
        kt = kt_ref[0, :, pl.ds(start, nk)]
        v = v_ref[0, 0, pl.ds(start, nk), :]
        s = [_dot(q_ref[0, h], kt) for h in heads]
        if dists:
            s = [s[h] + jnp.concatenate([nb_ref[h, d] for d in dists], axis=1) for h in heads]
            dist = (dists[0] * t + lax.broadcasted_iota(jnp.int32, (t, nk), 0)
                    - lax.broadcasted_iota(jnp.int32, (t, nk), 1))
            causal = dists[-1] == 0
            banded = windowed and dists[0] * t + t - 1 >= WINDOW
            assert not (causal and banded)
            keep = None
            if causal:
                keep = dist >= 0
            elif banded:
                keep = dist < WINDOW
            if keep is not None:
                s = [jnp.where(keep, s[h], MASK_BIAS) for h in heads]
        m_prev = [m_ref[h] for h in heads]
        m_new = [jnp.maximum(m_prev[h], jnp.max(s[h], axis=1, keepdims=True)) for h in heads]
        p = [jnp.exp(s[h] - jnp.tile(m_new[h], (1, nk // LANES))).astype(BF16) for h in heads]
        pv = [_dot(p[h], v) for h in heads]
        for h in heads:
            acc_ref[h] = jnp.exp(m_prev[h] - m_new[h]) * acc_ref[h] + pv[h]
            m_ref[h] = m_new[h]

    def near_steps(first):
        @pl.when(i >= first)
        def _():
            d = first
            if d % 2:
                tile_step(i - d, 1, (d,))
                d -= 1
            for d in range(d, 0, -2):
                tile_step(i - d, 2, (d, d - 1))

        @pl.when(i < first)
        def _():
            for d in range(first - 1, 0, -1):
                @pl.when(i >= d)
                def _():
                    tile_step(i - d, 1, (d,))
        tile_step(i, 1, (0,))

    if windowed:
        near_steps(WIN_TILES)
    else:
        n_far = jnp.maximum(i - (N_NEAR - 1), 0)
        far_tiles = FAR_KEYS // t

        def far_body(j, carry):
            tile_step(j * far_tiles, far_tiles)
            return carry
        n_far_steps = jnp.right_shift(n_far, _log2(far_tiles))
        lax.fori_loop(0, n_far_steps, far_body, 0)
        done = n_far_steps * far_tiles
        run = far_tiles // 2
        while run >= 1:
            has_run = jnp.bitwise_and(n_far, run) != 0

            @pl.when(has_run)
            def _():
                tile_step(done, run)
            done = done + jnp.where(has_run, run, 0)
            run //= 2
        near_steps(N_NEAR - 1)

    lane = lax.broadcasted_iota(jnp.int32, (t, LANES), 1)
    gates = gate_ref[0]
    outs = []
    for h in range(HEADS_PER_GROUP):
        col = _gate_column(g * HEADS_PER_GROUP + h, branch)
        gate = jnp.sum(jnp.where(lane == col, gates, 0.0), axis=1, keepdims=True)
        acc = acc_ref[h]
        outs.append(acc[:, :HEAD_DIM] / jnp.maximum(acc[:, HEAD_DIM:HEAD_DIM + 1], 1.0) * gate)
    o_ref[0] = jnp.concatenate(outs, axis=1).astype(BF16)


def _flash(gates, q_aug, kt, v, gv, *, windowed, branch):
    b, _, s, kd = q_aug.shape
    assert kt.shape[1] == kd
    t = T_ATT
    grid = (b, KV_GROUPS, s // t)
    return pl.pallas_call(
        functools.partial(_flash_kernel, windowed=windowed, branch=branch), grid=grid,
        in_specs=[
            pl.BlockSpec((1, t, LANES), lambda bi, g, i: (bi, i, 0)),
            pl.BlockSpec((1, HEADS_PER_GROUP, t, kd), lambda bi, g, i: (bi, g, i, 0)),
            pl.BlockSpec((1, kd, s), lambda bi, g, i: (bi, 0, 0)),
            pl.BlockSpec((1, 1, s, LANES), lambda bi, g, i: (bi, g, 0, 0)),
            pl.BlockSpec((1, HEADS_PER_GROUP, TOEPLITZ_ROWS, 2 * t), lambda bi, g, i: (g, 0, 0, 0)),
        ],
        out_specs=pl.BlockSpec((1, t, HEADS_PER_GROUP * HEAD_DIM), lambda bi, g, i: (bi, i, g)),
        out_shape=jax.ShapeDtypeStruct((b, s, NSA_WIDTH), BF16),
        scratch_shapes=[pltpu.VMEM((HEADS_PER_GROUP, t, LANES), F32),
                        pltpu.VMEM((HEADS_PER_GROUP, t, LANES), F32),
                        pltpu.VMEM((HEADS_PER_GROUP, WIN_TILES + 1 if windowed else N_NEAR, t, t),
                                   F32)],
        compiler_params=_params("parallel", "parallel", "arbitrary"),
        name="win_attention" if windowed else "sel_attention",
    )(gates, q_aug, kt, v, gv)


def _toeplitz_generators(rel_bias):
    t = T_ATT
    d = np.arange(TOEPLITZ_ROWS)[:, None]
    y = np.arange(2 * t)[None, :]
    dist = np.where(y < t, d * t - y, d * t + 2 * t - y)
    tab = _bias_lookup(rel_bias, _BUCKETS[np.clip(dist, 0, None)]) - rel_bias[REL_BUCKETS - 1]
    tab = jnp.where(jnp.asarray(dist >= 0)[..., None], tab, 0.0)
    return jnp.transpose(tab, (2, 0, 1)).reshape(KV_GROUPS, HEADS_PER_GROUP, TOEPLITZ_ROWS, 2 * t)


def _pool_tile(x, halo, i, w, scale):
    t = x.shape[0]
    halo = jnp.where(i > 0, halo, 0.0)
    xs = jnp.concatenate([halo, x], axis=0)
    sums = []
    acc = xs
    span = 1
    for win in POOL_WINDOWS:
        while span < win:
            acc = acc + pltpu.roll(acc, span, 0)
            span *= 2
        sums.append(acc[HALO:])
    lane_grp = jnp.right_shift(lax.broadcasted_iota(jnp.int32, (t, POOL_WIDTH), 1), _log2(POOL_CH))
    tok = i * t + lax.broadcasted_iota(jnp.int32, (t, POOL_WIDTH), 0)
    ssum = sums[-1]
    win = jnp.full((t, POOL_WIDTH), POOL_WINDOWS[-1], jnp.int32)
    for gi in range(len(POOL_WINDOWS) - 2, -1, -1):
        ssum = jnp.where(lane_grp == gi, sums[gi], ssum)
        win = jnp.where(lane_grp == gi, POOL_WINDOWS[gi], win)
    cnt = jnp.minimum(tok + 1, win).astype(F32)
    dlt = (ssum / cnt - x).astype(BF16)
    return (_dot(dlt, w) * scale).astype(BF16)


def _sgu_tile(u, v, norm_g, norm_b, ws_cat, bs_exp):
    t = u.shape[0]
    v = _gelu(v)
    mu = jnp.mean(v, axis=-1, keepdims=True)
    var = jnp.mean(jnp.square(v - mu), axis=-1, keepdims=True)
    vn = (v - mu) * lax.rsqrt(var + EPS) * norm_g + norm_b
    lane_grp = jnp.right_shift(lax.broadcasted_iota(jnp.int32, (SGU_CHUNK, SGU_WIDTH), 1),
                               _log2(SGU_CH))
    r = lax.broadcasted_iota(jnp.int32, (SGU_CHUNK, SGU_GROUPS * SGU_CHUNK), 0)
    c = lax.broadcasted_iota(jnp.int32, (SGU_CHUNK, SGU_GROUPS * SGU_CHUNK), 1) & (SGU_CHUNK - 1)
    ws = jnp.where(c <= r, ws_cat, 0.0).astype(BF16)
    outs = []
    for n in range(t // SGU_CHUNK):
        vc = vn[n * SGU_CHUNK:(n + 1) * SGU_CHUNK]
        stacked = jnp.concatenate(
            [jnp.where(lane_grp == gi, vc, 0.0) for gi in range(SGU_GROUPS)], axis=0).astype(BF16)
        mixed = _dot(ws, stacked) + bs_exp
        outs.append((_gelu(u[n * SGU_CHUNK:(n + 1) * SGU_CHUNK]) * mixed).astype(BF16))
    return jnp.concatenate(outs, axis=0)


def _out_proj_kernel(oc_ref, os_ref, ow_ref, pin_ref, halo_ref, u_ref, v_ref, pw_ref, ps_ref,
                     ng_ref, nb_ref, ws_ref, bs_ref, w_ref, g_ref, x_ref, y_ref):
    a = (oc_ref[0].astype(F32) + os_ref[0].astype(F32) + ow_ref[0].astype(F32)).astype(BF16)
    pool = _pool_tile(pin_ref[0], halo_ref[0], pl.program_id(1), pw_ref[...], ps_ref[...])
    sgu = _sgu_tile(u_ref[0], v_ref[0], ng_ref[...], nb_ref[...], ws_ref[...], bs_ref[...])
    mix = (_dot(a, w_ref[:NSA_WIDTH, :])
           + _dot(pool, w_ref[NSA_WIDTH:NSA_WIDTH + POOL_WIDTH, :])
           + _dot(sgu, w_ref[NSA_WIDTH + POOL_WIDTH:, :]))
    y_ref[0] = x_ref[0] + _rms(mix, g_ref[...])


def _out_proj(o_cmp, o_sel, o_win, pin, u, v, pool_w, pool_scale, sgu_g, sgu_b, ws_cat, bs_exp,
              w, g, x, l):
    b, s, d = x.shape
    t = T_OUT
    assert t % SGU_CHUNK == 0 and t % HALO == 0
    row = lambda n: pl.BlockSpec((1, t, n), lambda bi, i: (bi, i, 0))
    halo = pl.BlockSpec((1, HALO, POOL_WIDTH),
                        lambda bi, i: (bi, jnp.maximum(i * (t // HALO) - 1, 0), 0))
    layer = [_layer_spec(a, l) for a in (pool_w, pool_scale, sgu_g, sgu_b, ws_cat, bs_exp, w, g)]
    return pl.pallas_call(
        _out_proj_kernel, grid=(b, s // t),
        in_specs=[row(NSA_WIDTH), row(NSA_WIDTH), row(NSA_WIDTH), row(POOL_WIDTH), halo,
                  row(SGU_WIDTH), row(SGU_WIDTH)] + layer + [row(d)],
        out_specs=row(d), out_shape=jax.ShapeDtypeStruct((b, s, d), F32),
        compiler_params=_params("parallel", "parallel"), name="out_proj",
    )(o_cmp, o_sel, o_win, pin, pin, u, v, pool_w, pool_scale, sgu_g, sgu_b, ws_cat, bs_exp, w, g, x)


def _mem_kv_kernel(mem_ref, g_ref, wk_ref, wv_ref, kt_ref, v_ref):
    m = _rms(mem_ref[0], g_ref[...]).astype(BF16)
    kt_ref[0] = _dot(m, wk_ref[...]).T.astype(BF16)
    v_ref[0] = _dot(m, wv_ref[...]).astype(BF16)


def _mem_kv(mem, g, wk, wv, l):
    b, m, d = mem.shape
    blk = pl.BlockSpec((1, m, d), lambda bi: (bi, 0, 0))
    blk_t = pl.BlockSpec((1, d, m), lambda bi: (bi, 0, 0))
    return pl.pallas_call(
        _mem_kv_kernel, grid=(b,),
        in_specs=[blk] + [_layer_spec(a, l) for a in (g, wk, wv)],
        out_specs=(blk_t, blk),
        out_shape=(jax.ShapeDtypeStruct((b, d, m), BF16), jax.ShapeDtypeStruct((b, m, d), BF16)),
        compiler_params=_params("parallel"), name="mem_kv",
    )(mem, g, wk, wv)


def _mem_attn_kernel(x_ref, gpre_ref, wq_ref, kt_ref, v_ref, wo_ref, gpost_ref, y_ref):
    x = x_ref[0]
    h = _rms(x, gpre_ref[...]).astype(BF16)
    q = (_dot(h, wq_ref[...]) * (MEM_HEAD_DIM ** -0.5)).astype(BF16)
    heads = range(MEM_HEADS)
    sl = [slice(hd * MEM_HEAD_DIM, (hd + 1) * MEM_HEAD_DIM) for hd in heads]
    s = [_dot(q[:, sl[hd]], kt_ref[0, sl[hd], :]) for hd in heads]
    e = [jnp.exp(s[hd] - jnp.max(s[hd], axis=-1, keepdims=True)) for hd in heads]
    p = [(e[hd] * (1.0 / jnp.sum(e[hd], axis=-1, keepdims=True))).astype(BF16) for hd in heads]
    o = jnp.concatenate([_dot(p[hd], v_ref[0, :, sl[hd]]).astype(BF16) for hd in heads], axis=1)
    y_ref[0] = x + _rms(_dot(o, wo_ref[...]), gpost_ref[...])


def _mem_attn(x, gpre, wq, kt, v, wo, gpost, l):
    b, s, d = x.shape
    m = v.shape[1]
    t = T_MEM
    row = pl.BlockSpec((1, t, d), lambda bi, i: (bi, i, 0))
    return pl.pallas_call(
        _mem_attn_kernel, grid=(b, s // t),
        in_specs=[row, _layer_spec(gpre, l), _layer_spec(wq, l),
                  pl.BlockSpec((1, d, m), lambda bi, i: (bi, 0, 0)),
                  pl.BlockSpec((1, m, d), lambda bi, i: (bi, 0, 0)),
                  _layer_spec(wo, l), _layer_spec(gpost, l)],
        out_specs=row, out_shape=jax.ShapeDtypeStruct((b, s, d), F32),
        compiler_params=_params("parallel", "parallel"), name="mem_attention",
    )(x, gpre, wq, kt, v, wo, gpost)


def _ffn_kernel(x_ref, halo_ref, gpre_ref, wup_ref, cw_ref, cb_ref, wd_ref, gpost_ref, y_ref,
                act_ref):
    i = pl.program_id(1)
    f = wd_ref.shape[0]
    halo = jnp.where(i > 0, halo_ref[0], 0.0)
    h = _rms(jnp.concatenate([halo, x_ref[0]], axis=0), gpre_ref[...]).astype(BF16)

    def conv(off):
        a = _dot(h, wup_ref[:, off:off + F_CHUNK])
        cw = cw_ref[:, off:off + F_CHUNK]
        return (cw[0:1] * pltpu.roll(a, 2, 0) + cw[1:2] * pltpu.roll(a, 1, 0) + cw[2:3] * a
                + cb_ref[:, off:off + F_CHUNK])

    for off in range(0, f, F_CHUNK):
        act = _gelu(conv(off)) * conv(f + off)
        act_ref[:, off:off + F_CHUNK] = act[HALO:].astype(BF16)
    y_ref[0] = x_ref[0] + _rms(_dot(act_ref[...], wd_ref[...]), gpost_ref[...])


def _ffn(x, gpre, w_up, conv_w, conv_b, w_down, gpost, l):
    b, s, d = x.shape
    f = w_down.shape[1]
    t = T_FFN
    assert f % F_CHUNK == 0
    resident = lambda a: _layer_spec(a, l, pipeline_mode=pl.Buffered(1))
    return pl.pallas_call(
        _ffn_kernel, grid=(b, s // t),
        in_specs=[
            pl.BlockSpec((1, t, d), lambda bi, i: (bi, i, 0)),
            pl.BlockSpec((1, HALO, d), lambda bi, i: (bi, jnp.maximum(i * (t // HALO) - 1, 0), 0)),
            resident(gpre), resident(w_up), resident(conv_w), resident(conv_b), resident(w_down),
            resident(gpost),
        ],
        out_specs=pl.BlockSpec((1, t, d), lambda bi, i: (bi, i, 0)),
        out_shape=jax.ShapeDtypeStruct((b, s, d), F32),
        scratch_shapes=[pltpu.VMEM((t, f), BF16)],
        compiler_params=_params("parallel", "parallel"), name="conv_ffn",
    )(x, x, gpre, w_up, conv_w, conv_b, w_down, gpost)


def kernel(x, mem, rel_bias, mix_norm_pre, mix_norm_post, w_in, cmp_pos, cmp_w1, cmp_b1, cmp_w2,
           cmp_b2, pool_w, pool_scale, sgu_norm_g, sgu_norm_b, sgu_w, sgu_b, w_out, mem_norm_pre,
           mem_norm_kv, mem_norm_post, w_mq, w_mk, w_mv, w_mo, ffn_norm_pre, ffn_norm_post, w_up,
           conv_w, conv_b, w_down):
    depth = w_in.shape[0]
    s = x.shape[1]
    assert s % max(T_IN, T_CMP, T_ATT, T_OUT, T_MEM, T_FFN) == 0
    nb = s // CMP_STRIDE

    w_in_p = _in_proj_weights(w_in)
    band, wagg = _cmp_tables(rel_bias, nb)
    toep = _toeplitz_generators(rel_bias)
    row = lambda a: a[:, None, :]
    bf = lambda a: a.astype(BF16)
    w_out_b, w_mq_b, w_mk_b, w_mv_b, w_mo_b = bf(w_out), bf(w_mq), bf(w_mk), bf(w_mv), bf(w_mo)
    w_up_b, w_down_b = bf(w_up), bf(w_down)
    g_mix_pre, g_mix_post = row(mix_norm_pre), row(mix_norm_post)
    g_mem_pre, g_mem_kv, g_mem_post = row(mem_norm_pre), row(mem_norm_kv), row(mem_norm_post)
    g_ffn_pre, g_ffn_post, conv_b_r = row(ffn_norm_pre), row(ffn_norm_post), row(conv_b)
    cw = jax.vmap(_compress_weights)(cmp_pos, cmp_w1, cmp_b1, cmp_w2, cmp_b2)
    n_pool = len(POOL_WINDOWS)
    pool_blk = bf(jnp.einsum("lgcd,gh->lgchd", pool_w, jnp.eye(n_pool, dtype=F32)).reshape(
        depth, POOL_WIDTH, POOL_WIDTH))
    pool_scale_r = row(pool_scale)
    sgu_g, sgu_bias = row(sgu_norm_g), row(sgu_norm_b)
    ws_cat = jnp.concatenate([sgu_w[:, gi] for gi in range(SGU_GROUPS)], axis=2)
    bs_exp = jnp.repeat(jnp.swapaxes(sgu_b, 1, 2), SGU_CH, axis=2)

    for l in range(depth):
        qa, ksel_t, kwin_t, kcmp, vcmp, vsel, vwin, gates, pin, u, v = _in_proj(
            x, g_mix_pre, w_in_p, l)
        kc, vct = _compress(kcmp, vcmp, cw, l)
        o_cmp, q_aug = _cmp_attention(gates, qa, kc, vct, band, wagg)
        o_sel = _flash(gates, q_aug, ksel_t, vsel, toep, windowed=False, branch=1)
        o_win = _flash(gates, qa, kwin_t, vwin, toep, windowed=True, branch=2)
        x = _out_proj(o_cmp, o_sel, o_win, pin, u, v, pool_blk, pool_scale_r, sgu_g, sgu_bias,
                      ws_cat, bs_exp, w_out_b, g_mix_post, x, l)
        km, vm = _mem_kv(mem, g_mem_kv, w_mk_b, w_mv_b, l)
        x = _mem_attn(x, g_mem_pre, w_mq_b, km, vm, w_mo_b, g_mem_post, l)
        x = _ffn(x, g_ffn_pre, w_up_b, conv_w, conv_b_r, w_down_b, g_ffn_post, l)
    return x
```

```python
import functools
import math

import numpy as np
import jax
import jax.numpy as jnp
from jax import lax
from jax.experimental import pallas as pl
from jax.experimental.pallas import tpu as pltpu

F32 = jnp.float32
BF16 = jnp.bfloat16

NSA_WIDTH = 512
POOL_WIDTH = 256
SGU_WIDTH = 256
NSA_HEADS = 8
KV_GROUPS = 2
HEADS_PER_GROUP = 4
HEAD_DIM = 64
KV_WIDTH = 128
GATE_WIDTH = 24
CMP_STRIDE = 16
CMP_LEN = 32
CMP_HIDDEN = 128
SEL_BLOCK = 64
SEL_TOPK = 16
N_LOCAL_SEL = 2
SEL_AGG = (1.0, 2.0, 2.0, 2.0, 1.0)
FORCE_SCORE = 1e9
WINDOW = 512
REL_BUCKETS = 32
REL_MAX_DIST = 1024
POOL_WINDOWS = (2, 4, 8, 16)
POOL_CH = 64
SGU_GROUPS = 4
SGU_CH = 64
SGU_CHUNK = 128
MEM_HEADS = 4
MEM_HEAD_DIM = 256
EPS = 1e-6

LANES = 128
SUBLANES = 8
VMEM_LIMIT = 56 * 1024 * 1024

T_IN = 1024
T_CMP = 512
T_ATT = 512
T_OUT = 512
T_MEM = 1024
T_FFN = 1024
F_CHUNK = 256
HALO = 16

MASK_BIAS = -float(2 ** 30)
NEG_INIT = -1e30

_Q_OFF = 0
_KCMP_OFF = NSA_WIDTH
_VCMP_OFF = _KCMP_OFF + KV_WIDTH
_KSEL_OFF = _VCMP_OFF + KV_WIDTH
_VSEL_OFF = _KSEL_OFF + KV_WIDTH
_KWIN_OFF = _VSEL_OFF + KV_WIDTH
_VWIN_OFF = _KWIN_OFF + KV_WIDTH
_GATE_OFF = _VWIN_OFF + KV_WIDTH
_POOL_OFF = 1536
_U_OFF = _POOL_OFF + POOL_WIDTH
_V_OFF = _U_OFF + SGU_WIDTH
IN_COLS = _V_OFF + SGU_WIDTH
MXU_COLS = 256


def _bucket_table(n_max):
    n = np.arange(n_max)
    max_exact = REL_BUCKETS // 2
    nf = np.maximum(n, 1).astype(np.float64)
    large = max_exact + (np.log(nf / max_exact) / math.log(REL_MAX_DIST / max_exact)
                         * (REL_BUCKETS - max_exact)).astype(np.int32)
    large = np.minimum(large, REL_BUCKETS - 1)
    return np.where(n < max_exact, n, large).astype(np.int32)


_BUCKETS = _bucket_table(16384)
FAR_DIST = int(np.argmax(_BUCKETS == REL_BUCKETS - 1))


def _bias_lookup(rel_bias, buckets):
    onehot = jnp.asarray(np.eye(REL_BUCKETS, dtype=np.float32)[buckets])
    return jnp.einsum("...k,kh->...h", onehot, rel_bias, precision=lax.Precision.HIGHEST)


def _in_proj_weights(w_in):
    depth, d, _ = w_in.shape
    split = _GATE_OFF + GATE_WIDTH
    assert _POOL_OFF >= split and _POOL_OFF % MXU_COLS == 0
    return jnp.concatenate(
        [w_in[:, :, :split].astype(BF16), jnp.zeros((depth, d, _POOL_OFF - split), BF16),
         w_in[:, :, split:].astype(BF16)], axis=2)


def _log2(n):
    assert n & (n - 1) == 0
    return n.bit_length() - 1


def _params(*sem):
    return pltpu.CompilerParams(dimension_semantics=sem, vmem_limit_bytes=VMEM_LIMIT)


def _layer_spec(a, l, **kw):
    return pl.BlockSpec((None,) + a.shape[1:], lambda *_: (l,) + (0,) * (a.ndim - 1), **kw)


def _rms(x, g):
    return x * lax.rsqrt(jnp.mean(x * x, axis=-1, keepdims=True) + EPS) * g


def _gelu(x):
    c = math.sqrt(2.0 / math.pi)
    return x * (0.5 * (1.0 + jnp.tanh(c * (x + 0.044715 * (x * x * x)))))


def _dot(a, b):
    return jnp.dot(a, b, preferred_element_type=F32)


def _dot_nt(a, b):
    return lax.dot_general(a, b, (((1,), (1,)), ((), ())), preferred_element_type=F32)


def _split3(x):
    hi = x.astype(BF16)
    r1 = x - hi.astype(F32)
    mid = r1.astype(BF16)
    lo = (r1 - mid.astype(F32)).astype(BF16)
    return hi, mid, lo


def _dot_f32_left(w_bf16, x):
    hi, mid, lo = _split3(x)
    return _dot(w_bf16, hi) + _dot(w_bf16, mid) + _dot(w_bf16, lo)


def _in_proj_kernel(x_ref, g_ref, w_ref, qa_ref, ksel_ref, kwin_ref, kcmp_ref, vcmp_ref,
                    vsel_ref, vwin_ref, gate_ref, pool_ref, u_ref, v_ref):
    t = x_ref.shape[1]
    h = _rms(x_ref[0], g_ref[...]).astype(BF16)

    def proj(off):
        assert off % MXU_COLS == 0
        return _dot(h, w_ref[:, off:off + MXU_COLS])

    zeros = jnp.zeros((t, HEAD_DIM), F32)

    def group_slab(piece, g):
        return jnp.concatenate([piece, zeros] if g == 0 else [zeros, piece], axis=1)

    scale = HEAD_DIM ** -0.5
    for g in range(KV_GROUPS):
        qg = proj(_Q_OFF + g * MXU_COLS) * scale
        for hg in range(HEADS_PER_GROUP):
            qa_ref[0, g * HEADS_PER_GROUP + hg] = group_slab(
                qg[:, hg * HEAD_DIM:(hg + 1) * HEAD_DIM], g).astype(BF16)

    lane = lax.broadcasted_iota(jnp.int32, (t, LANES), 1)
    kv_cmp = proj(_KCMP_OFF)
    kcmp_ref[0] = kv_cmp[:, :KV_WIDTH].astype(BF16)
    vcmp_ref[0] = kv_cmp[:, KV_WIDTH:].astype(BF16)
    tok = pl.program_id(1) * t + lax.broadcasted_iota(jnp.int32, (LANES, t), 1)
    blk_id = lax.broadcasted_iota(jnp.int32, (LANES, t), 0)
    ksel_ref[0, LANES:, :] = jnp.where(
        jnp.right_shift(tok, _log2(SEL_BLOCK)) == blk_id, 1.0, 0.0).astype(BF16)

    def keys_values(off, v_out):
        kv = proj(off)
        for g in range(KV_GROUPS):
            vg = group_slab(kv[:, KV_WIDTH + g * HEAD_DIM:KV_WIDTH + (g + 1) * HEAD_DIM], 0)
            v_out[0, g] = jnp.where(lane == HEAD_DIM, 1.0, vg).astype(BF16)
        return kv[:, :KV_WIDTH].T.astype(BF16)

    ksel_ref[0, :LANES, :] = keys_values(_KSEL_OFF, vsel_ref)
    kwin_ref[0] = keys_values(_KWIN_OFF, vwin_ref)
    gate_ref[0] = jax.nn.sigmoid(proj(_GATE_OFF)[:, :LANES])
    pool_ref[0] = proj(_POOL_OFF)
    u_ref[0] = proj(_U_OFF)
    v_ref[0] = proj(_V_OFF)


def _in_proj(x, g, w, l):
    b, s, d = x.shape
    t = T_IN
    grid = (b, s // t)
    row3 = lambda n: pl.BlockSpec((1, t, n), lambda bi, i: (bi, i, 0))
    out_shape = (
        jax.ShapeDtypeStruct((b, NSA_HEADS, s, LANES), BF16),
        jax.ShapeDtypeStruct((b, 2 * LANES, s), BF16),
        jax.ShapeDtypeStruct((b, LANES, s), BF16),
        jax.ShapeDtypeStruct((b, s, LANES), BF16),
        jax.ShapeDtypeStruct((b, s, LANES), BF16),
        jax.ShapeDtypeStruct((b, KV_GROUPS, s, LANES), BF16),
        jax.ShapeDtypeStruct((b, KV_GROUPS, s, LANES), BF16),
        jax.ShapeDtypeStruct((b, s, LANES), F32),
        jax.ShapeDtypeStruct((b, s, POOL_WIDTH), F32),
        jax.ShapeDtypeStruct((b, s, SGU_WIDTH), F32),
        jax.ShapeDtypeStruct((b, s, SGU_WIDTH), F32),
    )
    out_specs = (
        pl.BlockSpec((1, NSA_HEADS, t, LANES), lambda bi, i: (bi, 0, i, 0)),
        pl.BlockSpec((1, 2 * LANES, t), lambda bi, i: (bi, 0, i)),
        pl.BlockSpec((1, LANES, t), lambda bi, i: (bi, 0, i)),
        row3(LANES), row3(LANES),
        pl.BlockSpec((1, KV_GROUPS, t, LANES), lambda bi, i: (bi, 0, i, 0)),
        pl.BlockSpec((1, KV_GROUPS, t, LANES), lambda bi, i: (bi, 0, i, 0)),
        row3(LANES), row3(POOL_WIDTH), row3(SGU_WIDTH), row3(SGU_WIDTH),
    )
    return pl.pallas_call(
        _in_proj_kernel, grid=grid,
        in_specs=[row3(d), _layer_spec(g, l), _layer_spec(w, l)],
        out_specs=out_specs, out_shape=out_shape,
        compiler_params=_params("parallel", "parallel"), name="in_proj",
    )(x, g, w)


def _compress_kernel(rk_ref, rv_ref, w1a_ref, w1b_ref, pos_ref, b1_ref, w2k_ref, b2k_ref,
                     w2v_ref, b2v_ref, kc_ref, vct_ref):
    nb = rk_ref.shape[1]
    keep = lax.broadcasted_iota(jnp.int32, (nb, LANES), 0) < nb - 1

    def hidden(r, which):
        a = _dot(r, w1a_ref[which])
        bm = _dot(r, w1b_ref[which])
        pre = a + pltpu.roll(bm, nb - 1, 0)
        posb = (_dot(pos_ref[which, 0], w1a_ref[which])[0:1]
                + _dot(pos_ref[which, 1], w1b_ref[which])[0:1])
        return _gelu(pre + posb + b1_ref[which]).astype(BF16)

    kc = _dot(hidden(rk_ref[0], 0), w2k_ref[...]) + b2k_ref[...]
    kc_ref[0] = jnp.where(keep, kc, 0.0).astype(BF16)
    vc = _dot(hidden(rv_ref[0], 1), w2v_ref[...]) + b2v_ref[...]
    vct_ref[0] = jnp.where(keep, vc, 0.0).T.astype(BF16)


def _compress(kcmp, vcmp, cw, l):
    b, s, _ = kcmp.shape
    nb = s // CMP_STRIDE
    rk = kcmp.reshape(b, nb, CMP_STRIDE * KV_WIDTH)
    rv = vcmp.reshape(b, nb, CMP_STRIDE * KV_WIDTH)
    ws = (cw["w1a"], cw["w1b"], cw["pos"], cw["b1"], cw["w2k"], cw["b2k"], cw["w2v"], cw["b2v"])
    return pl.pallas_call(
        _compress_kernel, grid=(b,),
        in_specs=[pl.BlockSpec((1, nb, rk.shape[2]), lambda bi: (bi, 0, 0)),
                  pl.BlockSpec((1, nb, rv.shape[2]), lambda bi: (bi, 0, 0))]
                 + [_layer_spec(a, l) for a in ws],
        out_specs=(pl.BlockSpec((1, nb, KV_WIDTH), lambda bi: (bi, 0, 0)),
                   pl.BlockSpec((1, KV_WIDTH, nb), lambda bi: (bi, 0, 0))),
        out_shape=(jax.ShapeDtypeStruct((b, nb, KV_WIDTH), BF16),
                   jax.ShapeDtypeStruct((b, KV_WIDTH, nb), BF16)),
        compiler_params=_params("parallel"), name="compress",
    )(rk, rv, *ws)


def _compress_weights(pos, w1, b1, w2, b2):
    half = CMP_LEN // 2

    def group_diag(w):
        z = jnp.zeros_like(w)
        return jnp.stack([jnp.concatenate([w if p == g else z for g in range(KV_GROUPS)], axis=-1)
                          for p in range(KV_GROUPS)], axis=-3)

    def w1_blocks(w):
        return group_diag(w).reshape(half * KV_WIDTH, KV_GROUPS * CMP_HIDDEN)

    def pos_rows(p):
        r = jnp.broadcast_to(p[:, None, :], (half, KV_GROUPS, HEAD_DIM)).reshape(1, half * KV_WIDTH)
        return jnp.concatenate([r, jnp.zeros((SUBLANES - 1, half * KV_WIDTH), F32)], axis=0)

    w1a = jnp.stack([w1_blocks(w1[i, :half]) for i in range(2)]).astype(BF16)
    w1b = jnp.stack([w1_blocks(w1[i, half:]) for i in range(2)]).astype(BF16)
    posr = jnp.stack([jnp.stack([pos_rows(pos[i, :half]), pos_rows(pos[i, half:])])
                      for i in range(2)]).astype(BF16)
    b1t = jnp.tile(b1, (1, KV_GROUPS))[:, None, :]
    w2blk = lambda w: group_diag(w).reshape(KV_GROUPS * CMP_HIDDEN, KV_WIDTH)
    return dict(w1a=w1a, w1b=w1b, pos=posr, b1=b1t,
                w2k=w2blk(w2[0]).astype(BF16), b2k=jnp.tile(b2[0], KV_GROUPS)[None, :],
                w2v=w2blk(w2[1]).astype(BF16), b2v=jnp.tile(b2[1], KV_GROUPS)[None, :])


CMP_PAD = 64
CMP_BAND_BACK = 56
CMP_BAND = CMP_BAND_BACK + T_CMP // CMP_STRIDE
CMP_BAND_DIST0 = CMP_BAND_BACK * CMP_STRIDE - (CMP_LEN - 1)
assert CMP_BAND_DIST0 + CMP_STRIDE >= FAR_DIST and CMP_BAND_BACK <= CMP_PAD
CMP_ROW_CHUNK = 128
SEL_RATIO = SEL_BLOCK // CMP_STRIDE


def _gate_column(head, branch):
    return head * 3 + branch


def _cmp_kernel(gate_ref, qa_ref, kc_ref, vct_ref, band_ref, wagg_ref, ocmp_ref, qaug_ref,
                st_ref, gt_ref, sel_ref):
    tq = qa_ref.shape[2]
    nb = kc_ref.shape[1]
    i = pl.program_id(1)
    g = pl.program_id(0)
    s0 = i * tq
    t = s0 + lax.broadcasted_iota(jnp.int32, (1, tq), 1)
    n_live = (i + 1) * (tq // CMP_STRIDE)
    band_row0 = pl.multiple_of(s0 // CMP_STRIDE - CMP_BAND_BACK + CMP_PAD, SUBLANES)
    gt_ref[...] = gate_ref[0].T
    st_ref[:, :CMP_PAD, :] = jnp.zeros((HEADS_PER_GROUP, CMP_PAD, tq), F32)
    has_block = t >= CMP_LEN - 1

    def body(rows):
        nsr = rows // SEL_RATIO
        kc = kc_ref[0, :rows, :]
        tail = rows - CMP_ROW_CHUNK
        dead = tail + lax.broadcasted_iota(jnp.int32, (CMP_ROW_CHUNK, 1), 0) >= n_live
        imp = jnp.zeros((rows, tq), F32)
        outs = []
        for h in range(HEADS_PER_GROUP):
            st_ref[h, CMP_PAD:CMP_PAD + rows, :] = _dot_nt(kc, qa_ref[0, h])
        for h in range(HEADS_PER_GROUP):
            st_ref[h, pl.ds(band_row0, CMP_BAND), :] += band_ref[h]
        for h in range(HEADS_PER_GROUP):
            last = jnp.where(dead, MASK_BIAS, st_ref[h, CMP_PAD + tail:CMP_PAD + rows, :])
            s = last if tail == 0 else jnp.concatenate(
                [st_ref[h, CMP_PAD:CMP_PAD + tail, :], last], axis=0)
            e = jnp.exp(s - jnp.max(s, axis=0, keepdims=True))
            denom = jnp.maximum(jnp.sum(e, axis=0, keepdims=True), 1.0)
            rinv = jnp.where(has_block, 1.0 / denom, 0.0)
            imp = imp + e * rinv
            gate = gt_ref[pl.ds(_gate_column(g * HEADS_PER_GROUP + h, 0), 1), :]
            outs.append((gate * rinv) * _dot(vct_ref[0, :, :rows], e.astype(BF16)))
        ocmp_ref[0] = jnp.concatenate(outs, axis=0).T.astype(BF16)

        p_sel = _dot_f32_left(wagg_ref[:nsr, :rows], imp)
        j = lax.broadcasted_iota(jnp.int32, (nsr, 1), 0)
        rel = jnp.right_shift(t, _log2(SEL_BLOCK)) - j
        forced = (j == 0) | ((rel >= 0) & (rel < N_LOCAL_SEL))
        jfull = jnp.broadcast_to(j, (nsr, tq)).astype(F32)
        score = jnp.where(forced, FORCE_SCORE - 1e3 * jfull,
                          jnp.where(j * SEL_BLOCK <= t, p_sel, -1.0))
        taken_mark = -3e38

        def top_k(score, break_ties):
            for _ in range(SEL_TOPK):
                hit = score == jnp.max(score, axis=0, keepdims=True)
                if break_ties:
                    hit = jfull == jnp.min(jnp.where(hit, jfull, float(nsr)), axis=0, keepdims=True)
                score = jnp.where(hit, taken_mark, score)
            return score

        sel_ref[:nsr, :] = top_k(score, False)
        crowded = t >= (SEL_TOPK - 1) * SEL_BLOCK
        taken = jnp.sum(jnp.where(sel_ref[:nsr, :] == taken_mark, 1.0, 0.0), axis=0, keepdims=True)
        tied = jnp.max(jnp.where(crowded & (taken != SEL_TOPK), 1.0, 0.0)) > 0.0

        @pl.when(tied)
        def _():
            sel_ref[:nsr, :] = top_k(score, True)
        notsel = jnp.where(sel_ref[:nsr, :] == taken_mark, 0.0, MASK_BIAS)
        if nsr < LANES:
            notsel = jnp.concatenate(
                [notsel, jnp.full((LANES - nsr, tq), MASK_BIAS, F32)], axis=0)
        notsel = notsel.T.astype(BF16)
        for h in range(HEADS_PER_GROUP):
            qaug_ref[0, h, :, :LANES] = qa_ref[0, h]
            qaug_ref[0, h, :, LANES:] = notsel

    for rows in range(CMP_ROW_CHUNK, nb + 1, CMP_ROW_CHUNK):
        @pl.when((n_live > rows - CMP_ROW_CHUNK) & (n_live <= rows))
        def _():
            body(rows)


def _cmp_attention(gates, qa, kc, vct, band, wagg):
    b, _, s, _ = qa.shape
    nb = kc.shape[1]
    tq = T_CMP
    assert nb % CMP_ROW_CHUNK == 0
    grid = (KV_GROUPS, s // tq, b)
    return pl.pallas_call(
        _cmp_kernel, grid=grid,
        in_specs=[
            pl.BlockSpec((1, tq, LANES), lambda g, i, bi: (bi, i, 0)),
            pl.BlockSpec((1, HEADS_PER_GROUP, tq, LANES), lambda g, i, bi: (bi, g, i, 0)),
            pl.BlockSpec((1, nb, KV_WIDTH), lambda g, i, bi: (bi, 0, 0)),
            pl.BlockSpec((1, HEAD_DIM, nb), lambda g, i, bi: (bi, g, 0)),
            pl.BlockSpec((HEADS_PER_GROUP, CMP_BAND, tq), lambda g, i, bi: (g, 0, 0)),
            pl.BlockSpec(wagg.shape, lambda g, i, bi: (0, 0)),
        ],
        out_specs=(
            pl.BlockSpec((1, tq, HEADS_PER_GROUP * HEAD_DIM), lambda g, i, bi: (bi, i, g)),
            pl.BlockSpec((1, HEADS_PER_GROUP, tq, 2 * LANES), lambda g, i, bi: (bi, g, i, 0)),
        ),
        out_shape=(jax.ShapeDtypeStruct((b, s, NSA_WIDTH), BF16),
                   jax.ShapeDtypeStruct((b, NSA_HEADS, s, 2 * LANES), BF16)),
        scratch_shapes=[pltpu.VMEM((HEADS_PER_GROUP, CMP_PAD + nb, tq), F32),
                        pltpu.VMEM((LANES, tq), F32), pltpu.VMEM((LANES, tq), F32)],
        compiler_params=_params("parallel", "parallel", "parallel"), name="cmp_attention",
    )(gates, qa, kc, vct, band, wagg)


def _cmp_tables(rel_bias, nb):
    jj = np.arange(CMP_BAND)[:, None]
    ii = np.arange(T_CMP)[None, :]
    dist = CMP_BAND_DIST0 + ii - CMP_STRIDE * jj
    bidx = _BUCKETS[np.clip(dist, 0, None)]
    band = jnp.transpose(_bias_lookup(rel_bias, bidx) - rel_bias[REL_BUCKETS - 1], (2, 0, 1))
    band = jnp.where(jnp.asarray(dist >= 0)[None], band, MASK_BIAS)
    ns = nb // SEL_RATIO
    assert SEL_TOPK <= ns <= LANES
    wagg = np.zeros((LANES, nb), np.float32)
    for m, w in enumerate(SEL_AGG):
        for jb in range(ns):
            c = SEL_RATIO * jb + m - 1
            if 0 <= c < nb - 1:
                wagg[jb, c] = w
    return band, jnp.asarray(wagg, BF16)


N_NEAR = (FAR_DIST + T_ATT - 1) // T_ATT + 1
assert N_NEAR * T_ATT - (T_ATT - 1) >= FAR_DIST
WIN_TILES = (WINDOW - 1 + T_ATT - 1) // T_ATT
TOEPLITZ_ROWS = 8
assert max(N_NEAR, WIN_TILES + 1) <= TOEPLITZ_ROWS
FAR_KEYS = 4 * T_ATT


def _flash_kernel(gate_ref, q_ref, kt_ref, v_ref, gv_ref, o_ref, m_ref, acc_ref, nb_ref, *,
                  windowed, branch):
    t = T_ATT
    g = pl.program_id(1)
    i = pl.program_id(2)
    n_tab = WIN_TILES + 1 if windowed else N_NEAR

    @pl.when(i == 0)
    def _():
        for h in range(HEADS_PER_GROUP):
            for d in range(n_tab):
                gen = jnp.broadcast_to(gv_ref[0, h, d:d + 1, :], (t, 2 * t))
                nb_ref[h, d] = pltpu.roll(gen, 0, 1, stride=1, stride_axis=0)[:, :t]

    m_ref[...] = jnp.full(m_ref.shape, NEG_INIT, F32)
    acc_ref[...] = jnp.zeros(acc_ref.shape, F32)

    def tile_step(first_tile, n_tiles, dists=()):
        heads = range(HEADS_PER_GROUP)
        assert not dists or len(dists) == n_tiles
        nk = t * n_tiles
        start = pl.multiple_of(first_tile * t, t)
        kt = kt_ref[0, :, pl.ds(start, nk)]
        v = v_ref[0, 0, pl.ds(start, nk), :]
        s = [_dot(q_ref[0, h], kt) for h in heads]
        if dists:
            s = [s[h] + jnp.concatenate([nb_ref[h, d] for d in dists], axis=1) for h in heads]
            dist = (dists[0] * t + lax.broadcasted_iota(jnp.int32, (t, nk), 0)
                    - lax.broadcasted_iota(jnp.int32, (t, nk), 1))
            causal = dists[-1] == 0
            banded = windowed and dists[0] * t + t - 1 >= WINDOW
            assert not (causal and banded)
            keep = None
            if causal:
                keep = dist >= 0
            elif banded:
                keep = dist < WINDOW
            if keep is not None:
                s = [jnp.where(keep, s[h], MASK_BIAS) for h in heads]
        m_prev = [m_ref[h] for h in heads]
        m_new = [jnp.maximum(m_prev[h], jnp.max(s[h], axis=1, keepdims=True)) for h in heads]
        p = [jnp.exp(s[h] - jnp.tile(m_new[h], (1, nk // LANES))).astype(BF16) for h in heads]
        pv = [_dot(p[h], v) for h in heads]
        for h in heads:
            acc_ref[h] = jnp.exp(m_prev[h] - m_new[h]) * acc_ref[h] + pv[h]
            m_ref[h] = m_new[h]

    def near_steps(first):
        @pl.when(i >= first)
        def _():
            d = first
            if d % 2:
                tile_step(i - d, 1, (d,))
                d -= 1
            for d in range(d, 0, -2):
                tile_step(i - d, 2, (d, d - 1))

        @pl.when(i < first)
        def _():
            for d in range(first - 1, 0, -1):
                @pl.when(i >= d)
                def _():
                    tile_step(i - d, 1, (d,))
        tile_step(i, 1, (0,))

    if windowed:
        near_steps(WIN_TILES)
    else:
        n_far = jnp.maximum(i - (N_NEAR - 1), 0)
        far_tiles = FAR_KEYS // t

        def far_body(j, carry):
            tile_step(j * far_tiles, far_tiles)
            return carry
        n_far_steps = jnp.right_shift(n_far, _log2(far_tiles))
        lax.fori_loop(0, n_far_steps, far_body, 0)
        done = n_far_steps * far_tiles
        run = far_tiles // 2
        while run >= 1:
            has_run = jnp.bitwise_and(n_far, run) != 0

            @pl.when(has_run)
            def _():
                tile_step(done, run)
            done = done + jnp.where(has_run, run, 0)
            run //= 2
        near_steps(N_NEAR - 1)

    lane = lax.broadcasted_iota(jnp.int32, (t, LANES), 1)
    gates = gate_ref[0]
    outs = []
    for h in range(HEADS_PER_GROUP):
        col = _gate_column(g * HEADS_PER_GROUP + h, branch)
        gate = jnp.sum(jnp.where(lane == col, gates, 0.0), axis=1, keepdims=True)
        acc = acc_ref[h]
        outs.append(acc[:, :HEAD_DIM] / jnp.maximum(acc[:, HEAD_DIM:HEAD_DIM + 1], 1.0) * gate)
    o_ref[0] = jnp.concatenate(outs, axis=1).astype(BF16)


def _flash(gates, q_aug, kt, v, gv, *, windowed, branch):
    b, _, s, kd = q_aug.shape
    assert kt.shape[1] == kd
    t = T_ATT
    grid = (b, KV_GROUPS, s // t)
    return pl.pallas_call(
        functools.partial(_flash_kernel, windowed=windowed, branch=branch), grid=grid,
        in_specs=[
            pl.BlockSpec((1, t, LANES), lambda bi, g, i: (bi, i, 0)),
            pl.BlockSpec((1, HEADS_PER_GROUP, t, kd), lambda bi, g, i: (bi, g, i, 0)),
            pl.BlockSpec((1, kd, s), lambda bi, g, i: (bi, 0, 0)),
            pl.BlockSpec((1, 1, s, LANES), lambda bi, g, i: (bi, g, 0, 0)),
            pl.BlockSpec((1, HEADS_PER_GROUP, TOEPLITZ_ROWS, 2 * t), lambda bi, g, i: (g, 0, 0, 0)),
        ],
        out_specs=pl.BlockSpec((1, t, HEADS_PER_GROUP * HEAD_DIM), lambda bi, g, i: (bi, i, g)),
        out_shape=jax.ShapeDtypeStruct((b, s, NSA_WIDTH), BF16),
        scratch_shapes=[pltpu.VMEM((HEADS_PER_GROUP, t, LANES), F32),
                        pltpu.VMEM((HEADS_PER_GROUP, t, LANES), F32),
                        pltpu.VMEM((HEADS_PER_GROUP, WIN_TILES + 1 if windowed else N_NEAR, t, t),
                                   F32)],
        compiler_params=_params("parallel", "parallel", "arbitrary"),
        name="win_attention" if windowed else "sel_attention",
    )(gates, q_aug, kt, v, gv)


def _toeplitz_generators(rel_bias):
    t = T_ATT
    d = np.arange(TOEPLITZ_ROWS)[:, None]
    y = np.arange(2 * t)[None, :]
    dist = np.where(y < t, d * t - y, d * t + 2 * t - y)
    tab = _bias_lookup(rel_bias, _BUCKETS[np.clip(dist, 0, None)]) - rel_bias[REL_BUCKETS - 1]
    tab = jnp.where(jnp.asarray(dist >= 0)[..., None], tab, 0.0)
    return jnp.transpose(tab, (2, 0, 1)).reshape(KV_GROUPS, HEADS_PER_GROUP, TOEPLITZ_ROWS, 2 * t)


def _pool_tile(x, halo, i, w, scale):
    t = x.shape[0]
    halo = jnp.where(i > 0, halo, 0.0)
    xs = jnp.concatenate([halo, x], axis=0)
    sums = []
    acc = xs
    span = 1
    for win in POOL_WINDOWS:
        while span < win:
            acc = acc + pltpu.roll(acc, span, 0)
            span *= 2
        sums.append(acc[HALO:])
    lane_grp = jnp.right_shift(lax.broadcasted_iota(jnp.int32, (t, POOL_WIDTH), 1), _log2(POOL_CH))
    tok = i * t + lax.broadcasted_iota(jnp.int32, (t, POOL_WIDTH), 0)
    ssum = sums[-1]
    win = jnp.full((t, POOL_WIDTH), POOL_WINDOWS[-1], jnp.int32)
    for gi in range(len(POOL_WINDOWS) - 2, -1, -1):
        ssum = jnp.where(lane_grp == gi, sums[gi], ssum)
        win = jnp.where(lane_grp == gi, POOL_WINDOWS[gi], win)
    cnt = jnp.minimum(tok + 1, win).astype(F32)
    dlt = (ssum / cnt - x).astype(BF16)
    return (_dot(dlt, w) * scale).astype(BF16)


def _sgu_tile(u, v, norm_g, norm_b, ws_cat, bs_exp):
    t = u.shape[0]
    v = _gelu(v)
    mu = jnp.mean(v, axis=-1, keepdims=True)
    var = jnp.mean(jnp.square(v - mu), axis=-1, keepdims=True)
    vn = (v - mu) * lax.rsqrt(var + EPS) * norm_g + norm_b
    lane_grp = jnp.right_shift(lax.broadcasted_iota(jnp.int32, (SGU_CHUNK, SGU_WIDTH), 1),
                               _log2(SGU_CH))
    r = lax.broadcasted_iota(jnp.int32, (SGU_CHUNK, SGU_GROUPS * SGU_CHUNK), 0)
    c = lax.broadcasted_iota(jnp.int32, (SGU_CHUNK, SGU_GROUPS * SGU_CHUNK), 1) & (SGU_CHUNK - 1)
    ws = jnp.where(c <= r, ws_cat, 0.0).astype(BF16)
    outs = []
    for n in range(t // SGU_CHUNK):
        vc = vn[n * SGU_CHUNK:(n + 1) * SGU_CHUNK]
        stacked = jnp.concatenate(
            [jnp.where(lane_grp == gi, vc, 0.0) for gi in range(SGU_GROUPS)], axis=0).astype(BF16)
        mixed = _dot(ws, stacked) + bs_exp
        outs.append((_gelu(u[n * SGU_CHUNK:(n + 1) * SGU_CHUNK]) * mixed).astype(BF16))
    return jnp.concatenate(outs, axis=0)


def _out_proj_kernel(oc_ref, os_ref, ow_ref, pin_ref, halo_ref, u_ref, v_ref, pw_ref, ps_ref,
                     ng_ref, nb_ref, ws_ref, bs_ref, w_ref, g_ref, x_ref, y_ref):
    a = (oc_ref[0].astype(F32) + os_ref[0].astype(F32) + ow_ref[0].astype(F32)).astype(BF16)
    pool = _pool_tile(pin_ref[0], halo_ref[0], pl.program_id(1), pw_ref[...], ps_ref[...])
    sgu = _sgu_tile(u_ref[0], v_ref[0], ng_ref[...], nb_ref[...], ws_ref[...], bs_ref[...])
    mix = (_dot(a, w_ref[:NSA_WIDTH, :])
           + _dot(pool, w_ref[NSA_WIDTH:NSA_WIDTH + POOL_WIDTH, :])
           + _dot(sgu, w_ref[NSA_WIDTH + POOL_WIDTH:, :]))
    y_ref[0] = x_ref[0] + _rms(mix, g_ref[...])


def _out_proj(o_cmp, o_sel, o_win, pin, u, v, pool_w, pool_scale, sgu_g, sgu_b, ws_cat, bs_exp,
              w, g, x, l):
    b, s, d = x.shape
    t = T_OUT
    assert t % SGU_CHUNK == 0 and t % HALO == 0
    row = lambda n: pl.BlockSpec((1, t, n), lambda bi, i: (bi, i, 0))
    halo = pl.BlockSpec((1, HALO, POOL_WIDTH),
                        lambda bi, i: (bi, jnp.maximum(i * (t // HALO) - 1, 0), 0))
    layer = [_layer_spec(a, l) for a in (pool_w, pool_scale, sgu_g, sgu_b, ws_cat, bs_exp, w, g)]
    return pl.pallas_call(
        _out_proj_kernel, grid=(b, s // t),
        in_specs=[row(NSA_WIDTH), row(NSA_WIDTH), row(NSA_WIDTH), row(POOL_WIDTH), halo,
                  row(SGU_WIDTH), row(SGU_WIDTH)] + layer + [row(d)],
        out_specs=row(d), out_shape=jax.ShapeDtypeStruct((b, s, d), F32),
        compiler_params=_params("parallel", "parallel"), name="out_proj",
    )(o_cmp, o_sel, o_win, pin, pin, u, v, pool_w, pool_scale, sgu_g, sgu_b, ws_cat, bs_exp, w, g, x)


def _mem_kv_kernel(mem_ref, g_ref, wk_ref, wv_ref, kt_ref, v_ref):
    m = _rms(mem_ref[0], g_ref[...]).astype(BF16)
    kt_ref[0] = _dot(m, wk_ref[...]).T.astype(BF16)
    v_ref[0] = _dot(m, wv_ref[...]).astype(BF16)


def _mem_kv(mem, g, wk, wv, l):
    b, m, d = mem.shape
    blk = pl.BlockSpec((1, m, d), lambda bi: (bi, 0, 0))
    blk_t = pl.BlockSpec((1, d, m), lambda bi: (bi, 0, 0))
    return pl.pallas_call(
        _mem_kv_kernel, grid=(b,),
        in_specs=[blk] + [_layer_spec(a, l) for a in (g, wk, wv)],
        out_specs=(blk_t, blk),
        out_shape=(jax.ShapeDtypeStruct((b, d, m), BF16), jax.ShapeDtypeStruct((b, m, d), BF16)),
        compiler_params=_params("parallel"), name="mem_kv",
    )(mem, g, wk, wv)


def _mem_attn_kernel(x_ref, gpre_ref, wq_ref, kt_ref, v_ref, wo_ref, gpost_ref, y_ref):
    x = x_ref[0]
    h = _rms(x, gpre_ref[...]).astype(BF16)
    q = (_dot(h, wq_ref[...]) * (MEM_HEAD_DIM ** -0.5)).astype(BF16)
    heads = range(MEM_HEADS)
    sl = [slice(hd * MEM_HEAD_DIM, (hd + 1) * MEM_HEAD_DIM) for hd in heads]
    s = [_dot(q[:, sl[hd]], kt_ref[0, sl[hd], :]) for hd in heads]
    e = [jnp.exp(s[hd] - jnp.max(s[hd], axis=-1, keepdims=True)) for hd in heads]
    p = [(e[hd] * (1.0 / jnp.sum(e[hd], axis=-1, keepdims=True))).astype(BF16) for hd in heads]
    o = jnp.concatenate([_dot(p[hd], v_ref[0, :, sl[hd]]).astype(BF16) for hd in heads], axis=1)
    y_ref[0] = x + _rms(_dot(o, wo_ref[...]), gpost_ref[...])


def _mem_attn(x, gpre, wq, kt, v, wo, gpost, l):
    b, s, d = x.shape
    m = v.shape[1]
    t = T_MEM
    row = pl.BlockSpec((1, t, d), lambda bi, i: (bi, i, 0))
    return pl.pallas_call(
        _mem_attn_kernel, grid=(b, s // t),
        in_specs=[row, _layer_spec(gpre, l), _layer_spec(wq, l),
                  pl.BlockSpec((1, d, m), lambda bi, i: (bi, 0, 0)),
                  pl.BlockSpec((1, m, d), lambda bi, i: (bi, 0, 0)),
                  _layer_spec(wo, l), _layer_spec(gpost, l)],
        out_specs=row, out_shape=jax.ShapeDtypeStruct((b, s, d), F32),
        compiler_params=_params("parallel", "parallel"), name="mem_attention",
    )(x, gpre, wq, kt, v, wo, gpost)


def _ffn_kernel(x_ref, halo_ref, gpre_ref, wup_ref, cw_ref, cb_ref, wd_ref, gpost_ref, y_ref,
                act_ref):
    i = pl.program_id(1)
    f = wd_ref.shape[0]
    halo = jnp.where(i > 0, halo_ref[0], 0.0)
    h = _rms(jnp.concatenate([halo, x_ref[0]], axis=0), gpre_ref[...]).astype(BF16)

    def conv(off):
        a = _dot(h, wup_ref[:, off:off + F_CHUNK])
        cw = cw_ref[:, off:off + F_CHUNK]
        return (cw[0:1] * pltpu.roll(a, 2, 0) + cw[1:2] * pltpu.roll(a, 1, 0) + cw[2:3] * a
                + cb_ref[:, off:off + F_CHUNK])

    for off in range(0, f, F_CHUNK):
        act = _gelu(conv(off)) * conv(f + off)
        act_ref[:, off:off + F_CHUNK] = act[HALO:].astype(BF16)
    y_ref[0] = x_ref[0] + _rms(_dot(act_ref[...], wd_ref[...]), gpost_ref[...])


def _ffn(x, gpre, w_up, conv_w, conv_b, w_down, gpost, l):
    b, s, d = x.shape
    f = w_down.shape[1]
    t = T_FFN
    assert f % F_CHUNK == 0
    resident = lambda a: _layer_spec(a, l, pipeline_mode=pl.Buffered(1))
    return pl.pallas_call(
        _ffn_kernel, grid=(b, s // t),
        in_specs=[
            pl.BlockSpec((1, t, d), lambda bi, i: (bi, i, 0)),
            pl.BlockSpec((1, HALO, d), lambda bi, i: (bi, jnp.maximum(i * (t // HALO) - 1, 0), 0)),
            resident(gpre), resident(w_up), resident(conv_w), resident(conv_b), resident(w_down),
            resident(gpost),
        ],
        out_specs=pl.BlockSpec((1, t, d), lambda bi, i: (bi, i, 0)),
        out_shape=jax.ShapeDtypeStruct((b, s, d), F32),
        scratch_shapes=[pltpu.VMEM((t, f), BF16)],
        compiler_params=_params("parallel", "parallel"), name="conv_ffn",
    )(x, x, gpre, w_up, conv_w, conv_b, w_down, gpost)


def kernel(x, mem, rel_bias, mix_norm_pre, mix_norm_post, w_in, cmp_pos, cmp_w1, cmp_b1, cmp_w2,
           cmp_b2, pool_w, pool_scale, sgu_norm_g, sgu_norm_b, sgu_w, sgu_b, w_out, mem_norm_pre,
           mem_norm_kv, mem_norm_post, w_mq, w_mk, w_mv, w_mo, ffn_norm_pre, ffn_norm_post, w_up,
           conv_w, conv_b, w_down):
    depth = w_in.shape[0]
    s = x.shape[1]
    assert s % max(T_IN, T_CMP, T_ATT, T_OUT, T_MEM, T_FFN) == 0
    nb = s // CMP_STRIDE

    w_in_p = _in_proj_weights(w_in)
    band, wagg = _cmp_tables(rel_bias, nb)
    toep = _toeplitz_generators(rel_bias)
    row = lambda a: a[:, None, :]
    bf = lambda a: a.astype(BF16)
    w_out_b, w_mq_b, w_mk_b, w_mv_b, w_mo_b = bf(w_out), bf(w_mq), bf(w_mk), bf(w_mv), bf(w_mo)
    w_up_b, w_down_b = bf(w_up), bf(w_down)
    g_mix_pre, g_mix_post = row(mix_norm_pre), row(mix_norm_post)
    g_mem_pre, g_mem_kv, g_mem_post = row(mem_norm_pre), row(mem_norm_kv), row(mem_norm_post)
    g_ffn_pre, g_ffn_post, conv_b_r = row(ffn_norm_pre), row(ffn_norm_post), row(conv_b)
    cw = jax.vmap(_compress_weights)(cmp_pos, cmp_w1, cmp_b1, cmp_w2, cmp_b2)
    n_pool = len(POOL_WINDOWS)
    pool_blk = bf(jnp.einsum("lgcd,gh->lgchd", pool_w, jnp.eye(n_pool, dtype=F32)).reshape(
        depth, POOL_WIDTH, POOL_WIDTH))
    pool_scale_r = row(pool_scale)
    sgu_g, sgu_bias = row(sgu_norm_g), row(sgu_norm_b)
    ws_cat = jnp.concatenate([sgu_w[:, gi] for gi in range(SGU_GROUPS)], axis=2)
    bs_exp = jnp.repeat(jnp.swapaxes(sgu_b, 1, 2), SGU_CH, axis=2)

    for l in range(depth):
        qa, ksel_t, kwin_t, kcmp, vcmp, vsel, vwin, gates, pin, u, v = _in_proj(
            x, g_mix_pre, w_in_p, l)
        kc, vct = _compress(kcmp, vcmp, cw, l)
        o_cmp, q_aug = _cmp_attention(gates, qa, kc, vct, band, wagg)
        o_sel = _flash(gates, q_aug, ksel_t, vsel, toep, windowed=False, branch=1)
        o_win = _flash(gates, qa, kwin_t, vwin, toep, windowed=True, branch=2)
        x = _out_proj(o_cmp, o_sel, o_win, pin, u, v, pool_blk, pool_scale_r, sgu_g, sgu_bias,
                      ws_cat, bs_exp, w_out_b, g_mix_post, x, l)
        km, vm = _mem_kv(mem, g_mem_kv, w_mk_b, w_mv_b, l)
        x = _mem_attn(x, g_mem_pre, w_mq_b, km, vm, w_mo_b, g_mem_post, l)
        x = _ffn(x, g_ffn_pre, w_up_b, conv_w, conv_b_r, w_down_b, g_ffn_post, l)
    return x
```
